```python
import jax
import jax.numpy as jnp
from jax import lax
import numpy as np

D_MODEL = 4096
BATCH = 1
SEQ = 8192
DEPTH = 1

HEAD_DIM = 128
ATTN_WIDTH = D_MODEL // 2
CONV_WIDTH = D_MODEL - ATTN_WIDTH
N_Q_HEADS = ATTN_WIDTH // HEAD_DIM
N_KV_HEADS = N_Q_HEADS // 4
KV_WIDTH = N_KV_HEADS * HEAD_DIM
CONV_GROUPS = CONV_WIDTH // HEAD_DIM
CONV_K = 3
CMP_BLOCK = 32
CMP_STRIDE = 16
SLC_BLOCK = 64
N_SELECT = 16
N_LOCAL_FORCED = 2
WINDOW = 512
BAND_BLOCK = 128
QCHUNK = 64
N_MEM = 256
MEM_HEADS = 4
MEM_HEAD_DIM = 128
MEM_WIDTH = MEM_HEADS * MEM_HEAD_DIM
D_FF = 4 * D_MODEL
RMS_EPS = 1e-6
NEG_INF = -1e30
FORCED_SCORE = 1e9
SPLIT_POINTS = (ATTN_WIDTH, ATTN_WIDTH + 6 * KV_WIDTH, ATTN_WIDTH + 6 * KV_WIDTH + 3 * N_Q_HEADS)
IN_PROJ_WIDTH = ATTN_WIDTH + 6 * KV_WIDTH + 3 * N_Q_HEADS + 3 * CONV_WIDTH

kernel_name = 'hybrid_nsa_shortconv_alibi_sandwich'


def rms_norm(x, gain):
    xf = x.astype(jnp.float32)
    y = xf * lax.rsqrt(jnp.mean(xf * xf, axis=-1, keepdims=True) + RMS_EPS)
    return (y * gain.astype(jnp.float32)).astype(x.dtype)


def alibi_slopes(n_heads):
    i = jnp.arange(1, n_heads + 1, dtype=jnp.float32)
    return jnp.exp2(-8.0 * i / n_heads)


def masked_softmax(s, mask, axis=-1):
    p = jax.nn.softmax(jnp.where(mask, s, NEG_INF), axis=axis)
    return jnp.where(mask, p, 0.0)


def compress_tokens(kv, pe, w1, w2):
    b, t, g, d = kv.shape
    nc = (t - CMP_BLOCK) // CMP_STRIDE + 1
    idx = jnp.arange(nc)[:, None] * CMP_STRIDE + jnp.arange(CMP_BLOCK)[None, :]
    blocks = kv[:, idx] + pe[:, None, :]
    flat = jnp.transpose(blocks, (0, 3, 1, 2, 4)).reshape(b, g, nc, CMP_BLOCK * d)
    return jax.nn.gelu(flat @ w1) @ w2


def cmp_slc_attention(q, k_cmp, v_cmp, k_slc, v_slc, slopes):
    b, t, h, d = q.shape
    g = N_KV_HEADS
    hpg = h // g
    nc = k_cmp.shape[2]
    nb = t // SLC_BLOCK
    n_sel = min(N_SELECT, nb)
    nq = t // QCHUNK
    scale = d ** -0.5
    ks_blk = k_slc.reshape(b, nb, SLC_BLOCK, g, d).transpose(0, 3, 1, 2, 4)
    vs_blk = v_slc.reshape(b, nb, SLC_BLOCK, g, d).transpose(0, 3, 1, 2, 4)
    cmp_start = jnp.arange(nc) * CMP_STRIDE
    cmp_end = cmp_start + CMP_BLOCK - 1
    blk = jnp.arange(nb)
    slc_start = blk * SLC_BLOCK
    overlap = ((cmp_start[:, None] <= slc_start[None, :] + SLC_BLOCK - 1)
               & (cmp_end[:, None] >= slc_start[None, :])).astype(jnp.float32)
    slopes5 = slopes.reshape(g, hpg)[None, :, :, None, None]
    slopes6 = slopes.reshape(g, hpg)[None, :, :, None, None, None]
    b_ix = jnp.arange(b)[:, None, None, None]
    g_ix = jnp.arange(g)[None, :, None, None]
    qc = q.reshape(b, nq, QCHUNK, g, hpg, d).transpose(1, 0, 3, 4, 2, 5)

    def chunk(args):
        q_blk, ci = args
        tpos = ci * QCHUNK + jnp.arange(QCHUNK)
        s = jnp.einsum('bghqd,bgcd->bghqc', q_blk, k_cmp, preferred_element_type=jnp.float32) * scale
        s = s - slopes5 * (tpos[:, None] - cmp_end[None, :]).astype(jnp.float32)
        p_cmp = masked_softmax(s, cmp_end[None, :] <= tpos[:, None])
        o_cmp = jnp.einsum('bghqc,bgcd->bghqd', p_cmp.astype(v_cmp.dtype), v_cmp)
        imp = jnp.einsum('bgqc,cn->bgqn', p_cmp.sum(axis=2), overlap)
        cur = tpos // SLC_BLOCK
        valid = slc_start[None, :] <= tpos[:, None]
        back = cur[:, None] - blk[None, :]
        forced = (blk[None, :] == 0) | ((back >= 0) & (back < N_LOCAL_FORCED))
        imp = jnp.where(forced & valid, FORCED_SCORE, imp)
        imp = jnp.where(valid, imp, NEG_INF)
        top_val, top_idx = lax.top_k(imp, n_sel)
        sel_ok = top_val > 0.5 * NEG_INF
        k_sel = ks_blk[b_ix, g_ix, top_idx]
        v_sel = vs_blk[b_ix, g_ix, top_idx]
        kpos = top_idx[..., None] * SLC_BLOCK + jnp.arange(SLC_BLOCK)
        diff = tpos[:, None, None] - kpos
        s2 = jnp.einsum('bghqd,bgqnkd->bghqnk', q_blk, k_sel, preferred_element_type=jnp.float32) * scale
        s2 = s2 - slopes6 * diff[:, :, None].astype(jnp.float32)
        mask2 = ((diff >= 0) & sel_ok[..., None])[:, :, None]
        p_slc = masked_softmax(s2, mask2, axis=(-2, -1))
        o_slc = jnp.einsum('bghqnk,bgqnkd->bghqd', p_slc.astype(v_sel.dtype), v_sel)
        return o_cmp, o_slc

    o_cmp, o_slc = lax.map(chunk, (qc, jnp.arange(nq)))
    o_cmp = o_cmp.transpose(1, 0, 4, 2, 3, 5).reshape(b, t, h, d)
    o_slc = o_slc.transpose(1, 0, 4, 2, 3, 5).reshape(b, t, h, d)
    return o_cmp, o_slc


def window_attention(q, k, v, slopes):
    b, t, h, d = q.shape
    g = N_KV_HEADS
    hpg = h // g
    nqb = t // BAND_BLOCK
    nband = WINDOW // BAND_BLOCK
    pad = ((0, 0), (WINDOW, 0), (0, 0), (0, 0))
    kp = jnp.pad(k, pad).reshape(b, nqb + nband, BAND_BLOCK, g, d)
    vp = jnp.pad(v, pad).reshape(b, nqb + nband, BAND_BLOCK, g, d)
    kb = jnp.concatenate([kp[:, j:j + nqb] for j in range(nband + 1)], axis=2)
    vb = jnp.concatenate([vp[:, j:j + nqb] for j in range(nband + 1)], axis=2)
    qb = q.reshape(b, nqb, BAND_BLOCK, g, hpg, d)
    tpos = jnp.arange(t).reshape(nqb, BAND_BLOCK)
    spos = (jnp.arange(nqb)[:, None] - nband) * BAND_BLOCK + jnp.arange((nband + 1) * BAND_BLOCK)[None, :]
    diff = tpos[:, :, None] - spos[:, None, :]
    mask = (diff >= 0) & (diff < WINDOW) & (spos[:, None, :] >= 0)
    s = jnp.einsum('bnqghd,bnkgd->bnghqk', qb, kb, preferred_element_type=jnp.float32) * (d ** -0.5)
    s = s - slopes.reshape(g, hpg)[None, None, :, :, None, None] * diff[None, :, None, None].astype(jnp.float32)
    p = masked_softmax(s, mask[None, :, None, None])
    o = jnp.einsum('bnghqk,bnkgd->bnqghd', p.astype(vb.dtype), vb)
    return o.reshape(b, t, h, d)


def short_conv(b_gate, c_gate, hval, conv_w):
    u = c_gate * hval
    conv = lax.conv_general_dilated(u, conv_w[:, None, :].astype(u.dtype), window_strides=(1,),
                                    padding=[(CONV_K - 1, 0)], dimension_numbers=('NWC', 'WIO', 'NWC'),
                                    feature_group_count=u.shape[-1])
    return b_gate * conv


def mixer_sublayer(x, norm_pre, norm_post, w_in, cmp_k_pe, cmp_k_w1, cmp_k_w2,
                   cmp_v_pe, cmp_v_w1, cmp_v_w2, conv_w, w_out):
    b, t, _ = x.shape
    slopes = alibi_slopes(N_Q_HEADS)
    proj = rms_norm(x, norm_pre) @ w_in
    q, kv, gate_logits, conv_in = jnp.split(proj, SPLIT_POINTS, axis=-1)
    q = q.reshape(b, t, N_Q_HEADS, HEAD_DIM)
    kc, vc, ks, vs, kw, vw = [a.reshape(b, t, N_KV_HEADS, HEAD_DIM) for a in jnp.split(kv, 6, axis=-1)]
    k_cmp = compress_tokens(kc, cmp_k_pe, cmp_k_w1, cmp_k_w2)
    v_cmp = compress_tokens(vc, cmp_v_pe, cmp_v_w1, cmp_v_w2)
    o_cmp, o_slc = cmp_slc_attention(q, k_cmp, v_cmp, ks, vs, slopes)
    o_win = window_attention(q, kw, vw, slopes)
    gates = jax.nn.sigmoid(gate_logits.astype(jnp.float32)).reshape(b, t, 3, N_Q_HEADS)[..., None]
    o_attn = (gates[:, :, 0] * o_cmp + gates[:, :, 1] * o_slc + gates[:, :, 2] * o_win).astype(x.dtype)
    b_gate, c_gate, hval = jnp.split(conv_in, 3, axis=-1)
    o_conv = short_conv(b_gate, c_gate, hval, conv_w)
    y = jnp.concatenate([o_attn.reshape(b, t, ATTN_WIDTH), o_conv], axis=-1) @ w_out
    return x + rms_norm(y, norm_post)


def memory_sublayer(x, mem, norm_pre, norm_kv, norm_post, wq, wk, wv, wo):
    b, t, _ = x.shape
    m = mem.shape[1]
    mn = rms_norm(mem, norm_kv)
    qm = (rms_norm(x, norm_pre) @ wq).reshape(b, t, MEM_HEADS, MEM_HEAD_DIM)
    km = (mn @ wk).reshape(b, m, MEM_HEADS, MEM_HEAD_DIM)
    vm = (mn @ wv).reshape(b, m, MEM_HEADS, MEM_HEAD_DIM)
    s = jnp.einsum('bthd,bmhd->bhtm', qm, km, preferred_element_type=jnp.float32) * (MEM_HEAD_DIM ** -0.5)
    p = jax.nn.softmax(s, axis=-1)
    o = jnp.einsum('bhtm,bmhd->bthd', p.astype(vm.dtype), vm).reshape(b, t, MEM_WIDTH) @ wo
    return x + rms_norm(o, norm_post)


def mlp_sublayer(x, norm_pre, norm_post, w_up, w_down):
    hid = jnp.square(jax.nn.relu(rms_norm(x, norm_pre) @ w_up))
    return x + rms_norm(hid @ w_down, norm_post)


def setup_inputs(seed: int = 0) -> dict:
    key = jax.random.key(seed)
    ks = jax.random.split(key, 26)

    def dense(k, shape, fan_in):
        return jax.random.normal(k, shape, jnp.float32) * (fan_in ** -0.5)

    def gain(k):
        return 1.0 + 0.02 * jax.random.normal(k, (DEPTH, D_MODEL), jnp.float32)

    return {
        'x': jax.random.normal(ks[0], (BATCH, SEQ, D_MODEL), jnp.float32),
        'mem': jax.random.normal(ks[1], (BATCH, N_MEM, D_MODEL), jnp.float32),
        'mix_norm_pre': gain(ks[2]),
        'mix_norm_post': gain(ks[3]),
        'w_in': dense(ks[4], (DEPTH, D_MODEL, IN_PROJ_WIDTH), D_MODEL),
        'cmp_k_pe': 0.02 * jax.random.normal(ks[5], (DEPTH, CMP_BLOCK, HEAD_DIM), jnp.float32),
        'cmp_k_w1': dense(ks[6], (DEPTH, CMP_BLOCK * HEAD_DIM, HEAD_DIM), CMP_BLOCK * HEAD_DIM),
        'cmp_k_w2': dense(ks[7], (DEPTH, HEAD_DIM, HEAD_DIM), HEAD_DIM),
        'cmp_v_pe': 0.02 * jax.random.normal(ks[8], (DEPTH, CMP_BLOCK, HEAD_DIM), jnp.float32),
        'cmp_v_w1': dense(ks[9], (DEPTH, CMP_BLOCK * HEAD_DIM, HEAD_DIM), CMP_BLOCK * HEAD_DIM),
        'cmp_v_w2': dense(ks[10], (DEPTH, HEAD_DIM, HEAD_DIM), HEAD_DIM),
        'conv_w': dense(ks[11], (DEPTH, CONV_K, CONV_WIDTH), CONV_K),
        'w_out': dense(ks[12], (DEPTH, D_MODEL, D_MODEL), D_MODEL),
        'mem_norm_pre': gain(ks[13]),
        'mem_norm_kv': gain(ks[14]),
        'mem_norm_post': gain(ks[15]),
        'w_mem_q': dense(ks[16], (DEPTH, D_MODEL, MEM_WIDTH), D_MODEL),
        'w_mem_k': dense(ks[17], (DEPTH, D_MODEL, MEM_WIDTH), D_MODEL),
        'w_mem_v': dense(ks[18], (DEPTH, D_MODEL, MEM_WIDTH), D_MODEL),
        'w_mem_o': dense(ks[19], (DEPTH, MEM_WIDTH, D_MODEL), MEM_WIDTH),
        'mlp_norm_pre': gain(ks[20]),
        'mlp_norm_post': gain(ks[21]),
        'w_up': dense(ks[22], (DEPTH, D_MODEL, D_FF), D_MODEL),
        'w_down': dense(ks[23], (DEPTH, D_FF, D_MODEL), D_FF),
    }


def reference(x, mem, mix_norm_pre, mix_norm_post, w_in, cmp_k_pe, cmp_k_w1, cmp_k_w2,
              cmp_v_pe, cmp_v_w1, cmp_v_w2, conv_w, w_out, mem_norm_pre, mem_norm_kv,
              mem_norm_post, w_mem_q, w_mem_k, w_mem_v, w_mem_o, mlp_norm_pre, mlp_norm_post,
              w_up, w_down):
    for l in range(DEPTH):
        x = mixer_sublayer(x, mix_norm_pre[l], mix_norm_post[l], w_in[l], cmp_k_pe[l], cmp_k_w1[l],
                           cmp_k_w2[l], cmp_v_pe[l], cmp_v_w1[l], cmp_v_w2[l], conv_w[l], w_out[l])
        x = memory_sublayer(x, mem, mem_norm_pre[l], mem_norm_kv[l], mem_norm_post[l],
                            w_mem_q[l], w_mem_k[l], w_mem_v[l], w_mem_o[l])
        x = mlp_sublayer(x, mlp_norm_pre[l], mlp_norm_post[l], w_up[l], w_down[l])
    return x
```

```python
import functools

import jax
import jax.numpy as jnp
from jax import lax
from jax.experimental import pallas as pl
from jax.experimental.pallas import tpu as pltpu

HEAD_DIM = 128
CMP_BLOCK = 32
CMP_STRIDE = 16
SLC_BLOCK = 64
N_SELECT = 16
N_LOCAL_FORCED = 2
WINDOW = 512
CONV_K = 3
MEM_HEADS = 4
MEM_HEAD_DIM = 128
RMS_EPS = 1e-6
NEG_INF = -1e30
FORCED_SCORE = 1e9

V7X_VMEM_BYTES = 64 * 1024 * 1024
VMEM_LIMIT = V7X_VMEM_BYTES - 8 * 1024 * 1024
LANES = 128
SUBLANES = 8

QB = 128
KB = 128

F32 = jnp.float32
BF16 = jnp.bfloat16

_NT = (((1,), (1,)), ((), ()))


def _tile(n, pref):
    t = min(n, pref)
    while n % t:
        t -= 1
    return t


def _params(*sem):
    return pltpu.CompilerParams(dimension_semantics=sem, vmem_limit_bytes=VMEM_LIMIT)


def _rms(x, gain):
    return x * lax.rsqrt(jnp.mean(x * x, axis=-1, keepdims=True) + RMS_EPS) * gain


def _rmsnorm_kernel(x_ref, g_ref, o_ref):
    o_ref[...] = _rms(x_ref[...].astype(F32), g_ref[...]).astype(o_ref.dtype)


def rmsnorm_cast(x, gain, out_dtype=BF16):
    m, d = x.shape
    tm = _tile(m, 256)
    return pl.pallas_call(
        _rmsnorm_kernel,
        grid=(m // tm,),
        in_specs=[pl.BlockSpec((tm, d), lambda i: (i, 0)), pl.BlockSpec((1, d), lambda i: (0, 0))],
        out_specs=pl.BlockSpec((tm, d), lambda i: (i, 0)),
        out_shape=jax.ShapeDtypeStruct((m, d), out_dtype),
        compiler_params=_params("parallel"),
    )(x, gain.reshape(1, d))


def _mm_kernel(a_ref, w_ref, o_ref, *, scale):
    acc = jnp.dot(a_ref[...], w_ref[...], preferred_element_type=F32)
    if scale is not None:
        acc = acc * scale
    o_ref[...] = acc.astype(o_ref.dtype)


def matmul(a, w, out_dtype, scale=None, tm=1024, tn=1024):
    m, k = a.shape
    _, n = w.shape
    tm, tn = _tile(m, tm), _tile(n, tn)
    return pl.pallas_call(
        functools.partial(_mm_kernel, scale=scale),
        grid=(m // tm, n // tn),
        in_specs=[pl.BlockSpec((tm, k), lambda i, j: (i, 0)), pl.BlockSpec((k, tn), lambda i, j: (0, j))],
        out_specs=pl.BlockSpec((tm, tn), lambda i, j: (i, j)),
        out_shape=jax.ShapeDtypeStruct((m, n), out_dtype),
        compiler_params=_params("parallel", "arbitrary"),
    )(a, w)


def _mm_postnorm_kernel(a_ref, w_ref, x_ref, g_ref, o_ref, *, nk):
    k = pl.program_id(1)

    @pl.when(k == 0)
    def _():
        o_ref[...] = jnp.zeros_like(o_ref)

    o_ref[...] += jnp.dot(a_ref[...], w_ref[...], preferred_element_type=F32)

    @pl.when(k == nk - 1)
    def _():
        o_ref[...] = x_ref[...] + _rms(o_ref[...], g_ref[...])


def matmul_postnorm_residual(a, w, x, gain, tm=256, tk=1024):
    m, kd = a.shape
    _, n = w.shape
    tm, tk = _tile(m, tm), _tile(kd, tk)
    nk = kd // tk
    return pl.pallas_call(
        functools.partial(_mm_postnorm_kernel, nk=nk),
        grid=(m // tm, nk),
        in_specs=[
            pl.BlockSpec((tm, tk), lambda i, k: (i, k)),
            pl.BlockSpec((tk, n), lambda i, k: (k, 0)),
            pl.BlockSpec((tm, n), lambda i, k: (i, 0)),
            pl.BlockSpec((1, n), lambda i, k: (0, 0)),
        ],
        out_specs=pl.BlockSpec((tm, n), lambda i, k: (i, 0)),
        out_shape=jax.ShapeDtypeStruct((m, n), F32),
        compiler_params=_params("parallel", "arbitrary"),
    )(a, w, x, gain.reshape(1, n))


def _gelu_tanh(x):
    c = 0.7978845608028654
    return x * (0.5 * (1.0 + jnp.tanh(c * (x + 0.044715 * (x * x * x)))))


def _compress_kernel(c_ref, pe_ref, w1_ref, w2_ref, o_ref, *, half):
    c = c_ref[...]
    pe = pe_ref[...]
    top = (c + pe[0:1, :]).astype(BF16)
    bot = (c + pe[1:2, :]).astype(BF16)
    a = jnp.dot(top, w1_ref[0:half, :], preferred_element_type=F32)
    b = jnp.dot(bot, w1_ref[half:2 * half, :], preferred_element_type=F32)
    ncp = c.shape[0]
    hid = _gelu_tanh(a + pltpu.roll(b, ncp - 1, 0))
    o_ref[...] = jnp.dot(hid.astype(BF16), w2_ref[...], preferred_element_type=F32).astype(o_ref.dtype)


def compress_tokens(chunks, pe2, w1, w2):
    two, g, ncp, half = chunks.shape
    return pl.pallas_call(
        functools.partial(_compress_kernel, half=half),
        grid=(two, g),
        in_specs=[
            pl.BlockSpec((None, None, ncp, half), lambda s, j: (s, j, 0, 0)),
            pl.BlockSpec((None, 2, half), lambda s, j: (s, 0, 0)),
            pl.BlockSpec((None, 2 * half, HEAD_DIM), lambda s, j: (s, 0, 0)),
            pl.BlockSpec((None, HEAD_DIM, HEAD_DIM), lambda s, j: (s, 0, 0)),
        ],
        out_specs=pl.BlockSpec((None, None, ncp, HEAD_DIM), lambda s, j: (s, j, 0, 0)),
        out_shape=jax.ShapeDtypeStruct((two, g, ncp, HEAD_DIM), BF16),
        compiler_params=_params("parallel", "parallel"),
    )(chunks, pe2, w1, w2)


def _attn_kernel(slopes_ref, q_ref, kcmp_ref, vcmpT_ref, ks_ref, vsT_ref, kw_ref, vwT_ref, gl_ref,
                 o_ref, p_scr, imp_scr, sel_scr, ocmp_scr, acc_slc, acc_win,
                 *, hpg, nc, ncp, nb, n_sel):
    g = pl.program_id(0)
    i = pl.program_id(1)
    t0 = i * QB
    w = hpg * QB
    ninf = -jnp.inf

    q = q_ref[...]
    qs = jnp.concatenate([q[:, h * HEAD_DIM:(h + 1) * HEAD_DIM] for h in range(hpg)], axis=0)
    slopes = [slopes_ref[g * hpg + h] for h in range(hpg)]

    s = lax.dot_general(kcmp_ref[...], qs, _NT, preferred_element_type=F32)
    c_iota = lax.broadcasted_iota(jnp.int32, (ncp, QB), 0)
    q_iota = lax.broadcasted_iota(jnp.int32, (ncp, QB), 1)
    cmp_end = c_iota * CMP_STRIDE + (CMP_BLOCK - 1)
    mask_c = (cmp_end <= t0 + q_iota) & (c_iota < nc)
    rel_c = (cmp_end - t0).astype(F32)
    psum = jnp.zeros((ncp, QB), F32)
    for h in range(hpg):
        sh = s[:, h * QB:(h + 1) * QB] + slopes[h] * rel_c
        sh = jnp.where(mask_c, sh, ninf)
        m = jnp.maximum(jnp.max(sh, axis=0, keepdims=True), NEG_INF)
        p = jnp.exp(sh - m)
        l = jnp.sum(p, axis=0, keepdims=True)
        pn = p * jnp.where(l > 0.0, 1.0 / l, 0.0)
        psum = psum + pn
        ocmp_scr[:, h * QB:(h + 1) * QB] = jnp.dot(vcmpT_ref[...], pn.astype(BF16),
                                                    preferred_element_type=F32)

    p_scr[0:SUBLANES, :] = jnp.zeros((SUBLANES, QB), F32)
    p_scr[SUBLANES:SUBLANES + ncp, :] = psum
    imp = p_scr[pl.ds(SUBLANES - 1, nb, stride=4), :]
    for j in range(4):
        imp = imp + p_scr[pl.ds(SUBLANES + j, nb, stride=4), :]
    n_iota = lax.broadcasted_iota(jnp.int32, (nb, QB), 0)
    tq = t0 + lax.broadcasted_iota(jnp.int32, (nb, QB), 1)
    cur = lax.shift_right_logical(tq, 6)
    valid = n_iota <= cur
    back = cur - n_iota
    forced = (n_iota == 0) | ((back >= 0) & (back < N_LOCAL_FORCED))
    imp = jnp.where(forced & valid, FORCED_SCORE, imp)
    imp = jnp.where(valid, imp, NEG_INF)
    imp_scr[...] = imp

    def rank_body(mi, cnt):
        vm = jnp.broadcast_to(imp_scr[pl.ds(mi, 1), :], (nb, QB))
        v = imp_scr[...]
        beats = (vm > v) | ((vm == v) & (mi < n_iota))
        return cnt + jnp.where(beats, 1.0, 0.0)

    cnt = lax.fori_loop(0, nb, rank_body, jnp.zeros((nb, QB), F32))
    sel = (cnt < float(n_sel)) & valid
    sel_scr[...] = jnp.where(sel, 0.0, ninf)

    r_iota = lax.broadcasted_iota(jnp.int32, (KB, QB), 0)
    k_iota = lax.broadcasted_iota(jnp.int32, (KB, QB), 1)
    bps = KB // SLC_BLOCK

    def flash_tile(k_ref, vT_ref, acc_ref, kt, base_of, carry):
        ms, ls = carry
        k0 = pl.multiple_of(kt * KB, KB)
        st = lax.dot_general(k_ref[pl.ds(k0, KB), :], qs, _NT, preferred_element_type=F32)
        diff = (t0 - k0) + (k_iota - r_iota)
        base = base_of(kt, diff)
        rel = (-diff).astype(F32)
        new_m, new_l, alphas, ps = [], [], [], []
        for h in range(hpg):
            sh = st[:, h * QB:(h + 1) * QB] + (slopes[h] * rel + base)
            mh = jnp.maximum(ms[h], jnp.max(sh, axis=0, keepdims=True))
            alpha = jnp.exp(ms[h] - mh)
            p = jnp.exp(sh - mh)
            new_m.append(mh)
            new_l.append(alpha * ls[h] + jnp.sum(p, axis=0, keepdims=True))
            alphas.append(alpha)
            ps.append(p.astype(BF16))
        pv = jnp.dot(vT_ref[kt], jnp.concatenate(ps, axis=1), preferred_element_type=F32)
        acc_ref[...] = acc_ref[...] * jnp.concatenate(alphas, axis=1) + pv
        return tuple(new_m), tuple(new_l)

    def slc_base(kt, diff):
        rows = [jnp.broadcast_to(sel_scr[pl.ds(kt * bps + b, 1), :], (SLC_BLOCK, QB)) for b in range(bps)]
        return jnp.where(diff >= 0, jnp.concatenate(rows, axis=0), ninf)

    def win_base(kt, diff):
        return jnp.where((diff >= 0) & (diff < WINDOW), 0.0, ninf)

    init = (tuple(jnp.full((1, QB), NEG_INF, F32) for _ in range(hpg)),
            tuple(jnp.zeros((1, QB), F32) for _ in range(hpg)))

    acc_slc[...] = jnp.zeros_like(acc_slc)
    _, l_slc = lax.fori_loop(0, i + 1, lambda kt, c: flash_tile(ks_ref, vsT_ref, acc_slc, kt, slc_base, c), init)
    acc_win[...] = jnp.zeros_like(acc_win)
    _, l_win = lax.fori_loop(jnp.maximum(i - WINDOW // KB, 0), i + 1,
                             lambda kt, c: flash_tile(kw_ref, vwT_ref, acc_win, kt, win_base, c), init)

    for h in range(hpg):
        cols = slice(h * QB, (h + 1) * QB)
        g_cmp = jax.nn.sigmoid(gl_ref[0, h:h + 1, :])
        g_slc = jax.nn.sigmoid(gl_ref[1, h:h + 1, :])
        g_win = jax.nn.sigmoid(gl_ref[2, h:h + 1, :])
        oT = (g_cmp * ocmp_scr[:, cols] + (g_slc / l_slc[h]) * acc_slc[:, cols]
              + (g_win / l_win[h]) * acc_win[:, cols])
        o_ref[:, h * HEAD_DIM:(h + 1) * HEAD_DIM] = oT.T.astype(o_ref.dtype)


def sparse_attention(slopes, q, k_cmp, v_cmpT, kv_rest, vsT, vwT, gate_logits_t, *, nc):
    t, hd = q.shape
    n_groups, ncp, _ = k_cmp.shape
    hpg = hd // HEAD_DIM // n_groups
    nb = t // SLC_BLOCK
    nt = t // KB
    assert t % QB == 0 and ncp == 4 * nb and nb % SUBLANES == 0
    assert (CMP_BLOCK, CMP_STRIDE, SLC_BLOCK) == (32, 16, 64)
    n_sel = min(N_SELECT, nb)
    w = hpg * QB
    kern = functools.partial(_attn_kernel, hpg=hpg, nc=nc, ncp=ncp, nb=nb, n_sel=n_sel)
    return pl.pallas_call(
        kern,
        grid=(n_groups, t // QB),
        in_specs=[
            pl.BlockSpec(memory_space=pltpu.SMEM),
            pl.BlockSpec((QB, hpg * HEAD_DIM), lambda g, i: (i, g)),
            pl.BlockSpec((None, ncp, HEAD_DIM), lambda g, i: (g, 0, 0)),
            pl.BlockSpec((None, HEAD_DIM, ncp), lambda g, i: (g, 0, 0)),
            pl.BlockSpec((t, HEAD_DIM), lambda g, i: (0, g)),
            pl.BlockSpec((None, nt, HEAD_DIM, KB), lambda g, i: (g, 0, 0, 0)),
            pl.BlockSpec((t, HEAD_DIM), lambda g, i: (0, 2 * n_groups + g)),
            pl.BlockSpec((None, nt, HEAD_DIM, KB), lambda g, i: (g, 0, 0, 0)),
            pl.BlockSpec((3, None, hpg, QB), lambda g, i: (0, g, 0, i)),
        ],
        out_specs=pl.BlockSpec((QB, hpg * HEAD_DIM), lambda g, i: (i, g)),
        out_shape=jax.ShapeDtypeStruct((t, hd), BF16),
        scratch_shapes=[
            pltpu.VMEM((SUBLANES + ncp, QB), F32),
            pltpu.VMEM((nb, QB), F32),
            pltpu.VMEM((nb, QB), F32),
            pltpu.VMEM((HEAD_DIM, w), F32),
            pltpu.VMEM((HEAD_DIM, w), F32),
            pltpu.VMEM((HEAD_DIM, w), F32),
        ],
        compiler_params=_params("parallel", "arbitrary"),
    )(slopes, q, k_cmp, v_cmpT, kv_rest, vsT, kv_rest, vwT, gate_logits_t)


def _conv_kernel(b_ref, c_ref, h_ref, cp_ref, hp_ref, w_ref, o_ref):
    i = pl.program_id(0)
    wk = w_ref[...]
    w0, w1, w2 = wk[0:1, :], wk[1:2, :], wk[2:3, :]
    u = c_ref[...] * h_ref[...]
    tt = u.shape[0]
    y = b_ref[...] * (w0 * pltpu.roll(u, 2, 0) + w1 * pltpu.roll(u, 1, 0) + w2 * u)
    o_ref[...] = y.astype(o_ref.dtype)
    u_prev = jnp.where(i > 0, cp_ref[...] * hp_ref[...], 0.0)
    ue = jnp.concatenate([u_prev, u[0:SUBLANES, :]], axis=0)
    n2 = 2 * SUBLANES
    u1 = pltpu.roll(ue, 1, 0)[SUBLANES:n2, :]
    u2 = pltpu.roll(ue, 2, 0)[SUBLANES:n2, :]
    y0 = b_ref[0:SUBLANES, :] * (w0 * u2 + w1 * u1 + w2 * u[0:SUBLANES, :])
    o_ref[0:SUBLANES, :] = y0.astype(o_ref.dtype)


def short_conv(conv_in, conv_w, tt=512, tc=512):
    t, cw3 = conv_in.shape
    cw = cw3 // 3
    tt, tc = _tile(t, tt), _tile(cw, tc)
    ncb = cw // tc
    rb = tt // SUBLANES
    prev = lambda i, j, off: (jnp.maximum(i * rb - 1, 0), j + off)
    return pl.pallas_call(
        _conv_kernel,
        grid=(t // tt, ncb),
        in_specs=[
            pl.BlockSpec((tt, tc), lambda i, j: (i, j)),
            pl.BlockSpec((tt, tc), lambda i, j: (i, j + ncb)),
            pl.BlockSpec((tt, tc), lambda i, j: (i, j + 2 * ncb)),
            pl.BlockSpec((SUBLANES, tc), functools.partial(prev, off=ncb)),
            pl.BlockSpec((SUBLANES, tc), functools.partial(prev, off=2 * ncb)),
            pl.BlockSpec((CONV_K, tc), lambda i, j: (0, j)),
        ],
        out_specs=pl.BlockSpec((tt, tc), lambda i, j: (i, j)),
        out_shape=jax.ShapeDtypeStruct((t, cw), BF16),
        compiler_params=_params("parallel", "parallel"),
    )(conv_in, conv_in, conv_in, conv_in, conv_in, conv_w)


def _mem_kernel(x_ref, gpre_ref, gpost_ref, wq_ref, km_ref, vm_ref, wo_ref, o_ref):
    x = x_ref[...]
    xn = _rms(x, gpre_ref[...]).astype(BF16)
    qm = jnp.dot(xn, wq_ref[...], preferred_element_type=F32) * (MEM_HEAD_DIM ** -0.5)
    outs = []
    for h in range(MEM_HEADS):
        cols = slice(h * MEM_HEAD_DIM, (h + 1) * MEM_HEAD_DIM)
        s = lax.dot_general(qm[:, cols].astype(BF16), km_ref[:, cols], _NT, preferred_element_type=F32)
        p = jnp.exp(s - jnp.max(s, axis=-1, keepdims=True))
        p = p / jnp.sum(p, axis=-1, keepdims=True)
        outs.append(jnp.dot(p.astype(BF16), vm_ref[:, cols], preferred_element_type=F32).astype(BF16))
    y = jnp.dot(jnp.concatenate(outs, axis=1), wo_ref[...], preferred_element_type=F32)
    o_ref[...] = x + _rms(y, gpost_ref[...])


def memory_sublayer(x, gpre, gpost, wq, km, vm, wo, tm=256):
    m, d = x.shape
    mw = wq.shape[1]
    nm = km.shape[0]
    tm = _tile(m, tm)
    full = lambda i: (0, 0)
    return pl.pallas_call(
        _mem_kernel,
        grid=(m // tm,),
        in_specs=[
            pl.BlockSpec((tm, d), lambda i: (i, 0)),
            pl.BlockSpec((1, d), full),
            pl.BlockSpec((1, d), full),
            pl.BlockSpec((d, mw), full),
            pl.BlockSpec((nm, mw), full),
            pl.BlockSpec((nm, mw), full),
            pl.BlockSpec((mw, d), full),
        ],
        out_specs=pl.BlockSpec((tm, d), lambda i: (i, 0)),
        out_shape=jax.ShapeDtypeStruct((m, d), F32),
        compiler_params=_params("parallel"),
    )(x, gpre.reshape(1, d), gpost.reshape(1, d), wq, km, vm, wo)


def _mlp_kernel(x_ref, gpre_ref, gpost_ref, wu_ref, wd_ref, o_ref, xn_scr, *, nf):
    f = pl.program_id(1)

    @pl.when(f == 0)
    def _():
        xn_scr[...] = _rms(x_ref[...], gpre_ref[...]).astype(BF16)
        o_ref[...] = jnp.zeros_like(o_ref)

    hid = jnp.dot(xn_scr[...], wu_ref[...], preferred_element_type=F32)
    hid = jnp.square(jnp.maximum(hid, 0.0))
    o_ref[...] += jnp.dot(hid.astype(BF16), wd_ref[...], preferred_element_type=F32)

    @pl.when(f == nf - 1)
    def _():
        o_ref[...] = x_ref[...] + _rms(o_ref[...], gpost_ref[...])


def mlp_sublayer(x, gpre, gpost, w_up, w_down, tm=512, tf=256):
    m, d = x.shape
    dff = w_up.shape[1]
    tm, tf = _tile(m, tm), _tile(dff, tf)
    nf = dff // tf
    return pl.pallas_call(
        functools.partial(_mlp_kernel, nf=nf),
        grid=(m // tm, nf),
        in_specs=[
            pl.BlockSpec((tm, d), lambda i, f: (i, 0)),
            pl.BlockSpec((1, d), lambda i, f: (0, 0)),
            pl.BlockSpec((1, d), lambda i, f: (0, 0)),
            pl.BlockSpec((d, tf), lambda i, f: (0, f)),
            pl.BlockSpec((tf, d), lambda i, f: (f, 0)),
        ],
        out_specs=pl.BlockSpec((tm, d), lambda i, f: (i, 0)),
        out_shape=jax.ShapeDtypeStruct((m, d), F32),
        scratch_shapes=[pltpu.VMEM((tm, d), BF16)],
        compiler_params=_params("parallel", "arbitrary"),
    )(x, gpre.reshape(1, d), gpost.reshape(1, d), w_up, w_down)


def _mixer(x, norm_pre, norm_post, w_in, cmp_k_pe, cmp_k_w1, cmp_k_w2, cmp_v_pe, cmp_v_w1, cmp_v_w2,
           conv_w, w_out):
    t, d = x.shape
    attn_w = d // 2
    conv_cw = d - attn_w
    n_heads = attn_w // HEAD_DIM
    n_groups = n_heads // 4
    kvw = n_groups * HEAD_DIM
    o_kv = attn_w
    o_gate = o_kv + 6 * kvw
    o_conv = o_gate + 3 * n_heads
    assert w_in.shape[1] == o_conv + 3 * conv_cw

    xn = rmsnorm_cast(x, norm_pre)
    q = matmul(xn, w_in[:, :o_kv].astype(BF16), BF16, scale=HEAD_DIM ** -0.5)
    kvc = matmul(xn, w_in[:, o_kv:o_kv + 2 * kvw].astype(BF16), F32)
    kv_rest = matmul(xn, w_in[:, o_kv + 2 * kvw:o_gate].astype(BF16), BF16)
    w_gate = jnp.pad(w_in[:, o_gate:o_conv], ((0, 0), (0, LANES - 3 * n_heads))).astype(BF16)
    gate_logits = matmul(xn, w_gate, F32)[:, :3 * n_heads]
    conv_in = matmul(xn, w_in[:, o_conv:].astype(BF16), F32)

    nc = (t - CMP_BLOCK) // CMP_STRIDE + 1
    ncp = t // CMP_STRIDE
    half = CMP_STRIDE * HEAD_DIM
    chunks = kvc.reshape(t, 2, n_groups, HEAD_DIM).transpose(1, 2, 0, 3).reshape(2, n_groups, ncp, half)
    pe2 = jnp.stack([cmp_k_pe, cmp_v_pe]).reshape(2, 2, half)
    w1 = jnp.stack([cmp_k_w1, cmp_v_w1]).astype(BF16)
    w2 = jnp.stack([cmp_k_w2, cmp_v_w2]).astype(BF16)
    cmp = compress_tokens(chunks, pe2, w1, w2)
    k_cmp = cmp[0]
    v_cmpT = cmp[1].transpose(0, 2, 1)

    nt = t // KB
    def keys_on_lanes(v):
        return v.reshape(nt, KB, n_groups, HEAD_DIM).transpose(2, 0, 3, 1)
    vsT = keys_on_lanes(kv_rest[:, kvw:2 * kvw])
    vwT = keys_on_lanes(kv_rest[:, 3 * kvw:4 * kvw])
    gl_t = gate_logits.T.reshape(3, n_groups, n_heads // n_groups, t)
    idx = jnp.arange(1, n_heads + 1, dtype=F32)
    slopes = jnp.exp2(-8.0 * idx / n_heads)
    o_attn = sparse_attention(slopes, q, k_cmp, v_cmpT, kv_rest, vsT, vwT, gl_t, nc=nc)

    o_conv_out = short_conv(conv_in, conv_w)
    a = jnp.concatenate([o_attn, o_conv_out], axis=1)
    return matmul_postnorm_residual(a, w_out.astype(BF16), x, norm_post)


def _memory(x, mem, norm_pre, norm_kv, norm_post, wq, wk, wv, wo):
    mn = rmsnorm_cast(mem, norm_kv)
    km = matmul(mn, wk.astype(BF16), BF16)
    vm = matmul(mn, wv.astype(BF16), BF16)
    return memory_sublayer(x, norm_pre, norm_post, wq.astype(BF16), km, vm, wo.astype(BF16))


def kernel(x, mem, mix_norm_pre, mix_norm_post, w_in, cmp_k_pe, cmp_k_w1, cmp_k_w2, cmp_v_pe, cmp_v_w1,
           cmp_v_w2, conv_w, w_out, mem_norm_pre, mem_norm_kv, mem_norm_post, w_mem_q, w_mem_k, w_mem_v,
           w_mem_o, mlp_norm_pre, mlp_norm_post, w_up, w_down):
    b, t, d = x.shape
    assert b == 1
    h = x[0]
    m = mem[0]
    for l in range(w_in.shape[0]):
        h = _mixer(h, mix_norm_pre[l], mix_norm_post[l], w_in[l], cmp_k_pe[l], cmp_k_w1[l], cmp_k_w2[l],
                   cmp_v_pe[l], cmp_v_w1[l], cmp_v_w2[l], conv_w[l], w_out[l])
        h = _memory(h, m, mem_norm_pre[l], mem_norm_kv[l], mem_norm_post[l], w_mem_q[l], w_mem_k[l],
                    w_mem_v[l], w_mem_o[l])
        h = mlp_sublayer(h, mlp_norm_pre[l], mlp_norm_post[l], w_up[l].astype(BF16), w_down[l].astype(BF16))
    return h[None]
```

```python
import functools

import jax
import jax.numpy as jnp
from jax import lax
from jax.experimental import pallas as pl
from jax.experimental.pallas import tpu as pltpu

HEAD_DIM = 128
CMP_BLOCK = 32
CMP_STRIDE = 16
SLC_BLOCK = 64
N_SELECT = 16
N_LOCAL_FORCED = 2
WINDOW = 512
CONV_K = 3
MEM_HEADS = 4
MEM_HEAD_DIM = 128
RMS_EPS = 1e-6
NEG_INF = -1e30
FORCED_SCORE = 1e9

V7X_VMEM_BYTES = 64 * 1024 * 1024
VMEM_LIMIT = V7X_VMEM_BYTES - 8 * 1024 * 1024
LANES = 128
SUBLANES = 8

QB = 128
KB = 128
KT = 512
WIN_KEYS = WINDOW + QB

F32 = jnp.float32
BF16 = jnp.bfloat16

_NT = (((1,), (1,)), ((), ()))


def _tile(n, pref):
    t = min(n, pref)
    while n % t:
        t -= 1
    return t


def _params(*sem):
    return pltpu.CompilerParams(dimension_semantics=sem, vmem_limit_bytes=VMEM_LIMIT)


def _rms(x, gain):
    return x * lax.rsqrt(jnp.mean(x * x, axis=-1, keepdims=True) + RMS_EPS) * gain


def _rmsnorm_kernel(x_ref, g_ref, o_ref):
    o_ref[...] = _rms(x_ref[...].astype(F32), g_ref[...]).astype(o_ref.dtype)


def rmsnorm_cast(x, gain, out_dtype=BF16):
    m, d = x.shape
    tm = _tile(m, 256)
    return pl.pallas_call(
        _rmsnorm_kernel,
        grid=(m // tm,),
        in_specs=[pl.BlockSpec((tm, d), lambda i: (i, 0)), pl.BlockSpec((1, d), lambda i: (0, 0))],
        out_specs=pl.BlockSpec((tm, d), lambda i: (i, 0)),
        out_shape=jax.ShapeDtypeStruct((m, d), out_dtype),
        name="rmsnorm_cast",
        compiler_params=_params("parallel"),
    )(x, gain.reshape(1, d))


def _mm_kernel(a_ref, w_ref, o_ref, *, scale):
    acc = jnp.dot(a_ref[...], w_ref[...], preferred_element_type=F32)
    if scale is not None:
        acc = acc * scale
    o_ref[...] = acc.astype(o_ref.dtype)


def matmul(a, w, out_dtype, scale=None, tm=1024, tn=1024):
    m, k = a.shape
    _, n = w.shape
    tm, tn = _tile(m, tm), _tile(n, tn)
    return pl.pallas_call(
        functools.partial(_mm_kernel, scale=scale),
        grid=(m // tm, n // tn),
        in_specs=[pl.BlockSpec((tm, k), lambda i, j: (i, 0)), pl.BlockSpec((k, tn), lambda i, j: (0, j))],
        out_specs=pl.BlockSpec((tm, tn), lambda i, j: (i, j)),
        out_shape=jax.ShapeDtypeStruct((m, n), out_dtype),
        name="matmul",
        compiler_params=_params("parallel", "arbitrary"),
    )(a, w)


def _mm_postnorm_kernel(a_ref, w_ref, x_ref, g_ref, o_ref, *, nk):
    k = pl.program_id(1)

    @pl.when(k == 0)
    def _():
        o_ref[...] = jnp.zeros_like(o_ref)

    o_ref[...] += jnp.dot(a_ref[...], w_ref[...], preferred_element_type=F32)

    @pl.when(k == nk - 1)
    def _():
        o_ref[...] = x_ref[...] + _rms(o_ref[...], g_ref[...])


def matmul_postnorm_residual(a, w, x, gain, tm=512, tk=512):
    m, kd = a.shape
    _, n = w.shape
    tm, tk = _tile(m, tm), _tile(kd, tk)
    nk = kd // tk
    return pl.pallas_call(
        functools.partial(_mm_postnorm_kernel, nk=nk),
        grid=(m // tm, nk),
        in_specs=[
            pl.BlockSpec((tm, tk), lambda i, k: (i, k)),
            pl.BlockSpec((tk, n), lambda i, k: (k, 0)),
            pl.BlockSpec((tm, n), lambda i, k: (i, 0)),
            pl.BlockSpec((1, n), lambda i, k: (0, 0)),
        ],
        out_specs=pl.BlockSpec((tm, n), lambda i, k: (i, 0)),
        out_shape=jax.ShapeDtypeStruct((m, n), F32),
        name="outproj_postnorm",
        compiler_params=_params("parallel", "arbitrary"),
    )(a, w, x, gain.reshape(1, n))


def _gelu_tanh(x):
    c = 0.7978845608028654
    return x * (0.5 * (1.0 + jnp.tanh(c * (x + 0.044715 * (x * x * x)))))


def _compress_kernel(c_ref, pe_ref, w1_ref, w2_ref, o_ref, *, half):
    c = c_ref[...]
    pe = pe_ref[...]
    top = (c + pe[0:1, :]).astype(BF16)
    bot = (c + pe[1:2, :]).astype(BF16)
    a = jnp.dot(top, w1_ref[0:half, :], preferred_element_type=F32)
    b = jnp.dot(bot, w1_ref[half:2 * half, :], preferred_element_type=F32)
    ncp = c.shape[0]
    hid = _gelu_tanh(a + pltpu.roll(b, ncp - 1, 0))
    o_ref[...] = jnp.dot(hid.astype(BF16), w2_ref[...], preferred_element_type=F32).astype(o_ref.dtype)


def compress_tokens(chunks, pe2, w1, w2):
    two, g, ncp, half = chunks.shape
    return pl.pallas_call(
        functools.partial(_compress_kernel, half=half),
        grid=(two, g),
        in_specs=[
            pl.BlockSpec((None, None, ncp, half), lambda s, j: (s, j, 0, 0)),
            pl.BlockSpec((None, 2, half), lambda s, j: (s, 0, 0)),
            pl.BlockSpec((None, 2 * half, HEAD_DIM), lambda s, j: (s, 0, 0)),
            pl.BlockSpec((None, HEAD_DIM, HEAD_DIM), lambda s, j: (s, 0, 0)),
        ],
        out_specs=pl.BlockSpec((None, None, ncp, HEAD_DIM), lambda s, j: (s, j, 0, 0)),
        out_shape=jax.ShapeDtypeStruct((two, g, ncp, HEAD_DIM), BF16),
        name="compress_tokens",
        compiler_params=_params("parallel", "parallel"),
    )(chunks, pe2, w1, w2)


def _attn_kernel(slopes_ref, q_ref, kcmp_ref, vcmpT_ref, ks_ref, vsT_ref, kw_ref, vwT_ref, gl_ref,
                 o_ref, p_scr, imp_scr, sel_scr, ocmp_scr, acc_slc, acc_win,
                 *, hpg, nc, ncp, nb, n_sel):
    g = pl.program_id(0)
    i = pl.program_id(1)
    t0 = i * QB
    w = hpg * QB
    ninf = -jnp.inf

    q = q_ref[...]
    qs = jnp.concatenate([q[:, h * HEAD_DIM:(h + 1) * HEAD_DIM] for h in range(hpg)], axis=0)
    slopes = [slopes_ref[g * hpg + h] for h in range(hpg)]

    s = lax.dot_general(kcmp_ref[...], qs, _NT, preferred_element_type=F32)
    c_iota = lax.broadcasted_iota(jnp.int32, (ncp, QB), 0)
    q_iota = lax.broadcasted_iota(jnp.int32, (ncp, QB), 1)
    cmp_end = c_iota * CMP_STRIDE + (CMP_BLOCK - 1)
    mask_c = (cmp_end <= t0 + q_iota) & (c_iota < nc)
    rel_c = (cmp_end - t0).astype(F32)
    psum = jnp.zeros((ncp, QB), F32)
    for h in range(hpg):
        sh = s[:, h * QB:(h + 1) * QB] + slopes[h] * rel_c
        sh = jnp.where(mask_c, sh, ninf)
        m = jnp.maximum(jnp.max(sh, axis=0, keepdims=True), NEG_INF)
        p = jnp.exp(sh - m)
        l = jnp.sum(p, axis=0, keepdims=True)
        pn = p * jnp.where(l > 0.0, 1.0 / l, 0.0)
        psum = psum + pn
        ocmp_scr[:, h * QB:(h + 1) * QB] = jnp.dot(vcmpT_ref[...], pn.astype(BF16),
                                                    preferred_element_type=F32)

    p_scr[0:SUBLANES, :] = jnp.zeros((SUBLANES, QB), F32)
    p_scr[SUBLANES:SUBLANES + ncp, :] = psum
    imp = p_scr[pl.ds(SUBLANES - 1, nb, stride=4), :]
    for j in range(4):
        imp = imp + p_scr[pl.ds(SUBLANES + j, nb, stride=4), :]
    n_iota = lax.broadcasted_iota(jnp.int32, (nb, QB), 0)
    tq = t0 + lax.broadcasted_iota(jnp.int32, (nb, QB), 1)
    cur = lax.shift_right_logical(tq, 6)
    valid = n_iota <= cur
    back = cur - n_iota
    forced = (n_iota == 0) | ((back >= 0) & (back < N_LOCAL_FORCED))
    imp = jnp.where(forced & valid, FORCED_SCORE, imp)
    imp = jnp.where(valid, imp, NEG_INF)
    imp_scr[...] = imp

    n_grp = nb // SUBLANES
    vals = [imp[j * SUBLANES:(j + 1) * SUBLANES, :] for j in range(n_grp)]
    cnts = [jnp.zeros((SUBLANES, QB), F32) for _ in range(n_grp)]
    row8 = lax.broadcasted_iota(jnp.int32, (SUBLANES, QB), 0)
    for mi in range(nb):
        vm = jnp.broadcast_to(imp_scr[pl.ds(mi, 1), :], (SUBLANES, QB))
        for j in range(n_grp):
            lo = j * SUBLANES
            if lo + SUBLANES - 1 < mi:
                beats = vm > vals[j]
            elif lo > mi:
                beats = vm >= vals[j]
            else:
                beats = (vm > vals[j]) | ((vm == vals[j]) & (row8 > mi - lo))
            cnts[j] = cnts[j] + jnp.where(beats, 1.0, 0.0)
    cnt = jnp.concatenate(cnts, axis=0)
    sel = (cnt < float(n_sel)) & valid
    sel_scr[...] = jnp.where(sel, 0.0, ninf)

    def flash_tile(k_rows, vT, acc_ref, k0, base_of, carry):
        ms, ls = carry
        rows = k_rows.shape[0]
        st = lax.dot_general(k_rows, qs, _NT, preferred_element_type=F32)
        r_iota = lax.broadcasted_iota(jnp.int32, (rows, QB), 0)
        k_iota = lax.broadcasted_iota(jnp.int32, (rows, QB), 1)
        diff = (t0 - k0) + (k_iota - r_iota)
        base = base_of(diff)
        rel = (-diff).astype(F32)
        new_m, new_l, alphas, ps = [], [], [], []
        for h in range(hpg):
            sh = st[:, h * QB:(h + 1) * QB] + (slopes[h] * rel + base)
            mh = jnp.maximum(ms[h], jnp.max(sh, axis=0, keepdims=True))
            alpha = jnp.exp(ms[h] - mh)
            p = jnp.exp(sh - mh)
            new_m.append(mh)
            new_l.append(alpha * ls[h] + jnp.sum(p, axis=0, keepdims=True))
            alphas.append(alpha)
            ps.append(p.astype(BF16))
        pv = jnp.dot(vT, jnp.concatenate(ps, axis=1), preferred_element_type=F32)
        acc_ref[...] = acc_ref[...] * jnp.concatenate(alphas, axis=1) + pv
        return tuple(new_m), tuple(new_l)

    init = (tuple(jnp.full((1, QB), NEG_INF, F32) for _ in range(hpg)),
            tuple(jnp.zeros((1, QB), F32) for _ in range(hpg)))

    tiles_per_kt = KT // KB
    blocks_per_kt = KT // SLC_BLOCK

    def slc_step(kt, carry):
        k0 = pl.multiple_of(kt * KT, KT)
        vT = jnp.concatenate([vsT_ref[kt * tiles_per_kt + j] for j in range(tiles_per_kt)], axis=1)

        def base_of(diff):
            rows = [jnp.broadcast_to(sel_scr[pl.ds(kt * blocks_per_kt + b, 1), :], (SLC_BLOCK, QB))
                    for b in range(blocks_per_kt)]
            return jnp.where(diff >= 0, jnp.concatenate(rows, axis=0), ninf)

        return flash_tile(ks_ref[pl.ds(k0, KT), :], vT, acc_slc, k0, base_of, carry)

    acc_slc[...] = jnp.zeros_like(acc_slc)
    _, l_slc = lax.fori_loop(0, t0 // KT + 1, slc_step, init)

    w0 = pl.multiple_of(jnp.maximum(t0 - WINDOW, 0), KB)
    wt = jnp.maximum(i - WINDOW // KB, 0)
    vT_w = jnp.concatenate([vwT_ref[wt + j] for j in range(WIN_KEYS // KB)], axis=1)
    acc_win[...] = jnp.zeros_like(acc_win)
    _, l_win = flash_tile(kw_ref[pl.ds(w0, WIN_KEYS), :], vT_w, acc_win, w0,
                          lambda diff: jnp.where((diff >= 0) & (diff < WINDOW), 0.0, ninf), init)

    for h in range(hpg):
        cols = slice(h * QB, (h + 1) * QB)
        g_cmp = jax.nn.sigmoid(gl_ref[0, h:h + 1, :])
        g_slc = jax.nn.sigmoid(gl_ref[1, h:h + 1, :])
        g_win = jax.nn.sigmoid(gl_ref[2, h:h + 1, :])
        oT = (g_cmp * ocmp_scr[:, cols] + (g_slc / l_slc[h]) * acc_slc[:, cols]
              + (g_win / l_win[h]) * acc_win[:, cols])
        o_ref[:, h * HEAD_DIM:(h + 1) * HEAD_DIM] = oT.T.astype(o_ref.dtype)


def sparse_attention(slopes, q, k_cmp, v_cmpT, kv_rest, vsT, vwT, gate_logits_t, *, nc):
    t, hd = q.shape
    n_groups, ncp, _ = k_cmp.shape
    hpg = hd // HEAD_DIM // n_groups
    nb = t // SLC_BLOCK
    nt = t // KB
    assert t % KT == 0 and t >= WIN_KEYS and ncp == 4 * nb and nb % SUBLANES == 0
    assert (CMP_BLOCK, CMP_STRIDE, SLC_BLOCK) == (32, 16, 64)
    n_sel = min(N_SELECT, nb)
    w = hpg * QB
    kern = functools.partial(_attn_kernel, hpg=hpg, nc=nc, ncp=ncp, nb=nb, n_sel=n_sel)
    return pl.pallas_call(
        kern,
        grid=(n_groups, t // QB),
        in_specs=[
            pl.BlockSpec(memory_space=pltpu.SMEM),
            pl.BlockSpec((QB, hpg * HEAD_DIM), lambda g, i: (i, g)),
            pl.BlockSpec((None, ncp, HEAD_DIM), lambda g, i: (g, 0, 0)),
            pl.BlockSpec((None, HEAD_DIM, ncp), lambda g, i: (g, 0, 0)),
            pl.BlockSpec((t, HEAD_DIM), lambda g, i: (0, g)),
            pl.BlockSpec((None, nt, HEAD_DIM, KB), lambda g, i: (g, 0, 0, 0)),
            pl.BlockSpec((t, HEAD_DIM), lambda g, i: (0, 2 * n_groups + g)),
            pl.BlockSpec((None, nt, HEAD_DIM, KB), lambda g, i: (g, 0, 0, 0)),
            pl.BlockSpec((3, None, hpg, QB), lambda g, i: (0, g, 0, i)),
        ],
        out_specs=pl.BlockSpec((QB, hpg * HEAD_DIM), lambda g, i: (i, g)),
        out_shape=jax.ShapeDtypeStruct((t, hd), BF16),
        scratch_shapes=[
            pltpu.VMEM((SUBLANES + ncp, QB), F32),
            pltpu.VMEM((nb, QB), F32),
            pltpu.VMEM((nb, QB), F32),
            pltpu.VMEM((HEAD_DIM, w), F32),
            pltpu.VMEM((HEAD_DIM, w), F32),
            pltpu.VMEM((HEAD_DIM, w), F32),
        ],
        name="sparse_attention",
        compiler_params=_params("parallel", "arbitrary"),
    )(slopes, q, k_cmp, v_cmpT, kv_rest, vsT, kv_rest, vwT, gate_logits_t)


def _conv_kernel(b_ref, c_ref, h_ref, cp_ref, hp_ref, w_ref, o_ref):
    i = pl.program_id(0)
    wk = w_ref[...]
    w0, w1, w2 = wk[0:1, :], wk[1:2, :], wk[2:3, :]
    u = c_ref[...] * h_ref[...]
    tt = u.shape[0]
    y = b_ref[...] * (w0 * pltpu.roll(u, 2, 0) + w1 * pltpu.roll(u, 1, 0) + w2 * u)
    o_ref[...] = y.astype(o_ref.dtype)
    u_prev = jnp.where(i > 0, cp_ref[...] * hp_ref[...], 0.0)
    ue = jnp.concatenate([u_prev, u[0:SUBLANES, :]], axis=0)
    n2 = 2 * SUBLANES
    u1 = pltpu.roll(ue, 1, 0)[SUBLANES:n2, :]
    u2 = pltpu.roll(ue, 2, 0)[SUBLANES:n2, :]
    y0 = b_ref[0:SUBLANES, :] * (w0 * u2 + w1 * u1 + w2 * u[0:SUBLANES, :])
    o_ref[0:SUBLANES, :] = y0.astype(o_ref.dtype)


def short_conv(conv_in, conv_w, tt=512, tc=512):
    t, cw3 = conv_in.shape
    cw = cw3 // 3
    tt, tc = _tile(t, tt), _tile(cw, tc)
    ncb = cw // tc
    rb = tt // SUBLANES
    prev = lambda i, j, off: (jnp.maximum(i * rb - 1, 0), j + off)
    return pl.pallas_call(
        _conv_kernel,
        grid=(t // tt, ncb),
        in_specs=[
            pl.BlockSpec((tt, tc), lambda i, j: (i, j)),
            pl.BlockSpec((tt, tc), lambda i, j: (i, j + ncb)),
            pl.BlockSpec((tt, tc), lambda i, j: (i, j + 2 * ncb)),
            pl.BlockSpec((SUBLANES, tc), functools.partial(prev, off=ncb)),
            pl.BlockSpec((SUBLANES, tc), functools.partial(prev, off=2 * ncb)),
            pl.BlockSpec((CONV_K, tc), lambda i, j: (0, j)),
        ],
        out_specs=pl.BlockSpec((tt, tc), lambda i, j: (i, j)),
        out_shape=jax.ShapeDtypeStruct((t, cw), BF16),
        name="short_conv",
        compiler_params=_params("parallel", "parallel"),
    )(conv_in, conv_in, conv_in, conv_in, conv_in, conv_w)


def _mem_kernel(x_ref, gpre_ref, gpost_ref, wq_ref, km_ref, vm_ref, wo_ref, o_ref):
    x = x_ref[...]
    xn = _rms(x, gpre_ref[...]).astype(BF16)
    qm = jnp.dot(xn, wq_ref[...], preferred_element_type=F32) * (MEM_HEAD_DIM ** -0.5)
    outs = []
    for h in range(MEM_HEADS):
        cols = slice(h * MEM_HEAD_DIM, (h + 1) * MEM_HEAD_DIM)
        s = lax.dot_general(qm[:, cols].astype(BF16), km_ref[:, cols], _NT, preferred_element_type=F32)
        p = jnp.exp(s - jnp.max(s, axis=-1, keepdims=True))
        p = p / jnp.sum(p, axis=-1, keepdims=True)
        outs.append(jnp.dot(p.astype(BF16), vm_ref[:, cols], preferred_element_type=F32).astype(BF16))
    y = jnp.dot(jnp.concatenate(outs, axis=1), wo_ref[...], preferred_element_type=F32)
    o_ref[...] = x + _rms(y, gpost_ref[...])


def memory_sublayer(x, gpre, gpost, wq, km, vm, wo, tm=256):
    m, d = x.shape
    mw = wq.shape[1]
    nm = km.shape[0]
    tm = _tile(m, tm)
    full = lambda i: (0, 0)
    return pl.pallas_call(
        _mem_kernel,
        grid=(m // tm,),
        in_specs=[
            pl.BlockSpec((tm, d), lambda i: (i, 0)),
            pl.BlockSpec((1, d), full),
            pl.BlockSpec((1, d), full),
            pl.BlockSpec((d, mw), full),
            pl.BlockSpec((nm, mw), full),
            pl.BlockSpec((nm, mw), full),
            pl.BlockSpec((mw, d), full),
        ],
        out_specs=pl.BlockSpec((tm, d), lambda i: (i, 0)),
        out_shape=jax.ShapeDtypeStruct((m, d), F32),
        name="memory_sublayer",
        compiler_params=_params("parallel"),
    )(x, gpre.reshape(1, d), gpost.reshape(1, d), wq, km, vm, wo)


def _mlp_kernel(x_ref, gpre_ref, gpost_ref, wu_ref, wd_ref, o_ref, xn_scr, *, nf):
    f = pl.program_id(1)

    @pl.when(f == 0)
    def _():
        xn_scr[...] = _rms(x_ref[...], gpre_ref[...]).astype(BF16)
        o_ref[...] = jnp.zeros_like(o_ref)

    hid = jnp.dot(xn_scr[...], wu_ref[...], preferred_element_type=F32)
    hid = jnp.square(jnp.maximum(hid, 0.0))
    o_ref[...] += jnp.dot(hid.astype(BF16), wd_ref[...], preferred_element_type=F32)

    @pl.when(f == nf - 1)
    def _():
        o_ref[...] = x_ref[...] + _rms(o_ref[...], gpost_ref[...])


def mlp_sublayer(x, gpre, gpost, w_up, w_down, tm=512, tf=512):
    m, d = x.shape
    dff = w_up.shape[1]
    tm, tf = _tile(m, tm), _tile(dff, tf)
    nf = dff // tf
    return pl.pallas_call(
        functools.partial(_mlp_kernel, nf=nf),
        grid=(m // tm, nf),
        in_specs=[
            pl.BlockSpec((tm, d), lambda i, f: (i, 0), pipeline_mode=pl.Buffered(1)),
            pl.BlockSpec((1, d), lambda i, f: (0, 0)),
            pl.BlockSpec((1, d), lambda i, f: (0, 0)),
            pl.BlockSpec((d, tf), lambda i, f: (0, f)),
            pl.BlockSpec((tf, d), lambda i, f: (f, 0)),
        ],
        out_specs=pl.BlockSpec((tm, d), lambda i, f: (i, 0)),
        out_shape=jax.ShapeDtypeStruct((m, d), F32),
        scratch_shapes=[pltpu.VMEM((tm, d), BF16)],
        name="mlp_sublayer",
        compiler_params=_params("parallel", "arbitrary"),
    )(x, gpre.reshape(1, d), gpost.reshape(1, d), w_up, w_down)


def _mixer(x, norm_pre, norm_post, w_in, cmp_k_pe, cmp_k_w1, cmp_k_w2, cmp_v_pe, cmp_v_w1, cmp_v_w2,
           conv_w, w_out):
    t, d = x.shape
    attn_w = d // 2
    conv_cw = d - attn_w
    n_heads = attn_w // HEAD_DIM
    n_groups = n_heads // 4
    kvw = n_groups * HEAD_DIM
    o_kv = attn_w
    o_gate = o_kv + 6 * kvw
    o_conv = o_gate + 3 * n_heads
    assert w_in.shape[1] == o_conv + 3 * conv_cw

    xn = rmsnorm_cast(x, norm_pre)
    q = matmul(xn, w_in[:, :o_kv].astype(BF16), BF16, scale=HEAD_DIM ** -0.5)
    kvc = matmul(xn, w_in[:, o_kv:o_kv + 2 * kvw].astype(BF16), F32)
    kv_rest = matmul(xn, w_in[:, o_kv + 2 * kvw:o_gate].astype(BF16), BF16)
    w_gate = jnp.pad(w_in[:, o_gate:o_conv], ((0, 0), (0, LANES - 3 * n_heads))).astype(BF16)
    gate_logits = matmul(xn, w_gate, F32)[:, :3 * n_heads]
    conv_in = matmul(xn, w_in[:, o_conv:].astype(BF16), F32)

    nc = (t - CMP_BLOCK) // CMP_STRIDE + 1
    ncp = t // CMP_STRIDE
    half = CMP_STRIDE * HEAD_DIM
    chunks = kvc.reshape(t, 2, n_groups, HEAD_DIM).transpose(1, 2, 0, 3).reshape(2, n_groups, ncp, half)
    pe2 = jnp.stack([cmp_k_pe, cmp_v_pe]).reshape(2, 2, half)
    w1 = jnp.stack([cmp_k_w1, cmp_v_w1]).astype(BF16)
    w2 = jnp.stack([cmp_k_w2, cmp_v_w2]).astype(BF16)
    cmp = compress_tokens(chunks, pe2, w1, w2)
    k_cmp = cmp[0]
    v_cmpT = cmp[1].transpose(0, 2, 1)

    nt = t // KB
    def keys_on_lanes(v):
        return v.reshape(nt, KB, n_groups, HEAD_DIM).transpose(2, 0, 3, 1)
    vsT = keys_on_lanes(kv_rest[:, kvw:2 * kvw])
    vwT = keys_on_lanes(kv_rest[:, 3 * kvw:4 * kvw])
    gl_t = gate_logits.T.reshape(3, n_groups, n_heads // n_groups, t)
    idx = jnp.arange(1, n_heads + 1, dtype=F32)
    slopes = jnp.exp2(-8.0 * idx / n_heads)
    o_attn = sparse_attention(slopes, q, k_cmp, v_cmpT, kv_rest, vsT, vwT, gl_t, nc=nc)

    o_conv_out = short_conv(conv_in, conv_w)
    a = jnp.concatenate([o_attn, o_conv_out], axis=1)
    return matmul_postnorm_residual(a, w_out.astype(BF16), x, norm_post)


def _memory(x, mem, norm_pre, norm_kv, norm_post, wq, wk, wv, wo):
    mn = rmsnorm_cast(mem, norm_kv)
    km = matmul(mn, wk.astype(BF16), BF16)
    vm = matmul(mn, wv.astype(BF16), BF16)
    return memory_sublayer(x, norm_pre, norm_post, wq.astype(BF16), km, vm, wo.astype(BF16))


def kernel(x, mem, mix_norm_pre, mix_norm_post, w_in, cmp_k_pe, cmp_k_w1, cmp_k_w2, cmp_v_pe, cmp_v_w1,
           cmp_v_w2, conv_w, w_out, mem_norm_pre, mem_norm_kv, mem_norm_post, w_mem_q, w_mem_k, w_mem_v,
           w_mem_o, mlp_norm_pre, mlp_norm_post, w_up, w_down):
    b, t, d = x.shape
    assert b == 1
    h = x[0]
    m = mem[0]
    for l in range(w_in.shape[0]):
        h = _mixer(h, mix_norm_pre[l], mix_norm_post[l], w_in[l], cmp_k_pe[l], cmp_k_w1[l], cmp_k_w2[l],
                   cmp_v_pe[l], cmp_v_w1[l], cmp_v_w2[l], conv_w[l], w_out[l])
        h = _memory(h, m, mem_norm_pre[l], mem_norm_kv[l], mem_norm_post[l], w_mem_q[l], w_mem_k[l],
                    w_mem_v[l], w_mem_o[l])
        h = mlp_sublayer(h, mlp_norm_pre[l], mlp_norm_post[l], w_up[l].astype(BF16), w_down[l].astype(BF16))
    return h[None]
```

```python
import functools

import jax
import jax.numpy as jnp
from jax import lax
from jax.experimental import pallas as pl
from jax.experimental.pallas import tpu as pltpu

HEAD_DIM = 128
CMP_BLOCK = 32
CMP_STRIDE = 16
SLC_BLOCK = 64
N_SELECT = 16
N_LOCAL_FORCED = 2
WINDOW = 512
CONV_K = 3
MEM_HEADS = 4
MEM_HEAD_DIM = 128
RMS_EPS = 1e-6
NEG_INF = -1e30
FORCED_SCORE = 1e9

V7X_VMEM_BYTES = 64 * 1024 * 1024
VMEM_LIMIT = V7X_VMEM_BYTES - 8 * 1024 * 1024
LANES = 128
SUBLANES = 8

QB = 128
KB = 128
KT = 512
WIN_KEYS = WINDOW + QB

F32 = jnp.float32
BF16 = jnp.bfloat16

_NT = (((1,), (1,)), ((), ()))
_TN = (((0,), (0,)), ((), ()))


def _tile(n, pref):
    t = min(n, pref)
    while n % t:
        t -= 1
    return t


def _params(*sem):
    return pltpu.CompilerParams(dimension_semantics=sem, vmem_limit_bytes=VMEM_LIMIT)


def _rms(x, gain):
    return x * lax.rsqrt(jnp.mean(x * x, axis=-1, keepdims=True) + RMS_EPS) * gain


def _rmsnorm_kernel(x_ref, g_ref, o_ref):
    o_ref[...] = _rms(x_ref[...].astype(F32), g_ref[...]).astype(o_ref.dtype)


def rmsnorm_cast(x, gain, out_dtype=BF16):
    m, d = x.shape
    tm = _tile(m, 256)
    return pl.pallas_call(
        _rmsnorm_kernel,
        grid=(m // tm,),
        in_specs=[pl.BlockSpec((tm, d), lambda i: (i, 0)), pl.BlockSpec((1, d), lambda i: (0, 0))],
        out_specs=pl.BlockSpec((tm, d), lambda i: (i, 0)),
        out_shape=jax.ShapeDtypeStruct((m, d), out_dtype),
        name="rmsnorm_cast",
        compiler_params=_params("parallel"),
    )(x, gain.reshape(1, d))


def _mm_kernel(a_ref, w_ref, o_ref, *, scale):
    acc = jnp.dot(a_ref[...], w_ref[...], preferred_element_type=F32)
    if scale is not None:
        acc = acc * scale
    o_ref[...] = acc.astype(o_ref.dtype)


def matmul(a, w, out_dtype, scale=None, col0=0, n=None, tm=1024, tn=1024):
    m, k = a.shape
    n = w.shape[1] - col0 if n is None else n
    tm, tn = _tile(m, tm), _tile(n, tn)
    while col0 % tn:
        tn = _tile(n, tn - 1)
    jb = col0 // tn
    return pl.pallas_call(
        functools.partial(_mm_kernel, scale=scale),
        grid=(m // tm, n // tn),
        in_specs=[pl.BlockSpec((tm, k), lambda i, j: (i, 0)), pl.BlockSpec((k, tn), lambda i, j: (0, j + jb))],
        out_specs=pl.BlockSpec((tm, tn), lambda i, j: (i, j)),
        out_shape=jax.ShapeDtypeStruct((m, n), out_dtype),
        name="matmul",
        compiler_params=_params("parallel", "arbitrary"),
    )(a, w)


def _mm_postnorm_kernel(a1_ref, a2_ref, w_ref, x_ref, g_ref, o_ref, *, nk1, nk):
    k = pl.program_id(1)

    @pl.when(k == 0)
    def _():
        o_ref[...] = jnp.zeros_like(o_ref)

    @pl.when(k < nk1)
    def _():
        o_ref[...] += jnp.dot(a1_ref[...], w_ref[...], preferred_element_type=F32)

    @pl.when(k >= nk1)
    def _():
        o_ref[...] += jnp.dot(a2_ref[...], w_ref[...], preferred_element_type=F32)

    @pl.when(k == nk - 1)
    def _():
        o_ref[...] = x_ref[...] + _rms(o_ref[...], g_ref[...])


def matmul_postnorm_residual(a1, a2, w, x, gain, tm=512, tk=512):
    m, k1 = a1.shape
    k2 = a2.shape[1]
    _, n = w.shape
    tm = _tile(m, tm)
    tk = _tile(k1, tk)
    while k2 % tk:
        tk = _tile(k1, tk - 1)
    nk1, nk = k1 // tk, (k1 + k2) // tk
    return pl.pallas_call(
        functools.partial(_mm_postnorm_kernel, nk1=nk1, nk=nk),
        grid=(m // tm, nk),
        in_specs=[
            pl.BlockSpec((tm, tk), lambda i, k: (i, jnp.minimum(k, nk1 - 1))),
            pl.BlockSpec((tm, tk), lambda i, k: (i, jnp.maximum(k - nk1, 0))),
            pl.BlockSpec((tk, n), lambda i, k: (k, 0)),
            pl.BlockSpec((tm, n), lambda i, k: (i, 0)),
            pl.BlockSpec((1, n), lambda i, k: (0, 0)),
        ],
        out_specs=pl.BlockSpec((tm, n), lambda i, k: (i, 0)),
        out_shape=jax.ShapeDtypeStruct((m, n), F32),
        name="outproj_postnorm",
        compiler_params=_params("parallel", "arbitrary"),
    )(a1, a2, w, x, gain.reshape(1, n))


def _gelu_tanh(x):
    c = 0.7978845608028654
    return x * (0.5 * (1.0 + jnp.tanh(c * (x + 0.044715 * (x * x * x)))))


def _compress_kernel(x_ref, pe_ref, w1_ref, w2_ref, o_ref, *, ncp):
    a = jnp.zeros((ncp, HEAD_DIM), F32)
    b = jnp.zeros((ncp, HEAD_DIM), F32)
    for l in range(CMP_STRIDE):
        xl = x_ref[pl.ds(l, ncp, stride=CMP_STRIDE), :]
        lo, hi = l, CMP_STRIDE + l
        a = a + jnp.dot((xl + pe_ref[lo:lo + 1, :]).astype(BF16), w1_ref[lo * HEAD_DIM:(lo + 1) * HEAD_DIM, :],
                        preferred_element_type=F32)
        b = b + jnp.dot((xl + pe_ref[hi:hi + 1, :]).astype(BF16), w1_ref[hi * HEAD_DIM:(hi + 1) * HEAD_DIM, :],
                        preferred_element_type=F32)
    hid = _gelu_tanh(a + pltpu.roll(b, ncp - 1, 0))
    o_ref[...] = jnp.dot(hid.astype(BF16), w2_ref[...], preferred_element_type=F32).astype(o_ref.dtype)


def compress_tokens(kvc, pe, w1, w2, n_groups):
    t = kvc.shape[0]
    ncp = t // CMP_STRIDE
    assert CMP_BLOCK == 2 * CMP_STRIDE
    return pl.pallas_call(
        functools.partial(_compress_kernel, ncp=ncp),
        grid=(2, n_groups),
        in_specs=[
            pl.BlockSpec((t, HEAD_DIM), lambda s, j: (0, s * n_groups + j)),
            pl.BlockSpec((None, CMP_BLOCK, HEAD_DIM), lambda s, j: (s, 0, 0)),
            pl.BlockSpec((None, CMP_BLOCK * HEAD_DIM, HEAD_DIM), lambda s, j: (s, 0, 0)),
            pl.BlockSpec((None, HEAD_DIM, HEAD_DIM), lambda s, j: (s, 0, 0)),
        ],
        out_specs=pl.BlockSpec((None, None, ncp, HEAD_DIM), lambda s, j: (s, j, 0, 0)),
        out_shape=jax.ShapeDtypeStruct((2, n_groups, ncp, HEAD_DIM), BF16),
        name="compress_tokens",
        compiler_params=_params("parallel", "parallel"),
    )(kvc, pe, w1, w2)


def _attn_kernel(slopes_ref, q_ref, kcmp_ref, vcmp_ref, ks_ref, vs_ref, kw_ref, vw_ref, gl_ref,
                 o_ref, p_scr, imp_scr, sel_scr, ocmp_scr, acc_slc, acc_win, flag_ref,
                 *, hpg, nc, ncp, nb, n_sel):
    g = pl.program_id(0)
    i = pl.program_id(1)
    t0 = i * QB
    w = hpg * QB
    ninf = -jnp.inf

    q = q_ref[...]
    qs = jnp.concatenate([q[:, h * HEAD_DIM:(h + 1) * HEAD_DIM] for h in range(hpg)], axis=0)
    slopes = [slopes_ref[g * hpg + h] for h in range(hpg)]

    s = lax.dot_general(kcmp_ref[...], qs, _NT, preferred_element_type=F32)
    c_iota = lax.broadcasted_iota(jnp.int32, (ncp, QB), 0)
    q_iota = lax.broadcasted_iota(jnp.int32, (ncp, QB), 1)
    cmp_end = c_iota * CMP_STRIDE + (CMP_BLOCK - 1)
    mask_c = (cmp_end <= t0 + q_iota) & (c_iota < nc)
    rel_c = (cmp_end - t0).astype(F32)
    psum = jnp.zeros((ncp, QB), F32)
    for h in range(hpg):
        sh = s[:, h * QB:(h + 1) * QB] + slopes[h] * rel_c
        sh = jnp.where(mask_c, sh, ninf)
        m = jnp.maximum(jnp.max(sh, axis=0, keepdims=True), NEG_INF)
        p = jnp.exp(sh - m)
        l = jnp.sum(p, axis=0, keepdims=True)
        pn = p * jnp.where(l > 0.0, 1.0 / l, 0.0)
        psum = psum + pn
        ocmp_scr[:, h * QB:(h + 1) * QB] = lax.dot_general(vcmp_ref[...], pn.astype(BF16), _TN,
                                                            preferred_element_type=F32)

    p_scr[0:SUBLANES, :] = jnp.zeros((SUBLANES, QB), F32)
    p_scr[SUBLANES:SUBLANES + ncp, :] = psum
    imp = p_scr[pl.ds(SUBLANES - 1, nb, stride=4), :]
    for j in range(4):
        imp = imp + p_scr[pl.ds(SUBLANES + j, nb, stride=4), :]
    n_iota = lax.broadcasted_iota(jnp.int32, (nb, QB), 0)
    tq = t0 + lax.broadcasted_iota(jnp.int32, (nb, QB), 1)
    cur = lax.shift_right_logical(tq, 6)
    valid = n_iota <= cur
    back = cur - n_iota
    forced = (n_iota == 0) | ((back >= 0) & (back < N_LOCAL_FORCED))
    imp = jnp.where(forced & valid, FORCED_SCORE, imp)
    imp = jnp.where(valid, imp, NEG_INF)
    imp_scr[...] = imp

    n_grp = nb // SUBLANES
    vals = [imp[j * SUBLANES:(j + 1) * SUBLANES, :] for j in range(n_grp)]
    cnts = [jnp.zeros((SUBLANES, QB), F32) for _ in range(n_grp)]
    row8 = lax.broadcasted_iota(jnp.int32, (SUBLANES, QB), 0)
    for mi in range(nb):
        vm = jnp.broadcast_to(imp_scr[pl.ds(mi, 1), :], (SUBLANES, QB))
        for j in range(n_grp):
            lo = j * SUBLANES
            if lo + SUBLANES - 1 < mi:
                beats = vm > vals[j]
            elif lo > mi:
                beats = vm >= vals[j]
            else:
                beats = (vm > vals[j]) | ((vm == vals[j]) & (row8 > mi - lo))
            cnts[j] = cnts[j] + jnp.where(beats, 1.0, 0.0)
    cnt = jnp.concatenate(cnts, axis=0)
    sel = (cnt < float(n_sel)) & valid
    sel_scr[...] = jnp.where(sel, 0.0, ninf)
    blocks_per_kt = KT // SLC_BLOCK
    sel_f = jnp.where(sel, 1.0, 0.0)
    for kt in range(nb // blocks_per_kt):
        any_sel = jnp.max(sel_f[kt * blocks_per_kt:(kt + 1) * blocks_per_kt, :])
        flag_ref[kt] = (any_sel > 0.5).astype(jnp.int32)

    def flash_tile(k_rows, v_rows, acc_ref, k0, base_of, carry):
        ms, ls = carry
        rows = k_rows.shape[0]
        st = lax.dot_general(k_rows, qs, _NT, preferred_element_type=F32)
        r_iota = lax.broadcasted_iota(jnp.int32, (rows, QB), 0)
        k_iota = lax.broadcasted_iota(jnp.int32, (rows, QB), 1)
        diff = (t0 - k0) + (k_iota - r_iota)
        base = base_of(diff)
        rel = (-diff).astype(F32)
        new_m, new_l, alphas, ps = [], [], [], []
        for h in range(hpg):
            sh = st[:, h * QB:(h + 1) * QB] + (slopes[h] * rel + base)
            mh = jnp.maximum(ms[h], jnp.max(sh, axis=0, keepdims=True))
            alpha = jnp.exp(ms[h] - mh)
            p = jnp.exp(sh - mh)
            new_m.append(mh)
            new_l.append(alpha * ls[h] + jnp.sum(p, axis=0, keepdims=True))
            alphas.append(alpha)
            ps.append(p.astype(BF16))
        pv = lax.dot_general(v_rows, jnp.concatenate(ps, axis=1), _TN, preferred_element_type=F32)
        acc_ref[...] = acc_ref[...] * jnp.concatenate(alphas, axis=1) + pv
        return tuple(new_m), tuple(new_l)

    init = (tuple(jnp.full((1, QB), NEG_INF, F32) for _ in range(hpg)),
            tuple(jnp.zeros((1, QB), F32) for _ in range(hpg)))

    def slc_step(kt, carry):
        k0 = pl.multiple_of(kt * KT, KT)

        def base_of(diff):
            rows = [jnp.broadcast_to(sel_scr[pl.ds(kt * blocks_per_kt + b, 1), :], (SLC_BLOCK, QB))
                    for b in range(blocks_per_kt)]
            return jnp.where(diff >= 0, jnp.concatenate(rows, axis=0), ninf)

        return flash_tile(ks_ref[pl.ds(k0, KT), :], vs_ref[pl.ds(k0, KT), :], acc_slc, k0, base_of, carry)

    acc_slc[...] = jnp.zeros_like(acc_slc)
    _, l_slc = lax.fori_loop(
        0, t0 // KT + 1,
        lambda kt, c: lax.cond(flag_ref[kt] > 0, lambda cc: slc_step(kt, cc), lambda cc: cc, c), init)

    w0 = pl.multiple_of(jnp.maximum(t0 - WINDOW, 0), KB)
    acc_win[...] = jnp.zeros_like(acc_win)
    _, l_win = flash_tile(kw_ref[pl.ds(w0, WIN_KEYS), :], vw_ref[pl.ds(w0, WIN_KEYS), :], acc_win, w0,
                          lambda diff: jnp.where((diff >= 0) & (diff < WINDOW), 0.0, ninf), init)

    for h in range(hpg):
        cols = slice(h * QB, (h + 1) * QB)
        g_cmp = jax.nn.sigmoid(gl_ref[0, h:h + 1, :])
        g_slc = jax.nn.sigmoid(gl_ref[1, h:h + 1, :])
        g_win = jax.nn.sigmoid(gl_ref[2, h:h + 1, :])
        oT = (g_cmp * ocmp_scr[:, cols] + (g_slc / l_slc[h]) * acc_slc[:, cols]
              + (g_win / l_win[h]) * acc_win[:, cols])
        o_ref[:, h * HEAD_DIM:(h + 1) * HEAD_DIM] = oT.T.astype(o_ref.dtype)


def sparse_attention(slopes, q, cmp_kv, kv_rest, gate_logits_t, *, nc):
    t, hd = q.shape
    _, n_groups, ncp, _ = cmp_kv.shape
    hpg = hd // HEAD_DIM // n_groups
    nb = t // SLC_BLOCK
    assert t % KT == 0 and t >= WIN_KEYS and ncp == 4 * nb and nb % SUBLANES == 0
    assert (CMP_BLOCK, CMP_STRIDE, SLC_BLOCK) == (32, 16, 64)
    n_sel = min(N_SELECT, nb)
    w = hpg * QB
    kern = functools.partial(_attn_kernel, hpg=hpg, nc=nc, ncp=ncp, nb=nb, n_sel=n_sel)
    kv_col = lambda part: (lambda g, i: (0, part * n_groups + g))
    return pl.pallas_call(
        kern,
        grid=(n_groups, t // QB),
        in_specs=[
            pl.BlockSpec(memory_space=pltpu.SMEM),
            pl.BlockSpec((QB, hpg * HEAD_DIM), lambda g, i: (i, g)),
            pl.BlockSpec((None, None, ncp, HEAD_DIM), lambda g, i: (0, g, 0, 0)),
            pl.BlockSpec((None, None, ncp, HEAD_DIM), lambda g, i: (1, g, 0, 0)),
            pl.BlockSpec((t, HEAD_DIM), kv_col(0)),
            pl.BlockSpec((t, HEAD_DIM), kv_col(1)),
            pl.BlockSpec((t, HEAD_DIM), kv_col(2)),
            pl.BlockSpec((t, HEAD_DIM), kv_col(3)),
            pl.BlockSpec((3, None, hpg, QB), lambda g, i: (0, g, 0, i)),
        ],
        out_specs=pl.BlockSpec((QB, hpg * HEAD_DIM), lambda g, i: (i, g)),
        out_shape=jax.ShapeDtypeStruct((t, hd), BF16),
        scratch_shapes=[
            pltpu.VMEM((SUBLANES + ncp, QB), F32),
            pltpu.VMEM((nb, QB), F32),
            pltpu.VMEM((nb, QB), F32),
            pltpu.VMEM((HEAD_DIM, w), F32),
            pltpu.VMEM((HEAD_DIM, w), F32),
            pltpu.VMEM((HEAD_DIM, w), F32),
            pltpu.SMEM((t // KT,), jnp.int32),
        ],
        name="sparse_attention",
        compiler_params=_params("parallel", "arbitrary"),
    )(slopes, q, cmp_kv, cmp_kv, kv_rest, kv_rest, kv_rest, kv_rest, gate_logits_t)


def _conv_kernel(b_ref, c_ref, h_ref, cp_ref, hp_ref, w_ref, o_ref):
    i = pl.program_id(0)
    wk = w_ref[...]
    w0, w1, w2 = wk[0:1, :], wk[1:2, :], wk[2:3, :]
    u = c_ref[...] * h_ref[...]
    tt = u.shape[0]
    y = b_ref[...] * (w0 * pltpu.roll(u, 2, 0) + w1 * pltpu.roll(u, 1, 0) + w2 * u)
    o_ref[...] = y.astype(o_ref.dtype)
    u_prev = jnp.where(i > 0, cp_ref[...] * hp_ref[...], 0.0)
    ue = jnp.concatenate([u_prev, u[0:SUBLANES, :]], axis=0)
    n2 = 2 * SUBLANES
    u1 = pltpu.roll(ue, 1, 0)[SUBLANES:n2, :]
    u2 = pltpu.roll(ue, 2, 0)[SUBLANES:n2, :]
    y0 = b_ref[0:SUBLANES, :] * (w0 * u2 + w1 * u1 + w2 * u[0:SUBLANES, :])
    o_ref[0:SUBLANES, :] = y0.astype(o_ref.dtype)


def short_conv(conv_in, conv_w, tt=512, tc=512):
    t, cw3 = conv_in.shape
    cw = cw3 // 3
    tt, tc = _tile(t, tt), _tile(cw, tc)
    ncb = cw // tc
    rb = tt // SUBLANES
    prev = lambda i, j, off: (jnp.maximum(i * rb - 1, 0), j + off)
    return pl.pallas_call(
        _conv_kernel,
        grid=(t // tt, ncb),
        in_specs=[
            pl.BlockSpec((tt, tc), lambda i, j: (i, j)),
            pl.BlockSpec((tt, tc), lambda i, j: (i, j + ncb)),
            pl.BlockSpec((tt, tc), lambda i, j: (i, j + 2 * ncb)),
            pl.BlockSpec((SUBLANES, tc), functools.partial(prev, off=ncb)),
            pl.BlockSpec((SUBLANES, tc), functools.partial(prev, off=2 * ncb)),
            pl.BlockSpec((CONV_K, tc), lambda i, j: (0, j)),
        ],
        out_specs=pl.BlockSpec((tt, tc), lambda i, j: (i, j)),
        out_shape=jax.ShapeDtypeStruct((t, cw), BF16),
        name="short_conv",
        compiler_params=_params("parallel", "parallel"),
    )(conv_in, conv_in, conv_in, conv_in, conv_in, conv_w)


def _mem_kernel(x_ref, gpre_ref, gpost_ref, wq_ref, km_ref, vm_ref, wo_ref, o_ref):
    x = x_ref[...]
    xn = _rms(x, gpre_ref[...]).astype(BF16)
    qm = jnp.dot(xn, wq_ref[...], preferred_element_type=F32) * (MEM_HEAD_DIM ** -0.5)
    outs = []
    for h in range(MEM_HEADS):
        cols = slice(h * MEM_HEAD_DIM, (h + 1) * MEM_HEAD_DIM)
        s = lax.dot_general(qm[:, cols].astype(BF16), km_ref[:, cols], _NT, preferred_element_type=F32)
        p = jnp.exp(s - jnp.max(s, axis=-1, keepdims=True))
        p = p / jnp.sum(p, axis=-1, keepdims=True)
        outs.append(jnp.dot(p.astype(BF16), vm_ref[:, cols], preferred_element_type=F32).astype(BF16))
    y = jnp.dot(jnp.concatenate(outs, axis=1), wo_ref[...], preferred_element_type=F32)
    o_ref[...] = x + _rms(y, gpost_ref[...])


def memory_sublayer(x, gpre, gpost, wq, km, vm, wo, tm=256):
    m, d = x.shape
    mw = wq.shape[1]
    nm = km.shape[0]
    tm = _tile(m, tm)
    full = lambda i: (0, 0)
    return pl.pallas_call(
        _mem_kernel,
        grid=(m // tm,),
        in_specs=[
            pl.BlockSpec((tm, d), lambda i: (i, 0)),
            pl.BlockSpec((1, d), full),
            pl.BlockSpec((1, d), full),
            pl.BlockSpec((d, mw), full),
            pl.BlockSpec((nm, mw), full),
            pl.BlockSpec((nm, mw), full),
            pl.BlockSpec((mw, d), full),
        ],
        out_specs=pl.BlockSpec((tm, d), lambda i: (i, 0)),
        out_shape=jax.ShapeDtypeStruct((m, d), F32),
        name="memory_sublayer",
        compiler_params=_params("parallel"),
    )(x, gpre.reshape(1, d), gpost.reshape(1, d), wq, km, vm, wo)


def _mlp_kernel(x_ref, gpre_ref, gpost_ref, wu_ref, wd_ref, o_ref, xn_scr, *, nf):
    f = pl.program_id(1)

    @pl.when(f == 0)
    def _():
        xn_scr[...] = _rms(x_ref[...], gpre_ref[...]).astype(BF16)
        o_ref[...] = jnp.zeros_like(o_ref)

    hid = jnp.dot(xn_scr[...], wu_ref[...], preferred_element_type=F32)
    hid = jnp.square(jnp.maximum(hid, 0.0))
    o_ref[...] += jnp.dot(hid.astype(BF16), wd_ref[...], preferred_element_type=F32)

    @pl.when(f == nf - 1)
    def _():
        o_ref[...] = x_ref[...] + _rms(o_ref[...], gpost_ref[...])


def mlp_sublayer(x, gpre, gpost, w_up, w_down, tm=512, tf=512):
    m, d = x.shape
    dff = w_up.shape[1]
    tm, tf = _tile(m, tm), _tile(dff, tf)
    nf = dff // tf
    return pl.pallas_call(
        functools.partial(_mlp_kernel, nf=nf),
        grid=(m // tm, nf),
        in_specs=[
            pl.BlockSpec((tm, d), lambda i, f: (i, 0), pipeline_mode=pl.Buffered(1)),
            pl.BlockSpec((1, d), lambda i, f: (0, 0)),
            pl.BlockSpec((1, d), lambda i, f: (0, 0)),
            pl.BlockSpec((d, tf), lambda i, f: (0, f)),
            pl.BlockSpec((tf, d), lambda i, f: (f, 0)),
        ],
        out_specs=pl.BlockSpec((tm, d), lambda i, f: (i, 0)),
        out_shape=jax.ShapeDtypeStruct((m, d), F32),
        scratch_shapes=[pltpu.VMEM((tm, d), BF16)],
        name="mlp_sublayer",
        compiler_params=_params("parallel", "arbitrary"),
    )(x, gpre.reshape(1, d), gpost.reshape(1, d), w_up, w_down)


def _mixer(x, norm_pre, norm_post, w_in, cmp_k_pe, cmp_k_w1, cmp_k_w2, cmp_v_pe, cmp_v_w1, cmp_v_w2,
           conv_w, w_out):
    t, d = x.shape
    attn_w = d // 2
    conv_cw = d - attn_w
    n_heads = attn_w // HEAD_DIM
    n_groups = n_heads // 4
    kvw = n_groups * HEAD_DIM
    o_kv = attn_w
    o_gate = o_kv + 6 * kvw
    o_conv = o_gate + 3 * n_heads
    assert w_in.shape[1] == o_conv + 3 * conv_cw

    w_gate = jnp.pad(w_in[:, o_gate:o_conv], ((0, 0), (0, LANES - 3 * n_heads)))
    w_all = jnp.concatenate([w_in[:, :o_gate], w_in[:, o_conv:], w_gate], axis=1).astype(BF16)
    c_conv = o_gate
    c_gate = o_gate + 3 * conv_cw

    xn = rmsnorm_cast(x, norm_pre)
    q = matmul(xn, w_all, BF16, scale=HEAD_DIM ** -0.5, col0=0, n=attn_w)
    kvc = matmul(xn, w_all, F32, col0=o_kv, n=2 * kvw)
    kv_rest = matmul(xn, w_all, BF16, col0=o_kv + 2 * kvw, n=4 * kvw)
    conv_in = matmul(xn, w_all, F32, col0=c_conv, n=3 * conv_cw)
    gate_logits = matmul(xn, w_all, F32, col0=c_gate, n=LANES)[:, :3 * n_heads]

    nc = (t - CMP_BLOCK) // CMP_STRIDE + 1
    pe = jnp.stack([cmp_k_pe, cmp_v_pe])
    w1 = jnp.stack([cmp_k_w1, cmp_v_w1]).astype(BF16)
    w2 = jnp.stack([cmp_k_w2, cmp_v_w2]).astype(BF16)
    cmp_kv = compress_tokens(kvc, pe, w1, w2, n_groups)

    gl_t = gate_logits.T.reshape(3, n_groups, n_heads // n_groups, t)
    idx = jnp.arange(1, n_heads + 1, dtype=F32)
    slopes = jnp.exp2(-8.0 * idx / n_heads)
    o_attn = sparse_attention(slopes, q, cmp_kv, kv_rest, gl_t, nc=nc)

    o_conv_out = short_conv(conv_in, conv_w)
    return matmul_postnorm_residual(o_attn, o_conv_out, w_out.astype(BF16), x, norm_post)


def _memory(x, mem, norm_pre, norm_kv, norm_post, wq, wk, wv, wo):
    mn = rmsnorm_cast(mem, norm_kv)
    km = matmul(mn, wk.astype(BF16), BF16)
    vm = matmul(mn, wv.astype(BF16), BF16)
    return memory_sublayer(x, norm_pre, norm_post, wq.astype(BF16), km, vm, wo.astype(BF16))


def kernel(x, mem, mix_norm_pre, mix_norm_post, w_in, cmp_k_pe, cmp_k_w1, cmp_k_w2, cmp_v_pe, cmp_v_w1,
           cmp_v_w2, conv_w, w_out, mem_norm_pre, mem_norm_kv, mem_norm_post, w_mem_q, w_mem_k, w_mem_v,
           w_mem_o, mlp_norm_pre, mlp_norm_post, w_up, w_down):
    b, t, d = x.shape
    assert b == 1
    h = x[0]
    m = mem[0]
    for l in range(w_in.shape[0]):
        h = _mixer(h, mix_norm_pre[l], mix_norm_post[l], w_in[l], cmp_k_pe[l], cmp_k_w1[l], cmp_k_w2[l],
                   cmp_v_pe[l], cmp_v_w1[l], cmp_v_w2[l], conv_w[l], w_out[l])
        h = _memory(h, m, mem_norm_pre[l], mem_norm_kv[l], mem_norm_post[l], w_mem_q[l], w_mem_k[l],
                    w_mem_v[l], w_mem_o[l])
        h = mlp_sublayer(h, mlp_norm_pre[l], mlp_norm_post[l], w_up[l].astype(BF16), w_down[l].astype(BF16))
    return h[None]
```

```python
import functools

import jax
import jax.numpy as jnp
from jax import lax
from jax.experimental import pallas as pl
from jax.experimental.pallas import tpu as pltpu

HEAD_DIM = 128
CMP_BLOCK = 32
CMP_STRIDE = 16
SLC_BLOCK = 64
N_SELECT = 16
N_LOCAL_FORCED = 2
WINDOW = 512
CONV_K = 3
MEM_HEADS = 4
MEM_HEAD_DIM = 128
RMS_EPS = 1e-6
NEG_INF = -1e30
FORCED_SCORE = 1e9
LOG2E = 1.4426950408889634

V7X_VMEM_BYTES = 64 * 1024 * 1024
VMEM_LIMIT = V7X_VMEM_BYTES - 8 * 1024 * 1024
LANES = 128
SUBLANES = 8

QB = 128
KB = 128
KT = 512
WIN_KEYS = WINDOW + QB

F32 = jnp.float32
BF16 = jnp.bfloat16

_NT = (((1,), (1,)), ((), ()))
_TN = (((0,), (0,)), ((), ()))


def _tile(n, pref):
    t = min(n, pref)
    while n % t:
        t -= 1
    return t


def _params(*sem):
    return pltpu.CompilerParams(dimension_semantics=sem, vmem_limit_bytes=VMEM_LIMIT)


def _rms(x, gain):
    return x * lax.rsqrt(jnp.mean(x * x, axis=-1, keepdims=True) + RMS_EPS) * gain


def _rmsnorm_kernel(x_ref, g_ref, o_ref):
    o_ref[...] = _rms(x_ref[...].astype(F32), g_ref[...]).astype(o_ref.dtype)


def rmsnorm_cast(x, gain, out_dtype=BF16):
    m, d = x.shape
    tm = _tile(m, 256)
    return pl.pallas_call(
        _rmsnorm_kernel,
        grid=(m // tm,),
        in_specs=[pl.BlockSpec((tm, d), lambda i: (i, 0)), pl.BlockSpec((1, d), lambda i: (0, 0))],
        out_specs=pl.BlockSpec((tm, d), lambda i: (i, 0)),
        out_shape=jax.ShapeDtypeStruct((m, d), out_dtype),
        name="rmsnorm_cast",
        compiler_params=_params("parallel"),
    )(x, gain.reshape(1, d))


def _mm_kernel(a_ref, w_ref, o_ref, *, scale):
    acc = jnp.dot(a_ref[...], w_ref[...], preferred_element_type=F32)
    if scale is not None:
        acc = acc * scale
    o_ref[...] = acc.astype(o_ref.dtype)


def matmul(a, w, out_dtype, scale=None, col0=0, n=None, tm=1024, tn=1024):
    m, k = a.shape
    n = w.shape[1] - col0 if n is None else n
    tm, tn = _tile(m, tm), _tile(n, tn)
    while col0 % tn:
        tn = _tile(n, tn - 1)
    jb = col0 // tn
    return pl.pallas_call(
        functools.partial(_mm_kernel, scale=scale),
        grid=(m // tm, n // tn),
        in_specs=[pl.BlockSpec((tm, k), lambda i, j: (i, 0)), pl.BlockSpec((k, tn), lambda i, j: (0, j + jb))],
        out_specs=pl.BlockSpec((tm, tn), lambda i, j: (i, j)),
        out_shape=jax.ShapeDtypeStruct((m, n), out_dtype),
        name="matmul",
        compiler_params=_params("parallel", "arbitrary"),
    )(a, w)


def _mm_postnorm_kernel(a1_ref, a2_ref, w_ref, x_ref, g_ref, o_ref, *, nk1, nk):
    k = pl.program_id(1)

    @pl.when(k == 0)
    def _():
        o_ref[...] = jnp.zeros_like(o_ref)

    @pl.when(k < nk1)
    def _():
        o_ref[...] += jnp.dot(a1_ref[...], w_ref[...], preferred_element_type=F32)

    @pl.when(k >= nk1)
    def _():
        o_ref[...] += jnp.dot(a2_ref[...], w_ref[...], preferred_element_type=F32)

    @pl.when(k == nk - 1)
    def _():
        o_ref[...] = x_ref[...] + _rms(o_ref[...], g_ref[...])


def matmul_postnorm_residual(a1, a2, w, x, gain, tm=512, tk=1024):
    m, k1 = a1.shape
    k2 = a2.shape[1]
    _, n = w.shape
    tm = _tile(m, tm)
    tk = _tile(k1, tk)
    while k2 % tk:
        tk = _tile(k1, tk - 1)
    nk1, nk = k1 // tk, (k1 + k2) // tk
    return pl.pallas_call(
        functools.partial(_mm_postnorm_kernel, nk1=nk1, nk=nk),
        grid=(m // tm, nk),
        in_specs=[
            pl.BlockSpec((tm, tk), lambda i, k: (i, jnp.minimum(k, nk1 - 1))),
            pl.BlockSpec((tm, tk), lambda i, k: (i, jnp.maximum(k - nk1, 0))),
            pl.BlockSpec((tk, n), lambda i, k: (k, 0)),
            pl.BlockSpec((tm, n), lambda i, k: (i, 0), pipeline_mode=pl.Buffered(1)),
            pl.BlockSpec((1, n), lambda i, k: (0, 0)),
        ],
        out_specs=pl.BlockSpec((tm, n), lambda i, k: (i, 0)),
        out_shape=jax.ShapeDtypeStruct((m, n), F32),
        name="outproj_postnorm",
        compiler_params=_params("parallel", "arbitrary"),
    )(a1, a2, w, x, gain.reshape(1, n))


def _gelu_tanh(x):
    c = 0.7978845608028654
    return x * (0.5 * (1.0 + jnp.tanh(c * (x + 0.044715 * (x * x * x)))))


def _compress_kernel(x_ref, pe_ref, w1_ref, w2_ref, o_ref, *, ncp):
    a = jnp.zeros((ncp, HEAD_DIM), F32)
    b = jnp.zeros((ncp, HEAD_DIM), F32)
    for l in range(CMP_STRIDE):
        xl = x_ref[pl.ds(l, ncp, stride=CMP_STRIDE), :]
        lo, hi = l, CMP_STRIDE + l
        a = a + jnp.dot((xl + pe_ref[lo:lo + 1, :]).astype(BF16), w1_ref[lo * HEAD_DIM:(lo + 1) * HEAD_DIM, :],
                        preferred_element_type=F32)
        b = b + jnp.dot((xl + pe_ref[hi:hi + 1, :]).astype(BF16), w1_ref[hi * HEAD_DIM:(hi + 1) * HEAD_DIM, :],
                        preferred_element_type=F32)
    hid = _gelu_tanh(a + pltpu.roll(b, ncp - 1, 0))
    o_ref[...] = jnp.dot(hid.astype(BF16), w2_ref[...], preferred_element_type=F32).astype(o_ref.dtype)


def compress_tokens(kvc, pe, w1, w2, n_groups):
    t = kvc.shape[0]
    ncp = t // CMP_STRIDE
    assert CMP_BLOCK == 2 * CMP_STRIDE
    return pl.pallas_call(
        functools.partial(_compress_kernel, ncp=ncp),
        grid=(2, n_groups),
        in_specs=[
            pl.BlockSpec((t, HEAD_DIM), lambda s, j: (0, s * n_groups + j)),
            pl.BlockSpec((None, CMP_BLOCK, HEAD_DIM), lambda s, j: (s, 0, 0)),
            pl.BlockSpec((None, CMP_BLOCK * HEAD_DIM, HEAD_DIM), lambda s, j: (s, 0, 0)),
            pl.BlockSpec((None, HEAD_DIM, HEAD_DIM), lambda s, j: (s, 0, 0)),
        ],
        out_specs=pl.BlockSpec((None, None, ncp, HEAD_DIM), lambda s, j: (s, j, 0, 0)),
        out_shape=jax.ShapeDtypeStruct((2, n_groups, ncp, HEAD_DIM), BF16),
        name="compress_tokens",
        compiler_params=_params("parallel", "parallel"),
    )(kvc, pe, w1, w2)


def _attn_kernel(slopes_ref, q_ref, kcmp_ref, vcmp_ref, ks_ref, vs_ref, kw_ref, vw_ref, gl_ref,
                 o_ref, p_scr, imp_scr, sel_scr, ocmp_scr, acc_slc, acc_win, alibi_scr, u_scr, flag_ref,
                 *, hpg, nc, ncp, nb, n_sel):
    g = pl.program_id(0)
    i = pl.program_id(1)
    t0 = i * QB
    w = hpg * QB
    ninf = -jnp.inf

    q = q_ref[...]
    qs = jnp.concatenate([q[:, h * HEAD_DIM:(h + 1) * HEAD_DIM] for h in range(hpg)], axis=0)
    slopes = [slopes_ref[g * hpg + h] for h in range(hpg)]

    s = lax.dot_general(kcmp_ref[...], qs, _NT, preferred_element_type=F32)
    c_iota = lax.broadcasted_iota(jnp.int32, (ncp, QB), 0)
    q_iota = lax.broadcasted_iota(jnp.int32, (ncp, QB), 1)
    cmp_end = c_iota * CMP_STRIDE + (CMP_BLOCK - 1)
    mask_c = (cmp_end <= t0 + q_iota) & (c_iota < nc)
    rel_c = (cmp_end - t0).astype(F32)
    psum = jnp.zeros((ncp, QB), F32)
    for h in range(hpg):
        sh = s[:, h * QB:(h + 1) * QB] + slopes[h] * rel_c
        sh = jnp.where(mask_c, sh, ninf)
        m = jnp.maximum(jnp.max(sh, axis=0, keepdims=True), NEG_INF)
        p = jnp.exp2(sh - m)
        l = jnp.sum(p, axis=0, keepdims=True)
        pn = p * jnp.where(l > 0.0, 1.0 / l, 0.0)
        psum = psum + pn
        ocmp_scr[:, h * QB:(h + 1) * QB] = lax.dot_general(vcmp_ref[...], pn.astype(BF16), _TN,
                                                            preferred_element_type=F32)

    p_scr[0:SUBLANES, :] = jnp.zeros((SUBLANES, QB), F32)
    p_scr[SUBLANES:SUBLANES + ncp, :] = psum
    imp = p_scr[pl.ds(SUBLANES - 1, nb, stride=4), :]
    for j in range(4):
        imp = imp + p_scr[pl.ds(SUBLANES + j, nb, stride=4), :]
    n_iota = lax.broadcasted_iota(jnp.int32, (nb, QB), 0)
    tq = t0 + lax.broadcasted_iota(jnp.int32, (nb, QB), 1)
    cur = lax.shift_right_logical(tq, 6)
    valid = n_iota <= cur
    back = cur - n_iota
    forced = (n_iota == 0) | ((back >= 0) & (back < N_LOCAL_FORCED))
    imp = jnp.where(forced & valid, FORCED_SCORE, imp)
    imp = jnp.where(valid, imp, NEG_INF)
    imp_scr[...] = imp

    n_grp = nb // SUBLANES
    vals = [imp[j * SUBLANES:(j + 1) * SUBLANES, :] for j in range(n_grp)]
    row8 = lax.broadcasted_iota(jnp.int32, (SUBLANES, QB), 0)

    def rank_chunk(mc, cnts):
        cnts = list(cnts)
        for mi in range(mc * SUBLANES, (mc + 1) * SUBLANES):
            vm = jnp.broadcast_to(imp_scr[pl.ds(mi, 1), :], (SUBLANES, QB))
            for j in range(n_grp):
                lo = j * SUBLANES
                if lo + SUBLANES - 1 < mi:
                    beats = vm > vals[j]
                elif lo > mi:
                    beats = vm >= vals[j]
                else:
                    beats = (vm > vals[j]) | ((vm == vals[j]) & (row8 > mi - lo))
                cnts[j] = cnts[j] + jnp.where(beats, 1.0, 0.0)
        return tuple(cnts)

    last_valid = (t0 + QB - 1) // SLC_BLOCK
    cnts = tuple(jnp.zeros((SUBLANES, QB), F32) for _ in range(n_grp))
    for mc in range(n_grp):
        cnts = lax.cond(mc * SUBLANES <= last_valid, functools.partial(rank_chunk, mc), lambda c: c, cnts)
    cnt = jnp.concatenate(cnts, axis=0)
    sel = (cnt < float(n_sel)) & valid
    sel_scr[...] = jnp.where(sel, 0.0, ninf)
    blocks_per_kt = KT // SLC_BLOCK
    sel_f = jnp.where(sel, 1.0, 0.0)
    for kt in range(nb // blocks_per_kt):
        any_sel = jnp.max(sel_f[kt * blocks_per_kt:(kt + 1) * blocks_per_kt, :])
        flag_ref[kt] = (any_sel > 0.5).astype(jnp.int32)

    rc_i = (lax.broadcasted_iota(jnp.int32, (WIN_KEYS, QB), 1)
            - lax.broadcasted_iota(jnp.int32, (WIN_KEYS, QB), 0))
    rc_f = (-rc_i).astype(F32)
    for h in range(hpg):
        alibi_scr[h] = slopes[h] * rc_f

    init = (tuple(jnp.full((1, QB), NEG_INF, F32) for _ in range(hpg)),
            tuple(jnp.zeros((1, QB), F32) for _ in range(hpg)))

    def stage_a(kt, slot):
        k0 = pl.multiple_of(kt * KT, KT)
        st = lax.dot_general(ks_ref[pl.ds(k0, KT), :], qs, _NT, preferred_element_type=F32)
        rows = [jnp.broadcast_to(sel_scr[pl.ds(kt * blocks_per_kt + b, 1), :], (SLC_BLOCK, QB))
                for b in range(blocks_per_kt)]
        diff = (t0 - k0) + rc_i[0:KT, :]
        base = jnp.where(diff >= 0, jnp.concatenate(rows, axis=0), ninf)
        off = (k0 - t0).astype(F32)
        tile_max = []
        for h in range(hpg):
            u = st[:, h * QB:(h + 1) * QB] + (alibi_scr[h, 0:KT, :] + base)
            u_scr[slot, h] = u
            tile_max.append(jnp.max(u, axis=0, keepdims=True) + slopes[h] * off)
        return tuple(tile_max)

    def stage_b(kt, slot, tile_max, ms, ls):
        k0 = pl.multiple_of(kt * KT, KT)
        off = (k0 - t0).astype(F32)
        new_m, new_l, alphas, ps = [], [], [], []
        for h in range(hpg):
            mh = jnp.maximum(ms[h], tile_max[h])
            alpha = jnp.exp2(ms[h] - mh)
            p = jnp.exp2(u_scr[slot, h] - (mh - slopes[h] * off))
            new_m.append(mh)
            new_l.append(alpha * ls[h] + jnp.sum(p, axis=0, keepdims=True))
            alphas.append(alpha)
            ps.append(p.astype(BF16))
        pv = lax.dot_general(vs_ref[pl.ds(k0, KT), :], jnp.concatenate(ps, axis=1), _TN,
                             preferred_element_type=F32)
        acc_slc[...] = acc_slc[...] * jnp.concatenate(alphas, axis=1) + pv
        return tuple(new_m), tuple(new_l)

    def slc_step(kt, carry):
        def visit(c):
            ms, ls, tile_max, prev, slot = c
            ms, ls = stage_b(prev, slot, tile_max, ms, ls)
            return ms, ls, stage_a(kt, 1 - slot), kt, 1 - slot
        return lax.cond(flag_ref[kt] > 0, visit, lambda c: c, carry)

    kd = t0 // KT
    acc_slc[...] = jnp.zeros_like(acc_slc)
    zero = jnp.int32(0)
    ms, ls, tile_max, prev, slot = lax.fori_loop(1, kd + 1, slc_step, init + (stage_a(zero, zero), zero, zero))
    _, l_slc = stage_b(prev, slot, tile_max, ms, ls)

    w0 = pl.multiple_of(jnp.maximum(t0 - WINDOW, 0), KB)
    diff_w = (t0 - w0) + rc_i
    base_w = jnp.where((diff_w >= 0) & (diff_w < WINDOW), 0.0, ninf)
    st_w = lax.dot_general(kw_ref[pl.ds(w0, WIN_KEYS), :], qs, _NT, preferred_element_type=F32)
    l_win, ps = [], []
    for h in range(hpg):
        u = st_w[:, h * QB:(h + 1) * QB] + (alibi_scr[h] + base_w)
        p = jnp.exp2(u - jnp.max(u, axis=0, keepdims=True))
        l_win.append(jnp.sum(p, axis=0, keepdims=True))
        ps.append(p.astype(BF16))
    acc_win[...] = lax.dot_general(vw_ref[pl.ds(w0, WIN_KEYS), :], jnp.concatenate(ps, axis=1), _TN,
                                   preferred_element_type=F32)

    for h in range(hpg):
        cols = slice(h * QB, (h + 1) * QB)
        g_cmp = jax.nn.sigmoid(gl_ref[0, h:h + 1, :])
        g_slc = jax.nn.sigmoid(gl_ref[1, h:h + 1, :])
        g_win = jax.nn.sigmoid(gl_ref[2, h:h + 1, :])
        oT = (g_cmp * ocmp_scr[:, cols] + (g_slc / l_slc[h]) * acc_slc[:, cols]
              + (g_win / l_win[h]) * acc_win[:, cols])
        o_ref[:, h * HEAD_DIM:(h + 1) * HEAD_DIM] = oT.T.astype(o_ref.dtype)


def sparse_attention(slopes, q, cmp_kv, kv_rest, gate_logits_t, *, nc):
    t, hd = q.shape
    _, n_groups, ncp, _ = cmp_kv.shape
    hpg = hd // HEAD_DIM // n_groups
    nb = t // SLC_BLOCK
    assert t % KT == 0 and t >= WIN_KEYS and ncp == 4 * nb and nb % SUBLANES == 0
    assert (CMP_BLOCK, CMP_STRIDE, SLC_BLOCK) == (32, 16, 64)
    n_sel = min(N_SELECT, nb)
    w = hpg * QB
    kern = functools.partial(_attn_kernel, hpg=hpg, nc=nc, ncp=ncp, nb=nb, n_sel=n_sel)
    kv_col = lambda part: (lambda g, i: (0, part * n_groups + g))
    return pl.pallas_call(
        kern,
        grid=(n_groups, t // QB),
        in_specs=[
            pl.BlockSpec(memory_space=pltpu.SMEM),
            pl.BlockSpec((QB, hpg * HEAD_DIM), lambda g, i: (i, g)),
            pl.BlockSpec((None, None, ncp, HEAD_DIM), lambda g, i: (0, g, 0, 0)),
            pl.BlockSpec((None, None, ncp, HEAD_DIM), lambda g, i: (1, g, 0, 0)),
            pl.BlockSpec((t, HEAD_DIM), kv_col(0)),
            pl.BlockSpec((t, HEAD_DIM), kv_col(1)),
            pl.BlockSpec((t, HEAD_DIM), kv_col(2)),
            pl.BlockSpec((t, HEAD_DIM), kv_col(3)),
            pl.BlockSpec((3, None, hpg, QB), lambda g, i: (0, g, 0, i)),
        ],
        out_specs=pl.BlockSpec((QB, hpg * HEAD_DIM), lambda g, i: (i, g)),
        out_shape=jax.ShapeDtypeStruct((t, hd), BF16),
        scratch_shapes=[
            pltpu.VMEM((SUBLANES + ncp, QB), F32),
            pltpu.VMEM((nb, QB), F32),
            pltpu.VMEM((nb, QB), F32),
            pltpu.VMEM((HEAD_DIM, w), F32),
            pltpu.VMEM((HEAD_DIM, w), F32),
            pltpu.VMEM((HEAD_DIM, w), F32),
            pltpu.VMEM((hpg, WIN_KEYS, QB), F32),
            pltpu.VMEM((2, hpg, KT, QB), F32),
            pltpu.SMEM((t // KT,), jnp.int32),
        ],
        name="sparse_attention",
        compiler_params=_params("parallel", "arbitrary"),
    )(slopes, q, cmp_kv, cmp_kv, kv_rest, kv_rest, kv_rest, kv_rest, gate_logits_t)


def _inproj_conv_kernel(a_ref, wb_ref, wc_ref, wh_ref, cw_ref, o_ref, halo_ref):
    i = pl.program_id(0)
    j = pl.program_id(1)
    a = a_ref[...]
    b = jnp.dot(a, wb_ref[...], preferred_element_type=F32)
    u = (jnp.dot(a, wc_ref[...], preferred_element_type=F32)
         * jnp.dot(a, wh_ref[...], preferred_element_type=F32))
    wk = cw_ref[...]
    w0, w1, w2 = wk[0:1, :], wk[1:2, :], wk[2:3, :]
    tm = u.shape[0]
    y = b * (w0 * pltpu.roll(u, 2, 0) + w1 * pltpu.roll(u, 1, 0) + w2 * u)
    o_ref[...] = y.astype(o_ref.dtype)
    @pl.when(i == 0)
    def _():
        halo_ref[j] = jnp.zeros(halo_ref.shape[1:], F32)

    ue = jnp.concatenate([halo_ref[j], u[0:SUBLANES, :]], axis=0)
    n2 = 2 * SUBLANES
    u1 = pltpu.roll(ue, 1, 0)[SUBLANES:n2, :]
    u2 = pltpu.roll(ue, 2, 0)[SUBLANES:n2, :]
    y0 = b[0:SUBLANES, :] * (w0 * u2 + w1 * u1 + w2 * u[0:SUBLANES, :])
    o_ref[0:SUBLANES, :] = y0.astype(o_ref.dtype)
    halo_ref[j] = u[tm - SUBLANES:tm, :]


def inproj_short_conv(a, w, conv_w, *, col0, cw, tm=1024, tn=512):
    m, k = a.shape
    tm, tn = _tile(m, tm), _tile(cw, tn)
    while col0 % tn:
        tn = _tile(cw, tn - 1)
    ncb = cw // tn
    jb = col0 // tn
    wcol = lambda part: (lambda i, j: (0, jb + part * ncb + j))
    return pl.pallas_call(
        _inproj_conv_kernel,
        grid=(m // tm, ncb),
        in_specs=[
            pl.BlockSpec((tm, k), lambda i, j: (i, 0)),
            pl.BlockSpec((k, tn), wcol(0)),
            pl.BlockSpec((k, tn), wcol(1)),
            pl.BlockSpec((k, tn), wcol(2)),
            pl.BlockSpec((CONV_K, tn), lambda i, j: (0, j)),
        ],
        out_specs=pl.BlockSpec((tm, tn), lambda i, j: (i, j)),
        out_shape=jax.ShapeDtypeStruct((m, cw), BF16),
        scratch_shapes=[pltpu.VMEM((ncb, SUBLANES, tn), F32)],
        name="inproj_short_conv",
        compiler_params=_params("arbitrary", "arbitrary"),
    )(a, w, w, w, conv_w)


def _mem_kernel(x_ref, gpre_ref, gpost_ref, wq_ref, km_ref, vm_ref, wo_ref, o_ref):
    x = x_ref[...]
    xn = _rms(x, gpre_ref[...]).astype(BF16)
    qm = jnp.dot(xn, wq_ref[...], preferred_element_type=F32) * (MEM_HEAD_DIM ** -0.5)
    outs = []
    for h in range(MEM_HEADS):
        cols = slice(h * MEM_HEAD_DIM, (h + 1) * MEM_HEAD_DIM)
        s = lax.dot_general(qm[:, cols].astype(BF16), km_ref[:, cols], _NT, preferred_element_type=F32)
        p = jnp.exp(s - jnp.max(s, axis=-1, keepdims=True))
        p = p / jnp.sum(p, axis=-1, keepdims=True)
        outs.append(jnp.dot(p.astype(BF16), vm_ref[:, cols], preferred_element_type=F32).astype(BF16))
    y = jnp.dot(jnp.concatenate(outs, axis=1), wo_ref[...], preferred_element_type=F32)
    o_ref[...] = x + _rms(y, gpost_ref[...])


def memory_sublayer(x, gpre, gpost, wq, km, vm, wo, tm=256):
    m, d = x.shape
    mw = wq.shape[1]
    nm = km.shape[0]
    tm = _tile(m, tm)
    full = lambda i: (0, 0)
    return pl.pallas_call(
        _mem_kernel,
        grid=(m // tm,),
        in_specs=[
            pl.BlockSpec((tm, d), lambda i: (i, 0)),
            pl.BlockSpec((1, d), full),
            pl.BlockSpec((1, d), full),
            pl.BlockSpec((d, mw), full),
            pl.BlockSpec((nm, mw), full),
            pl.BlockSpec((nm, mw), full),
            pl.BlockSpec((mw, d), full),
        ],
        out_specs=pl.BlockSpec((tm, d), lambda i: (i, 0)),
        out_shape=jax.ShapeDtypeStruct((m, d), F32),
        name="memory_sublayer",
        compiler_params=_params("parallel"),
    )(x, gpre.reshape(1, d), gpost.reshape(1, d), wq, km, vm, wo)


def _mlp_kernel(x_ref, gpre_ref, gpost_ref, wu_ref, wd_ref, o_ref, xn_scr, *, nf):
    f = pl.program_id(1)

    @pl.when(f == 0)
    def _():
        xn_scr[...] = _rms(x_ref[...], gpre_ref[...]).astype(BF16)
        o_ref[...] = jnp.zeros_like(o_ref)

    hid = jnp.dot(xn_scr[...], wu_ref[...], preferred_element_type=F32)
    hid = jnp.square(jnp.maximum(hid, 0.0))
    o_ref[...] += jnp.dot(hid.astype(BF16), wd_ref[...], preferred_element_type=F32)

    @pl.when(f == nf - 1)
    def _():
        o_ref[...] = x_ref[...] + _rms(o_ref[...], gpost_ref[...])


def mlp_sublayer(x, gpre, gpost, w_up, w_down, tm=512, tf=512):
    m, d = x.shape
    dff = w_up.shape[1]
    tm, tf = _tile(m, tm), _tile(dff, tf)
    nf = dff // tf
    w_up = w_up.astype(BF16).reshape(d, nf, tf).transpose(1, 0, 2)
    w_down = w_down.astype(BF16)
    return pl.pallas_call(
        functools.partial(_mlp_kernel, nf=nf),
        grid=(m // tm, nf),
        in_specs=[
            pl.BlockSpec((tm, d), lambda i, f: (i, 0), pipeline_mode=pl.Buffered(1)),
            pl.BlockSpec((1, d), lambda i, f: (0, 0)),
            pl.BlockSpec((1, d), lambda i, f: (0, 0)),
            pl.BlockSpec((None, d, tf), lambda i, f: (f, 0, 0)),
            pl.BlockSpec((tf, d), lambda i, f: (f, 0)),
        ],
        out_specs=pl.BlockSpec((tm, d), lambda i, f: (i, 0)),
        out_shape=jax.ShapeDtypeStruct((m, d), F32),
        scratch_shapes=[pltpu.VMEM((tm, d), BF16)],
        name="mlp_sublayer",
        compiler_params=_params("parallel", "arbitrary"),
    )(x, gpre.reshape(1, d), gpost.reshape(1, d), w_up, w_down)


def _mixer(x, norm_pre, norm_post, w_in, cmp_k_pe, cmp_k_w1, cmp_k_w2, cmp_v_pe, cmp_v_w1, cmp_v_w2,
           conv_w, w_out):
    t, d = x.shape
    attn_w = d // 2
    conv_cw = d - attn_w
    n_heads = attn_w // HEAD_DIM
    n_groups = n_heads // 4
    kvw = n_groups * HEAD_DIM
    o_kv = attn_w
    o_gate = o_kv + 6 * kvw
    o_conv = o_gate + 3 * n_heads
    assert w_in.shape[1] == o_conv + 3 * conv_cw

    w_gate = jnp.pad(w_in[:, o_gate:o_conv], ((0, 0), (0, LANES - 3 * n_heads)))
    w_all = jnp.concatenate([w_in[:, :o_gate], w_in[:, o_conv:], w_gate], axis=1).astype(BF16)
    c_conv = o_gate
    c_gate = o_gate + 3 * conv_cw

    xn = rmsnorm_cast(x, norm_pre)
    q = matmul(xn, w_all, BF16, scale=HEAD_DIM ** -0.5 * LOG2E, col0=0, n=attn_w)
    kvc = matmul(xn, w_all, F32, col0=o_kv, n=2 * kvw)
    kv_rest = matmul(xn, w_all, BF16, col0=o_kv + 2 * kvw, n=4 * kvw)
    o_conv_out = inproj_short_conv(xn, w_all, conv_w, col0=c_conv, cw=conv_cw)
    gate_logits = matmul(xn, w_all, F32, col0=c_gate, n=LANES)[:, :3 * n_heads]

    nc = (t - CMP_BLOCK) // CMP_STRIDE + 1
    pe = jnp.stack([cmp_k_pe, cmp_v_pe])
    w1 = jnp.stack([cmp_k_w1, cmp_v_w1]).astype(BF16)
    w2 = jnp.stack([cmp_k_w2, cmp_v_w2]).astype(BF16)
    cmp_kv = compress_tokens(kvc, pe, w1, w2, n_groups)

    gl_t = gate_logits.T.reshape(3, n_groups, n_heads // n_groups, t)
    idx = jnp.arange(1, n_heads + 1, dtype=F32)
    slopes = jnp.exp2(-8.0 * idx / n_heads) * LOG2E
    o_attn = sparse_attention(slopes, q, cmp_kv, kv_rest, gl_t, nc=nc)
    return matmul_postnorm_residual(o_attn, o_conv_out, w_out.astype(BF16), x, norm_post)


def _memory(x, mem, norm_pre, norm_kv, norm_post, wq, wk, wv, wo):
    mn = rmsnorm_cast(mem, norm_kv)
    km = matmul(mn, wk.astype(BF16), BF16)
    vm = matmul(mn, wv.astype(BF16), BF16)
    return memory_sublayer(x, norm_pre, norm_post, wq.astype(BF16), km, vm, wo.astype(BF16))


def kernel(x, mem, mix_norm_pre, mix_norm_post, w_in, cmp_k_pe, cmp_k_w1, cmp_k_w2, cmp_v_pe, cmp_v_w1,
           cmp_v_w2, conv_w, w_out, mem_norm_pre, mem_norm_kv, mem_norm_post, w_mem_q, w_mem_k, w_mem_v,
           w_mem_o, mlp_norm_pre, mlp_norm_post, w_up, w_down):
    b, t, d = x.shape
    assert b == 1
    h = x[0]
    m = mem[0]
    for l in range(w_in.shape[0]):
        h = _mixer(h, mix_norm_pre[l], mix_norm_post[l], w_in[l], cmp_k_pe[l], cmp_k_w1[l], cmp_k_w2[l],
                   cmp_v_pe[l], cmp_v_w1[l], cmp_v_w2[l], conv_w[l], w_out[l])
        h = _memory(h, m, mem_norm_pre[l], mem_norm_kv[l], mem_norm_post[l], w_mem_q[l], w_mem_k[l],
                    w_mem_v[l], w_mem_o[l])
        h = mlp_sublayer(h, mlp_norm_pre[l], mlp_norm_post[l], w_up[l], w_down[l])
    return h[None]
```

```python
import functools

import jax
import jax.numpy as jnp
from jax import lax
from jax.experimental import pallas as pl
from jax.experimental.pallas import tpu as pltpu

HEAD_DIM = 128
CMP_BLOCK = 32
CMP_STRIDE = 16
SLC_BLOCK = 64
N_SELECT = 16
N_LOCAL_FORCED = 2
WINDOW = 512
CONV_K = 3
MEM_HEADS = 4
MEM_HEAD_DIM = 128
RMS_EPS = 1e-6
NEG_INF = -1e30
FORCED_SCORE = 1e9
LOG2E = 1.4426950408889634

V7X_VMEM_BYTES = 64 * 1024 * 1024
VMEM_LIMIT = V7X_VMEM_BYTES - 8 * 1024 * 1024
LANES = 128
SUBLANES = 8

QB = 128
KB = 128
KT = 512
WIN_KEYS = WINDOW + QB

F32 = jnp.float32
BF16 = jnp.bfloat16

_NT = (((1,), (1,)), ((), ()))
_TN = (((0,), (0,)), ((), ()))


def _tile(n, pref):
    t = min(n, pref)
    while n % t:
        t -= 1
    return t


def _params(*sem):
    return pltpu.CompilerParams(dimension_semantics=sem, vmem_limit_bytes=VMEM_LIMIT)


def _rms(x, gain):
    return x * lax.rsqrt(jnp.mean(x * x, axis=-1, keepdims=True) + RMS_EPS) * gain


def _rmsnorm_kernel(x_ref, g_ref, o_ref):
    o_ref[...] = _rms(x_ref[...].astype(F32), g_ref[...]).astype(o_ref.dtype)


def rmsnorm_cast(x, gain, out_dtype=BF16):
    m, d = x.shape
    tm = _tile(m, 256)
    return pl.pallas_call(
        _rmsnorm_kernel,
        grid=(m // tm,),
        in_specs=[pl.BlockSpec((tm, d), lambda i: (i, 0)), pl.BlockSpec((1, d), lambda i: (0, 0))],
        out_specs=pl.BlockSpec((tm, d), lambda i: (i, 0)),
        out_shape=jax.ShapeDtypeStruct((m, d), out_dtype),
        name="rmsnorm_cast",
        compiler_params=_params("parallel"),
    )(x, gain.reshape(1, d))


def _mm_kernel(a_ref, w_ref, o_ref, *, scale):
    acc = jnp.dot(a_ref[...], w_ref[...], preferred_element_type=F32)
    if scale is not None:
        acc = acc * scale
    o_ref[...] = acc.astype(o_ref.dtype)


def matmul(a, w, out_dtype, scale=None, col0=0, n=None, tm=1024, tn=1024):
    m, k = a.shape
    n = w.shape[1] - col0 if n is None else n
    tm, tn = _tile(m, tm), _tile(n, tn)
    while col0 % tn:
        tn = _tile(n, tn - 1)
    jb = col0 // tn
    return pl.pallas_call(
        functools.partial(_mm_kernel, scale=scale),
        grid=(m // tm, n // tn),
        in_specs=[pl.BlockSpec((tm, k), lambda i, j: (i, 0)), pl.BlockSpec((k, tn), lambda i, j: (0, j + jb))],
        out_specs=pl.BlockSpec((tm, tn), lambda i, j: (i, j)),
        out_shape=jax.ShapeDtypeStruct((m, n), out_dtype),
        name="matmul",
        compiler_params=_params("parallel", "arbitrary"),
    )(a, w)


def _norm_mm_kernel(x_ref, g_ref, w_ref, xn_ref, o_ref, *, scale):
    @pl.when(pl.program_id(1) == 0)
    def _():
        xn_ref[...] = _rms(x_ref[...], g_ref[...]).astype(xn_ref.dtype)

    acc = jnp.dot(xn_ref[...], w_ref[...], preferred_element_type=F32)
    o_ref[...] = (acc * scale).astype(o_ref.dtype)


def norm_matmul(x, gain, w, out_dtype, scale, col0, n, tm=512, tn=1024):
    m, k = x.shape
    tm, tn = _tile(m, tm), _tile(n, tn)
    while col0 % tn:
        tn = _tile(n, tn - 1)
    jb = col0 // tn
    return pl.pallas_call(
        functools.partial(_norm_mm_kernel, scale=scale),
        grid=(m // tm, n // tn),
        in_specs=[
            pl.BlockSpec((tm, k), lambda i, j: (i, 0)),
            pl.BlockSpec((1, k), lambda i, j: (0, 0)),
            pl.BlockSpec((k, tn), lambda i, j: (0, j + jb)),
        ],
        out_specs=[pl.BlockSpec((tm, k), lambda i, j: (i, 0)), pl.BlockSpec((tm, tn), lambda i, j: (i, j))],
        out_shape=[jax.ShapeDtypeStruct((m, k), BF16), jax.ShapeDtypeStruct((m, n), out_dtype)],
        name="norm_matmul",
        compiler_params=_params("parallel", "arbitrary"),
    )(x, gain.reshape(1, k), w)


def _mm_postnorm_kernel(a1_ref, a2_ref, w_ref, x_ref, g_ref, o_ref, *, nk1, nk):
    k = pl.program_id(1)

    @pl.when(k == 0)
    def _():
        o_ref[...] = jnp.zeros_like(o_ref)

    @pl.when(k < nk1)
    def _():
        o_ref[...] += jnp.dot(a1_ref[...], w_ref[...], preferred_element_type=F32)

    @pl.when(k >= nk1)
    def _():
        o_ref[...] += jnp.dot(a2_ref[...], w_ref[...], preferred_element_type=F32)

    @pl.when(k == nk - 1)
    def _():
        o_ref[...] = x_ref[...] + _rms(o_ref[...], g_ref[...])


def matmul_postnorm_residual(a1, a2, w, x, gain, tm=512, tk=512):
    m, k1 = a1.shape
    k2 = a2.shape[1]
    _, n = w.shape
    tm = _tile(m, tm)
    tk = _tile(k1, tk)
    while k2 % tk:
        tk = _tile(k1, tk - 1)
    nk1, nk = k1 // tk, (k1 + k2) // tk
    return pl.pallas_call(
        functools.partial(_mm_postnorm_kernel, nk1=nk1, nk=nk),
        grid=(m // tm, nk),
        in_specs=[
            pl.BlockSpec((tm, tk), lambda i, k: (i, jnp.minimum(k, nk1 - 1))),
            pl.BlockSpec((tm, tk), lambda i, k: (i, jnp.maximum(k - nk1, 0))),
            pl.BlockSpec((tk, n), lambda i, k: (k, 0)),
            pl.BlockSpec((tm, n), lambda i, k: (i, 0)),
            pl.BlockSpec((1, n), lambda i, k: (0, 0)),
        ],
        out_specs=pl.BlockSpec((tm, n), lambda i, k: (i, 0)),
        out_shape=jax.ShapeDtypeStruct((m, n), F32),
        name="outproj_postnorm",
        compiler_params=_params("parallel", "arbitrary"),
    )(a1, a2, w, x, gain.reshape(1, n))


def _gelu_tanh(x):
    c = 0.7978845608028654
    return x * (0.5 * (1.0 + jnp.tanh(c * (x + 0.044715 * (x * x * x)))))


def _compress_kernel(x_ref, pe_ref, w1_ref, w2_ref, o_ref, *, ncp):
    a = jnp.zeros((ncp, HEAD_DIM), F32)
    b = jnp.zeros((ncp, HEAD_DIM), F32)
    for l in range(CMP_STRIDE):
        xl = x_ref[pl.ds(l, ncp, stride=CMP_STRIDE), :]
        lo, hi = l, CMP_STRIDE + l
        a = a + jnp.dot((xl + pe_ref[lo:lo + 1, :]).astype(BF16), w1_ref[lo * HEAD_DIM:(lo + 1) * HEAD_DIM, :],
                        preferred_element_type=F32)
        b = b + jnp.dot((xl + pe_ref[hi:hi + 1, :]).astype(BF16), w1_ref[hi * HEAD_DIM:(hi + 1) * HEAD_DIM, :],
                        preferred_element_type=F32)
    hid = _gelu_tanh(a + pltpu.roll(b, ncp - 1, 0))
    o_ref[...] = jnp.dot(hid.astype(BF16), w2_ref[...], preferred_element_type=F32).astype(o_ref.dtype)


def compress_tokens(kvc, pe, w1, w2, n_groups):
    t = kvc.shape[0]
    ncp = t // CMP_STRIDE
    assert CMP_BLOCK == 2 * CMP_STRIDE
    return pl.pallas_call(
        functools.partial(_compress_kernel, ncp=ncp),
        grid=(2, n_groups),
        in_specs=[
            pl.BlockSpec((t, HEAD_DIM), lambda s, j: (0, s * n_groups + j)),
            pl.BlockSpec((None, CMP_BLOCK, HEAD_DIM), lambda s, j: (s, 0, 0)),
            pl.BlockSpec((None, CMP_BLOCK * HEAD_DIM, HEAD_DIM), lambda s, j: (s, 0, 0)),
            pl.BlockSpec((None, HEAD_DIM, HEAD_DIM), lambda s, j: (s, 0, 0)),
        ],
        out_specs=pl.BlockSpec((None, None, ncp, HEAD_DIM), lambda s, j: (s, j, 0, 0)),
        out_shape=jax.ShapeDtypeStruct((2, n_groups, ncp, HEAD_DIM), BF16),
        name="compress_tokens",
        compiler_params=_params("parallel", "parallel"),
    )(kvc, pe, w1, w2)


def _attn_kernel(slopes_ref, q_ref, kcmp_ref, vcmp_ref, ks_ref, vs_ref, kw_ref, vw_ref, gl_ref,
                 o_ref, p_scr, imp_scr, sel_scr, ocmp_scr, acc_slc, acc_win, alibi_scr, u_scr, flag_ref,
                 *, hpg, nc, ncp, nb, n_sel):
    g = pl.program_id(0)
    i = pl.program_id(1)
    t0 = i * QB
    w = hpg * QB
    ninf = -jnp.inf

    q = q_ref[...]
    qs = jnp.concatenate([q[:, h * HEAD_DIM:(h + 1) * HEAD_DIM] for h in range(hpg)], axis=0)
    slopes = [slopes_ref[g * hpg + h] for h in range(hpg)]

    s = lax.dot_general(kcmp_ref[...], qs, _NT, preferred_element_type=F32)
    c_iota = lax.broadcasted_iota(jnp.int32, (ncp, QB), 0)
    q_iota = lax.broadcasted_iota(jnp.int32, (ncp, QB), 1)
    cmp_end = c_iota * CMP_STRIDE + (CMP_BLOCK - 1)
    mask_c = (cmp_end <= t0 + q_iota) & (c_iota < nc)
    rel_c = (cmp_end - t0).astype(F32)
    psum = jnp.zeros((ncp, QB), F32)
    for h in range(hpg):
        sh = s[:, h * QB:(h + 1) * QB] + slopes[h] * rel_c
        sh = jnp.where(mask_c, sh, ninf)
        m = jnp.maximum(jnp.max(sh, axis=0, keepdims=True), NEG_INF)
        p = jnp.exp2(sh - m)
        l = jnp.sum(p, axis=0, keepdims=True)
        pn = p * jnp.where(l > 0.0, 1.0 / l, 0.0)
        psum = psum + pn
        ocmp_scr[:, h * QB:(h + 1) * QB] = lax.dot_general(vcmp_ref[...], pn.astype(BF16), _TN,
                                                            preferred_element_type=F32)

    p_scr[0:SUBLANES, :] = jnp.zeros((SUBLANES, QB), F32)
    p_scr[SUBLANES:SUBLANES + ncp, :] = psum
    imp = p_scr[pl.ds(SUBLANES - 1, nb, stride=4), :]
    for j in range(4):
        imp = imp + p_scr[pl.ds(SUBLANES + j, nb, stride=4), :]
    n_iota = lax.broadcasted_iota(jnp.int32, (nb, QB), 0)
    tq = t0 + lax.broadcasted_iota(jnp.int32, (nb, QB), 1)
    cur = lax.shift_right_logical(tq, 6)
    valid = n_iota <= cur
    back = cur - n_iota
    forced = (n_iota == 0) | ((back >= 0) & (back < N_LOCAL_FORCED))
    imp = jnp.where(forced & valid, FORCED_SCORE, imp)
    imp = jnp.where(valid, imp, NEG_INF)
    imp_scr[...] = imp

    n_grp = nb // SUBLANES
    vals = [imp[j * SUBLANES:(j + 1) * SUBLANES, :] for j in range(n_grp)]
    row8 = lax.broadcasted_iota(jnp.int32, (SUBLANES, QB), 0)

    def rank_chunk(mc, cnts):
        cnts = list(cnts)
        for mi in range(mc * SUBLANES, (mc + 1) * SUBLANES):
            vm = jnp.broadcast_to(imp_scr[pl.ds(mi, 1), :], (SUBLANES, QB))
            for j in range(n_grp):
                lo = j * SUBLANES
                if lo + SUBLANES - 1 < mi:
                    beats = vm > vals[j]
                elif lo > mi:
                    beats = vm >= vals[j]
                else:
                    beats = (vm > vals[j]) | ((vm == vals[j]) & (row8 > mi - lo))
                cnts[j] = cnts[j] + jnp.where(beats, 1.0, 0.0)
        return tuple(cnts)

    last_valid = (t0 + QB - 1) // SLC_BLOCK
    cnts = tuple(jnp.zeros((SUBLANES, QB), F32) for _ in range(n_grp))
    for mc in range(n_grp):
        cnts = lax.cond(mc * SUBLANES <= last_valid, functools.partial(rank_chunk, mc), lambda c: c, cnts)
    cnt = jnp.concatenate(cnts, axis=0)
    sel = (cnt < float(n_sel)) & valid
    sel_scr[...] = jnp.where(sel, 0.0, ninf)
    blocks_per_kt = KT // SLC_BLOCK
    sel_f = jnp.where(sel, 1.0, 0.0)
    for kt in range(nb // blocks_per_kt):
        any_sel = jnp.max(sel_f[kt * blocks_per_kt:(kt + 1) * blocks_per_kt, :])
        flag_ref[kt] = (any_sel > 0.5).astype(jnp.int32)

    rc_i = (lax.broadcasted_iota(jnp.int32, (WIN_KEYS, QB), 1)
            - lax.broadcasted_iota(jnp.int32, (WIN_KEYS, QB), 0))
    @pl.when(i == 0)
    def _():
        rc_f = (-rc_i).astype(F32)
        for h in range(hpg):
            alibi_scr[h] = slopes[h] * rc_f

    init = (tuple(jnp.full((1, QB), NEG_INF, F32) for _ in range(hpg)),
            tuple(jnp.zeros((1, QB), F32) for _ in range(hpg)))

    def stage_a(kt, slot):
        k0 = pl.multiple_of(kt * KT, KT)
        st = lax.dot_general(ks_ref[pl.ds(k0, KT), :], qs, _NT, preferred_element_type=F32)
        rows = [jnp.broadcast_to(sel_scr[pl.ds(kt * blocks_per_kt + b, 1), :], (SLC_BLOCK, QB))
                for b in range(blocks_per_kt)]
        diff = (t0 - k0) + rc_i[0:KT, :]
        base = jnp.where(diff >= 0, jnp.concatenate(rows, axis=0), ninf)
        off = (k0 - t0).astype(F32)
        tile_max = []
        for h in range(hpg):
            u = st[:, h * QB:(h + 1) * QB] + (alibi_scr[h, 0:KT, :] + base)
            u_scr[slot, h] = u
            tile_max.append(jnp.max(u, axis=0, keepdims=True) + slopes[h] * off)
        return tuple(tile_max)

    def stage_b(kt, slot, tile_max, ms, ls):
        k0 = pl.multiple_of(kt * KT, KT)
        off = (k0 - t0).astype(F32)
        new_m, new_l, alphas, ps = [], [], [], []
        for h in range(hpg):
            mh = jnp.maximum(ms[h], tile_max[h])
            alpha = jnp.exp2(ms[h] - mh)
            p = jnp.exp2(u_scr[slot, h] - (mh - slopes[h] * off))
            new_m.append(mh)
            new_l.append(alpha * ls[h] + jnp.sum(p, axis=0, keepdims=True))
            alphas.append(alpha)
            ps.append(p.astype(BF16))
        pv = lax.dot_general(vs_ref[pl.ds(k0, KT), :], jnp.concatenate(ps, axis=1), _TN,
                             preferred_element_type=F32)
        acc_slc[...] = acc_slc[...] * jnp.concatenate(alphas, axis=1) + pv
        return tuple(new_m), tuple(new_l)

    def slc_step(kt, carry):
        def visit(c):
            ms, ls, tile_max, prev, slot = c
            ms, ls = stage_b(prev, slot, tile_max, ms, ls)
            return ms, ls, stage_a(kt, 1 - slot), kt, 1 - slot
        return lax.cond(flag_ref[kt] > 0, visit, lambda c: c, carry)

    kd = t0 // KT
    acc_slc[...] = jnp.zeros_like(acc_slc)
    zero = jnp.int32(0)
    ms, ls, tile_max, prev, slot = lax.fori_loop(1, kd + 1, slc_step, init + (stage_a(zero, zero), zero, zero))
    _, l_slc = stage_b(prev, slot, tile_max, ms, ls)

    w0 = pl.multiple_of(jnp.maximum(t0 - WINDOW, 0), KB)
    diff_w = (t0 - w0) + rc_i
    base_w = jnp.where((diff_w >= 0) & (diff_w < WINDOW), 0.0, ninf)
    st_w = lax.dot_general(kw_ref[pl.ds(w0, WIN_KEYS), :], qs, _NT, preferred_element_type=F32)
    l_win, ps = [], []
    for h in range(hpg):
        u = st_w[:, h * QB:(h + 1) * QB] + (alibi_scr[h] + base_w)
        p = jnp.exp2(u - jnp.max(u, axis=0, keepdims=True))
        l_win.append(jnp.sum(p, axis=0, keepdims=True))
        ps.append(p.astype(BF16))
    acc_win[...] = lax.dot_general(vw_ref[pl.ds(w0, WIN_KEYS), :], jnp.concatenate(ps, axis=1), _TN,
                                   preferred_element_type=F32)

    for h in range(hpg):
        cols = slice(h * QB, (h + 1) * QB)
        g_cmp = jax.nn.sigmoid(gl_ref[0, h:h + 1, :])
        g_slc = jax.nn.sigmoid(gl_ref[1, h:h + 1, :])
        g_win = jax.nn.sigmoid(gl_ref[2, h:h + 1, :])
        oT = (g_cmp * ocmp_scr[:, cols] + (g_slc / l_slc[h]) * acc_slc[:, cols]
              + (g_win / l_win[h]) * acc_win[:, cols])
        o_ref[:, h * HEAD_DIM:(h + 1) * HEAD_DIM] = oT.T.astype(o_ref.dtype)


def sparse_attention(slopes, q, cmp_kv, kv_rest, gate_logits_t, *, nc):
    t, hd = q.shape
    _, n_groups, ncp, _ = cmp_kv.shape
    hpg = hd // HEAD_DIM // n_groups
    nb = t // SLC_BLOCK
    assert t % KT == 0 and t >= WIN_KEYS and ncp == 4 * nb and nb % SUBLANES == 0
    assert (CMP_BLOCK, CMP_STRIDE, SLC_BLOCK) == (32, 16, 64)
    n_sel = min(N_SELECT, nb)
    w = hpg * QB
    kern = functools.partial(_attn_kernel, hpg=hpg, nc=nc, ncp=ncp, nb=nb, n_sel=n_sel)
    kv_col = lambda part: (lambda g, i: (0, part * n_groups + g))
    return pl.pallas_call(
        kern,
        grid=(n_groups, t // QB),
        in_specs=[
            pl.BlockSpec(memory_space=pltpu.SMEM),
            pl.BlockSpec((QB, hpg * HEAD_DIM), lambda g, i: (i, g)),
            pl.BlockSpec((None, None, ncp, HEAD_DIM), lambda g, i: (0, g, 0, 0)),
            pl.BlockSpec((None, None, ncp, HEAD_DIM), lambda g, i: (1, g, 0, 0)),
            pl.BlockSpec((t, HEAD_DIM), kv_col(0)),
            pl.BlockSpec((t, HEAD_DIM), kv_col(1)),
            pl.BlockSpec((t, HEAD_DIM), kv_col(2)),
            pl.BlockSpec((t, HEAD_DIM), kv_col(3)),
            pl.BlockSpec((3, None, hpg, QB), lambda g, i: (0, g, 0, i)),
        ],
        out_specs=pl.BlockSpec((QB, hpg * HEAD_DIM), lambda g, i: (i, g)),
        out_shape=jax.ShapeDtypeStruct((t, hd), BF16),
        scratch_shapes=[
            pltpu.VMEM((SUBLANES + ncp, QB), F32),
            pltpu.VMEM((nb, QB), F32),
            pltpu.VMEM((nb, QB), F32),
            pltpu.VMEM((HEAD_DIM, w), F32),
            pltpu.VMEM((HEAD_DIM, w), F32),
            pltpu.VMEM((HEAD_DIM, w), F32),
            pltpu.VMEM((hpg, WIN_KEYS, QB), F32),
            pltpu.VMEM((2, hpg, KT, QB), F32),
            pltpu.SMEM((t // KT,), jnp.int32),
        ],
        name="sparse_attention",
        compiler_params=_params("parallel", "arbitrary"),
    )(slopes, q, cmp_kv, cmp_kv, kv_rest, kv_rest, kv_rest, kv_rest, gate_logits_t)


def _inproj_conv_kernel(a_ref, wb_ref, wc_ref, wh_ref, cw_ref, o_ref, halo_ref):
    i = pl.program_id(0)
    j = pl.program_id(1)
    a = a_ref[...]
    b = jnp.dot(a, wb_ref[...], preferred_element_type=F32)
    u = (jnp.dot(a, wc_ref[...], preferred_element_type=F32)
         * jnp.dot(a, wh_ref[...], preferred_element_type=F32))
    wk = cw_ref[...]
    w0, w1, w2 = wk[0:1, :], wk[1:2, :], wk[2:3, :]
    tm = u.shape[0]
    y = b * (w0 * pltpu.roll(u, 2, 0) + w1 * pltpu.roll(u, 1, 0) + w2 * u)
    o_ref[...] = y.astype(o_ref.dtype)
    @pl.when(i == 0)
    def _():
        halo_ref[j] = jnp.zeros(halo_ref.shape[1:], F32)

    ue = jnp.concatenate([halo_ref[j], u[0:SUBLANES, :]], axis=0)
    n2 = 2 * SUBLANES
    u1 = pltpu.roll(ue, 1, 0)[SUBLANES:n2, :]
    u2 = pltpu.roll(ue, 2, 0)[SUBLANES:n2, :]
    y0 = b[0:SUBLANES, :] * (w0 * u2 + w1 * u1 + w2 * u[0:SUBLANES, :])
    o_ref[0:SUBLANES, :] = y0.astype(o_ref.dtype)
    halo_ref[j] = u[tm - SUBLANES:tm, :]


def inproj_short_conv(a, w, conv_w, *, col0, cw, tm=1024, tn=512):
    m, k = a.shape
    tm, tn = _tile(m, tm), _tile(cw, tn)
    while col0 % tn:
        tn = _tile(cw, tn - 1)
    ncb = cw // tn
    jb = col0 // tn
    wcol = lambda part: (lambda i, j: (0, jb + part * ncb + j))
    return pl.pallas_call(
        _inproj_conv_kernel,
        grid=(m // tm, ncb),
        in_specs=[
            pl.BlockSpec((tm, k), lambda i, j: (i, 0)),
            pl.BlockSpec((k, tn), wcol(0)),
            pl.BlockSpec((k, tn), wcol(1)),
            pl.BlockSpec((k, tn), wcol(2)),
            pl.BlockSpec((CONV_K, tn), lambda i, j: (0, j)),
        ],
        out_specs=pl.BlockSpec((tm, tn), lambda i, j: (i, j)),
        out_shape=jax.ShapeDtypeStruct((m, cw), BF16),
        scratch_shapes=[pltpu.VMEM((ncb, SUBLANES, tn), F32)],
        name="inproj_short_conv",
        compiler_params=_params("arbitrary", "arbitrary"),
    )(a, w, w, w, conv_w)


def _mem_kernel(x_ref, gpre_ref, gpost_ref, wq_ref, km_ref, vm_ref, wo_ref, o_ref):
    x = x_ref[...]
    xn = _rms(x, gpre_ref[...]).astype(BF16)
    qm = jnp.dot(xn, wq_ref[...], preferred_element_type=F32) * (MEM_HEAD_DIM ** -0.5)
    outs = []
    for h in range(MEM_HEADS):
        cols = slice(h * MEM_HEAD_DIM, (h + 1) * MEM_HEAD_DIM)
        s = lax.dot_general(qm[:, cols].astype(BF16), km_ref[:, cols], _NT, preferred_element_type=F32)
        p = jnp.exp(s - jnp.max(s, axis=-1, keepdims=True))
        p = p / jnp.sum(p, axis=-1, keepdims=True)
        outs.append(jnp.dot(p.astype(BF16), vm_ref[:, cols], preferred_element_type=F32).astype(BF16))
    y = jnp.dot(jnp.concatenate(outs, axis=1), wo_ref[...], preferred_element_type=F32)
    o_ref[...] = x + _rms(y, gpost_ref[...])


def memory_sublayer(x, gpre, gpost, wq, km, vm, wo, tm=512):
    m, d = x.shape
    mw = wq.shape[1]
    nm = km.shape[0]
    tm = _tile(m, tm)
    full = lambda i: (0, 0)
    return pl.pallas_call(
        _mem_kernel,
        grid=(m // tm,),
        in_specs=[
            pl.BlockSpec((tm, d), lambda i: (i, 0)),
            pl.BlockSpec((1, d), full),
            pl.BlockSpec((1, d), full),
            pl.BlockSpec((d, mw), full),
            pl.BlockSpec((nm, mw), full),
            pl.BlockSpec((nm, mw), full),
            pl.BlockSpec((mw, d), full),
        ],
        out_specs=pl.BlockSpec((tm, d), lambda i: (i, 0)),
        out_shape=jax.ShapeDtypeStruct((m, d), F32),
        name="memory_sublayer",
        compiler_params=_params("parallel"),
    )(x, gpre.reshape(1, d), gpost.reshape(1, d), wq, km, vm, wo)


def _mlp_kernel(x_ref, gpre_ref, gpost_ref, wu_ref, wd_ref, o_ref, xn_scr, *, nf):
    f = pl.program_id(1)

    @pl.when(f == 0)
    def _():
        xn_scr[...] = _rms(x_ref[...], gpre_ref[...]).astype(BF16)
        o_ref[...] = jnp.zeros_like(o_ref)

    hid = jnp.dot(xn_scr[...], wu_ref[...], preferred_element_type=F32)
    hid = jnp.square(jnp.maximum(hid, 0.0))
    o_ref[...] += jnp.dot(hid.astype(BF16), wd_ref[...], preferred_element_type=F32)

    @pl.when(f == nf - 1)
    def _():
        o_ref[...] = x_ref[...] + _rms(o_ref[...], gpost_ref[...])


def mlp_sublayer(x, gpre, gpost, w_up, w_down, tm=512, tf=512):
    m, d = x.shape
    dff = w_up.shape[1]
    tm, tf = _tile(m, tm), _tile(dff, tf)
    nf = dff // tf
    w_up = w_up.astype(BF16)
    w_down = w_down.astype(BF16)
    return pl.pallas_call(
        functools.partial(_mlp_kernel, nf=nf),
        grid=(m // tm, nf),
        in_specs=[
            pl.BlockSpec((tm, d), lambda i, f: (i, 0), pipeline_mode=pl.Buffered(1)),
            pl.BlockSpec((1, d), lambda i, f: (0, 0)),
            pl.BlockSpec((1, d), lambda i, f: (0, 0)),
            pl.BlockSpec((d, tf), lambda i, f: (0, f)),
            pl.BlockSpec((tf, d), lambda i, f: (f, 0)),
        ],
        out_specs=pl.BlockSpec((tm, d), lambda i, f: (i, 0)),
        out_shape=jax.ShapeDtypeStruct((m, d), F32),
        scratch_shapes=[pltpu.VMEM((tm, d), BF16)],
        name="mlp_sublayer",
        compiler_params=_params("parallel", "arbitrary"),
    )(x, gpre.reshape(1, d), gpost.reshape(1, d), w_up, w_down)


def _mixer(x, norm_pre, norm_post, w_in, cmp_k_pe, cmp_k_w1, cmp_k_w2, cmp_v_pe, cmp_v_w1, cmp_v_w2,
           conv_w, w_out):
    t, d = x.shape
    attn_w = d // 2
    conv_cw = d - attn_w
    n_heads = attn_w // HEAD_DIM
    n_groups = n_heads // 4
    kvw = n_groups * HEAD_DIM
    o_kv = attn_w
    o_gate = o_kv + 6 * kvw
    o_conv = o_gate + 3 * n_heads
    assert w_in.shape[1] == o_conv + 3 * conv_cw

    assert o_gate % LANES == 0 and o_gate + LANES <= w_in.shape[1]
    w_bf = w_in.astype(BF16)
    w_conv = w_in[:, o_conv:].astype(BF16)

    xn, q = norm_matmul(x, norm_pre, w_bf, BF16, scale=HEAD_DIM ** -0.5 * LOG2E, col0=0, n=attn_w)
    kvc = matmul(xn, w_bf, F32, col0=o_kv, n=2 * kvw)
    kv_rest = matmul(xn, w_bf, BF16, col0=o_kv + 2 * kvw, n=4 * kvw)
    o_conv_out = inproj_short_conv(xn, w_conv, conv_w, col0=0, cw=conv_cw)
    gate_logits = matmul(xn, w_bf, F32, col0=o_gate, n=LANES)[:, :3 * n_heads]

    nc = (t - CMP_BLOCK) // CMP_STRIDE + 1
    pe = jnp.stack([cmp_k_pe, cmp_v_pe])
    w1 = jnp.stack([cmp_k_w1, cmp_v_w1]).astype(BF16)
    w2 = jnp.stack([cmp_k_w2, cmp_v_w2]).astype(BF16)
    cmp_kv = compress_tokens(kvc, pe, w1, w2, n_groups)

    gl_t = gate_logits.T.reshape(3, n_groups, n_heads // n_groups, t)
    idx = jnp.arange(1, n_heads + 1, dtype=F32)
    slopes = jnp.exp2(-8.0 * idx / n_heads) * LOG2E
    o_attn = sparse_attention(slopes, q, cmp_kv, kv_rest, gl_t, nc=nc)
    return matmul_postnorm_residual(o_attn, o_conv_out, w_out.astype(BF16), x, norm_post)


def _memory(x, mem, norm_pre, norm_kv, norm_post, wq, wk, wv, wo):
    mn = rmsnorm_cast(mem, norm_kv)
    km = matmul(mn, wk.astype(BF16), BF16)
    vm = matmul(mn, wv.astype(BF16), BF16)
    return memory_sublayer(x, norm_pre, norm_post, wq.astype(BF16), km, vm, wo.astype(BF16))


def kernel(x, mem, mix_norm_pre, mix_norm_post, w_in, cmp_k_pe, cmp_k_w1, cmp_k_w2, cmp_v_pe, cmp_v_w1,
           cmp_v_w2, conv_w, w_out, mem_norm_pre, mem_norm_kv, mem_norm_post, w_mem_q, w_mem_k, w_mem_v,
           w_mem_o, mlp_norm_pre, mlp_norm_post, w_up, w_down):
    b, t, d = x.shape
    assert b == 1
    h = x[0]
    m = mem[0]
    for l in range(w_in.shape[0]):
        h = _mixer(h, mix_norm_pre[l], mix_norm_post[l], w_in[l], cmp_k_pe[l], cmp_k_w1[l], cmp_k_w2[l],
                   cmp_v_pe[l], cmp_v_w1[l], cmp_v_w2[l], conv_w[l], w_out[l])
        h = _memory(h, m, mem_norm_pre[l], mem_norm_kv[l], mem_norm_post[l], w_mem_q[l], w_mem_k[l],
                    w_mem_v[l], w_mem_o[l])
        h = mlp_sublayer(h, mlp_norm_pre[l], mlp_norm_post[l], w_up[l], w_down[l])
    return h[None]
```

```python
import functools

import jax
import jax.numpy as jnp
from jax import lax
from jax.experimental import pallas as pl
from jax.experimental.pallas import tpu as pltpu

HEAD_DIM = 128
CMP_BLOCK = 32
CMP_STRIDE = 16
SLC_BLOCK = 64
N_SELECT = 16
N_LOCAL_FORCED = 2
WINDOW = 512
CONV_K = 3
MEM_HEADS = 4
MEM_HEAD_DIM = 128
RMS_EPS = 1e-6
NEG_INF = -1e30
FORCED_SCORE = 1e9
LOG2E = 1.4426950408889634

V7X_VMEM_BYTES = 64 * 1024 * 1024
VMEM_LIMIT = V7X_VMEM_BYTES - 8 * 1024 * 1024
MLP_VMEM_LIMIT = V7X_VMEM_BYTES - 6 * 1024 * 1024
NORM_ROWS = 64
LANES = 128
SUBLANES = 8

QB = 128
KB = 128
KT = 512
WIN_KEYS = WINDOW + QB

F32 = jnp.float32
BF16 = jnp.bfloat16

_NT = (((1,), (1,)), ((), ()))
_TN = (((0,), (0,)), ((), ()))


def _tile(n, pref):
    t = min(n, pref)
    while n % t:
        t -= 1
    return t


def _params(*sem):
    return pltpu.CompilerParams(dimension_semantics=sem, vmem_limit_bytes=VMEM_LIMIT)


def _rms(x, gain):
    return x * lax.rsqrt(jnp.mean(x * x, axis=-1, keepdims=True) + RMS_EPS) * gain


def _rmsnorm_kernel(x_ref, g_ref, o_ref):
    o_ref[...] = _rms(x_ref[...].astype(F32), g_ref[...]).astype(o_ref.dtype)


def rmsnorm_cast(x, gain, out_dtype=BF16):
    m, d = x.shape
    tm = _tile(m, 256)
    return pl.pallas_call(
        _rmsnorm_kernel,
        grid=(m // tm,),
        in_specs=[pl.BlockSpec((tm, d), lambda i: (i, 0)), pl.BlockSpec((1, d), lambda i: (0, 0))],
        out_specs=pl.BlockSpec((tm, d), lambda i: (i, 0)),
        out_shape=jax.ShapeDtypeStruct((m, d), out_dtype),
        name="rmsnorm_cast",
        compiler_params=_params("parallel"),
    )(x, gain.reshape(1, d))


def _mm_kernel(a_ref, w_ref, o_ref, *, scale):
    acc = jnp.dot(a_ref[...], w_ref[...].astype(BF16), preferred_element_type=F32)
    if scale is not None:
        acc = acc * scale
    o_ref[...] = acc.astype(o_ref.dtype)


def matmul(a, w, out_dtype, scale=None, col0=0, n=None, tm=1024, tn=512):
    m, k = a.shape
    n = w.shape[1] - col0 if n is None else n
    tm, tn = _tile(m, tm), _tile(n, tn)
    while col0 % tn:
        tn = _tile(n, tn - 1)
    jb = col0 // tn
    return pl.pallas_call(
        functools.partial(_mm_kernel, scale=scale),
        grid=(m // tm, n // tn),
        in_specs=[pl.BlockSpec((tm, k), lambda i, j: (i, 0)), pl.BlockSpec((k, tn), lambda i, j: (0, j + jb))],
        out_specs=pl.BlockSpec((tm, tn), lambda i, j: (i, j)),
        out_shape=jax.ShapeDtypeStruct((m, n), out_dtype),
        name="matmul",
        compiler_params=_params("parallel", "arbitrary"),
    )(a, w)


def _norm_mm_kernel(x_ref, g_ref, w_ref, xn_ref, o_ref, *, scale):
    @pl.when(pl.program_id(1) == 0)
    def _():
        xn_ref[...] = _rms(x_ref[...], g_ref[...]).astype(xn_ref.dtype)

    acc = jnp.dot(xn_ref[...], w_ref[...].astype(BF16), preferred_element_type=F32)
    o_ref[...] = (acc * scale).astype(o_ref.dtype)


def norm_matmul(x, gain, w, out_dtype, scale, col0, n, tm=512, tn=512):
    m, k = x.shape
    tm, tn = _tile(m, tm), _tile(n, tn)
    while col0 % tn:
        tn = _tile(n, tn - 1)
    jb = col0 // tn
    return pl.pallas_call(
        functools.partial(_norm_mm_kernel, scale=scale),
        grid=(m // tm, n // tn),
        in_specs=[
            pl.BlockSpec((tm, k), lambda i, j: (i, 0)),
            pl.BlockSpec((1, k), lambda i, j: (0, 0)),
            pl.BlockSpec((k, tn), lambda i, j: (0, j + jb)),
        ],
        out_specs=[pl.BlockSpec((tm, k), lambda i, j: (i, 0)), pl.BlockSpec((tm, tn), lambda i, j: (i, j))],
        out_shape=[jax.ShapeDtypeStruct((m, k), BF16), jax.ShapeDtypeStruct((m, n), out_dtype)],
        name="norm_matmul",
        compiler_params=_params("parallel", "arbitrary"),
    )(x, gain.reshape(1, k), w)


def _mm_postnorm_kernel(a1_ref, a2_ref, w_ref, x_ref, g_ref, o_ref, *, nk1, nk):
    k = pl.program_id(1)

    @pl.when(k == 0)
    def _():
        o_ref[...] = jnp.zeros_like(o_ref)

    @pl.when(k < nk1)
    def _():
        o_ref[...] += jnp.dot(a1_ref[...], w_ref[...], preferred_element_type=F32)

    @pl.when(k >= nk1)
    def _():
        o_ref[...] += jnp.dot(a2_ref[...], w_ref[...], preferred_element_type=F32)

    @pl.when(k == nk - 1)
    def _():
        def post(r, carry):
            rows = pl.ds(pl.multiple_of(r * NORM_ROWS, NORM_ROWS), NORM_ROWS)
            o_ref[rows, :] = x_ref[rows, :] + _rms(o_ref[rows, :], g_ref[...])
            return carry
        lax.fori_loop(0, o_ref.shape[0] // NORM_ROWS, post, 0)


def matmul_postnorm_residual(a1, a2, w, x, gain, tm=512, tk=1024):
    m, k1 = a1.shape
    k2 = a2.shape[1]
    _, n = w.shape
    tm = _tile(m, tm)
    tk = _tile(k1, tk)
    while k2 % tk:
        tk = _tile(k1, tk - 1)
    nk1, nk = k1 // tk, (k1 + k2) // tk
    return pl.pallas_call(
        functools.partial(_mm_postnorm_kernel, nk1=nk1, nk=nk),
        grid=(m // tm, nk),
        in_specs=[
            pl.BlockSpec((tm, tk), lambda i, k: (i, jnp.minimum(k, nk1 - 1))),
            pl.BlockSpec((tm, tk), lambda i, k: (i, jnp.maximum(k - nk1, 0))),
            pl.BlockSpec((tk, n), lambda i, k: (k, 0)),
            pl.BlockSpec((tm, n), lambda i, k: (i, 0)),
            pl.BlockSpec((1, n), lambda i, k: (0, 0)),
        ],
        out_specs=pl.BlockSpec((tm, n), lambda i, k: (i, 0)),
        out_shape=jax.ShapeDtypeStruct((m, n), F32),
        name="outproj_postnorm",
        compiler_params=_params("parallel", "arbitrary"),
    )(a1, a2, w, x, gain.reshape(1, n))


def _gelu_tanh(x):
    c = 0.7978845608028654
    return x * (0.5 * (1.0 + jnp.tanh(c * (x + 0.044715 * (x * x * x)))))


def _compress_kernel(x_ref, pe_ref, w1_ref, w2_ref, o_ref, *, ncp):
    a = jnp.zeros((ncp, HEAD_DIM), F32)
    b = jnp.zeros((ncp, HEAD_DIM), F32)
    for l in range(CMP_STRIDE):
        xl = x_ref[pl.ds(l, ncp, stride=CMP_STRIDE), :]
        lo, hi = l, CMP_STRIDE + l
        a = a + jnp.dot((xl + pe_ref[lo:lo + 1, :]).astype(BF16), w1_ref[lo * HEAD_DIM:(lo + 1) * HEAD_DIM, :],
                        preferred_element_type=F32)
        b = b + jnp.dot((xl + pe_ref[hi:hi + 1, :]).astype(BF16), w1_ref[hi * HEAD_DIM:(hi + 1) * HEAD_DIM, :],
                        preferred_element_type=F32)
    hid = _gelu_tanh(a + pltpu.roll(b, ncp - 1, 0))
    o_ref[...] = jnp.dot(hid.astype(BF16), w2_ref[...], preferred_element_type=F32).astype(o_ref.dtype)


def compress_tokens(kvc, pe, w1, w2, n_groups):
    t = kvc.shape[0]
    ncp = t // CMP_STRIDE
    assert CMP_BLOCK == 2 * CMP_STRIDE
    return pl.pallas_call(
        functools.partial(_compress_kernel, ncp=ncp),
        grid=(2, n_groups),
        in_specs=[
            pl.BlockSpec((t, HEAD_DIM), lambda s, j: (0, s * n_groups + j)),
            pl.BlockSpec((None, CMP_BLOCK, HEAD_DIM), lambda s, j: (s, 0, 0)),
            pl.BlockSpec((None, CMP_BLOCK * HEAD_DIM, HEAD_DIM), lambda s, j: (s, 0, 0)),
            pl.BlockSpec((None, HEAD_DIM, HEAD_DIM), lambda s, j: (s, 0, 0)),
        ],
        out_specs=pl.BlockSpec((None, None, ncp, HEAD_DIM), lambda s, j: (s, j, 0, 0)),
        out_shape=jax.ShapeDtypeStruct((2, n_groups, ncp, HEAD_DIM), BF16),
        name="compress_tokens",
        compiler_params=_params("parallel", "parallel"),
    )(kvc, pe, w1, w2)


def _attn_kernel(slopes_ref, q_ref, kcmp_ref, vcmp_ref, ks_ref, vs_ref, kw_ref, vw_ref, gl_ref,
                 o_ref, p_scr, imp_scr, sel_scr, ocmp_scr, acc_slc, acc_win, alibi_scr, u_scr, flag_ref,
                 *, hpg, nc, ncp, nb, n_sel):
    g = pl.program_id(0)
    i = pl.program_id(1)
    t0 = i * QB
    w = hpg * QB
    ninf = -jnp.inf

    q = q_ref[...]
    qs = jnp.concatenate([q[:, h * HEAD_DIM:(h + 1) * HEAD_DIM] for h in range(hpg)], axis=0)
    slopes = [slopes_ref[g * hpg + h] for h in range(hpg)]

    s = lax.dot_general(kcmp_ref[...], qs, _NT, preferred_element_type=F32)
    c_iota = lax.broadcasted_iota(jnp.int32, (ncp, QB), 0)
    q_iota = lax.broadcasted_iota(jnp.int32, (ncp, QB), 1)
    cmp_end = c_iota * CMP_STRIDE + (CMP_BLOCK - 1)
    mask_c = (cmp_end <= t0 + q_iota) & (c_iota < nc)
    rel_c = (cmp_end - t0).astype(F32)
    psum = jnp.zeros((ncp, QB), F32)
    for h in range(hpg):
        sh = s[:, h * QB:(h + 1) * QB] + slopes[h] * rel_c
        sh = jnp.where(mask_c, sh, ninf)
        m = jnp.maximum(jnp.max(sh, axis=0, keepdims=True), NEG_INF)
        p = jnp.exp2(sh - m)
        l = jnp.sum(p, axis=0, keepdims=True)
        pn = p * jnp.where(l > 0.0, 1.0 / l, 0.0)
        psum = psum + pn
        ocmp_scr[:, h * QB:(h + 1) * QB] = lax.dot_general(vcmp_ref[...], pn.astype(BF16), _TN,
                                                            preferred_element_type=F32)

    p_scr[0:SUBLANES, :] = jnp.zeros((SUBLANES, QB), F32)
    p_scr[SUBLANES:SUBLANES + ncp, :] = psum
    imp = p_scr[pl.ds(SUBLANES - 1, nb, stride=4), :]
    for j in range(4):
        imp = imp + p_scr[pl.ds(SUBLANES + j, nb, stride=4), :]
    n_iota = lax.broadcasted_iota(jnp.int32, (nb, QB), 0)
    tq = t0 + lax.broadcasted_iota(jnp.int32, (nb, QB), 1)
    cur = lax.shift_right_logical(tq, 6)
    valid = n_iota <= cur
    back = cur - n_iota
    forced = (n_iota == 0) | ((back >= 0) & (back < N_LOCAL_FORCED))
    imp = jnp.where(forced & valid, FORCED_SCORE, imp)
    imp = jnp.where(valid, imp, NEG_INF)
    imp_scr[...] = imp

    n_grp = nb // SUBLANES
    vals = [imp[j * SUBLANES:(j + 1) * SUBLANES, :] for j in range(n_grp)]
    row8 = lax.broadcasted_iota(jnp.int32, (SUBLANES, QB), 0)

    def rank_chunk(mc, cnts):
        cnts = list(cnts)
        for mi in range(mc * SUBLANES, (mc + 1) * SUBLANES):
            vm = jnp.broadcast_to(imp_scr[pl.ds(mi, 1), :], (SUBLANES, QB))
            for j in range(n_grp):
                lo = j * SUBLANES
                if lo + SUBLANES - 1 < mi:
                    beats = vm > vals[j]
                elif lo > mi:
                    beats = vm >= vals[j]
                else:
                    beats = (vm > vals[j]) | ((vm == vals[j]) & (row8 > mi - lo))
                cnts[j] = cnts[j] + jnp.where(beats, 1.0, 0.0)
        return tuple(cnts)

    last_valid = (t0 + QB - 1) // SLC_BLOCK
    cnts = tuple(jnp.zeros((SUBLANES, QB), F32) for _ in range(n_grp))
    for mc in range(n_grp):
        cnts = lax.cond(mc * SUBLANES <= last_valid, functools.partial(rank_chunk, mc), lambda c: c, cnts)
    cnt = jnp.concatenate(cnts, axis=0)
    sel = (cnt < float(n_sel)) & valid
    sel_scr[...] = jnp.where(sel, 0.0, ninf)
    blocks_per_kt = KT // SLC_BLOCK
    sel_f = jnp.where(sel, 1.0, 0.0)
    for kt in range(nb // blocks_per_kt):
        any_sel = jnp.max(sel_f[kt * blocks_per_kt:(kt + 1) * blocks_per_kt, :])
        flag_ref[kt] = (any_sel > 0.5).astype(jnp.int32)

    rc_i = (lax.broadcasted_iota(jnp.int32, (WIN_KEYS, QB), 1)
            - lax.broadcasted_iota(jnp.int32, (WIN_KEYS, QB), 0))
    @pl.when(i == 0)
    def _():
        rc_f = (-rc_i).astype(F32)
        for h in range(hpg):
            alibi_scr[h] = slopes[h] * rc_f

    init = (tuple(jnp.full((1, QB), NEG_INF, F32) for _ in range(hpg)),
            tuple(jnp.zeros((1, QB), F32) for _ in range(hpg)))

    def stage_a(kt, slot):
        k0 = pl.multiple_of(kt * KT, KT)
        st = lax.dot_general(ks_ref[pl.ds(k0, KT), :], qs, _NT, preferred_element_type=F32)
        rows = [jnp.broadcast_to(sel_scr[pl.ds(kt * blocks_per_kt + b, 1), :], (SLC_BLOCK, QB))
                for b in range(blocks_per_kt)]
        diff = (t0 - k0) + rc_i[0:KT, :]
        base = jnp.where(diff >= 0, jnp.concatenate(rows, axis=0), ninf)
        off = (k0 - t0).astype(F32)
        tile_max = []
        for h in range(hpg):
            u = st[:, h * QB:(h + 1) * QB] + (alibi_scr[h, 0:KT, :] + base)
            u_scr[slot, h] = u
            tile_max.append(jnp.max(u, axis=0, keepdims=True) + slopes[h] * off)
        return tuple(tile_max)

    def stage_b(kt, slot, tile_max, ms, ls):
        k0 = pl.multiple_of(kt * KT, KT)
        off = (k0 - t0).astype(F32)
        new_m, new_l, alphas, ps = [], [], [], []
        for h in range(hpg):
            mh = jnp.maximum(ms[h], tile_max[h])
            alpha = jnp.exp2(ms[h] - mh)
            p = jnp.exp2(u_scr[slot, h] - (mh - slopes[h] * off))
            new_m.append(mh)
            new_l.append(alpha * ls[h] + jnp.sum(p, axis=0, keepdims=True))
            alphas.append(alpha)
            ps.append(p.astype(BF16))
        pv = lax.dot_general(vs_ref[pl.ds(k0, KT), :], jnp.concatenate(ps, axis=1), _TN,
                             preferred_element_type=F32)
        acc_slc[...] = acc_slc[...] * jnp.concatenate(alphas, axis=1) + pv
        return tuple(new_m), tuple(new_l)

    def slc_step(kt, carry):
        def visit(c):
            ms, ls, tile_max, prev, slot = c
            ms, ls = stage_b(prev, slot, tile_max, ms, ls)
            return ms, ls, stage_a(kt, 1 - slot), kt, 1 - slot
        return lax.cond(flag_ref[kt] > 0, visit, lambda c: c, carry)

    kd = t0 // KT
    acc_slc[...] = jnp.zeros_like(acc_slc)
    zero = jnp.int32(0)
    ms, ls, tile_max, prev, slot = lax.fori_loop(1, kd + 1, slc_step, init + (stage_a(zero, zero), zero, zero))
    _, l_slc = stage_b(prev, slot, tile_max, ms, ls)

    w0 = pl.multiple_of(jnp.maximum(t0 - WINDOW, 0), KB)
    diff_w = (t0 - w0) + rc_i
    base_w = jnp.where((diff_w >= 0) & (diff_w < WINDOW), 0.0, ninf)
    st_w = lax.dot_general(kw_ref[pl.ds(w0, WIN_KEYS), :], qs, _NT, preferred_element_type=F32)
    l_win, ps = [], []
    for h in range(hpg):
        u = st_w[:, h * QB:(h + 1) * QB] + (alibi_scr[h] + base_w)
        p = jnp.exp2(u - jnp.max(u, axis=0, keepdims=True))
        l_win.append(jnp.sum(p, axis=0, keepdims=True))
        ps.append(p.astype(BF16))
    acc_win[...] = lax.dot_general(vw_ref[pl.ds(w0, WIN_KEYS), :], jnp.concatenate(ps, axis=1), _TN,
                                   preferred_element_type=F32)

    for h in range(hpg):
        cols = slice(h * QB, (h + 1) * QB)
        g_cmp = jax.nn.sigmoid(gl_ref[0, h:h + 1, :])
        g_slc = jax.nn.sigmoid(gl_ref[1, h:h + 1, :])
        g_win = jax.nn.sigmoid(gl_ref[2, h:h + 1, :])
        oT = (g_cmp * ocmp_scr[:, cols] + (g_slc / l_slc[h]) * acc_slc[:, cols]
              + (g_win / l_win[h]) * acc_win[:, cols])
        o_ref[:, h * HEAD_DIM:(h + 1) * HEAD_DIM] = oT.T.astype(o_ref.dtype)


def sparse_attention(slopes, q, cmp_kv, kv_rest, gate_logits_t, *, nc):
    t, hd = q.shape
    _, n_groups, ncp, _ = cmp_kv.shape
    hpg = hd // HEAD_DIM // n_groups
    nb = t // SLC_BLOCK
    assert t % KT == 0 and t >= WIN_KEYS and ncp == 4 * nb and nb % SUBLANES == 0
    assert (CMP_BLOCK, CMP_STRIDE, SLC_BLOCK) == (32, 16, 64)
    n_sel = min(N_SELECT, nb)
    w = hpg * QB
    kern = functools.partial(_attn_kernel, hpg=hpg, nc=nc, ncp=ncp, nb=nb, n_sel=n_sel)
    kv_col = lambda part: (lambda g, i: (0, part * n_groups + g))
    return pl.pallas_call(
        kern,
        grid=(n_groups, t // QB),
        in_specs=[
            pl.BlockSpec(memory_space=pltpu.SMEM),
            pl.BlockSpec((QB, hpg * HEAD_DIM), lambda g, i: (i, g)),
            pl.BlockSpec((None, None, ncp, HEAD_DIM), lambda g, i: (0, g, 0, 0)),
            pl.BlockSpec((None, None, ncp, HEAD_DIM), lambda g, i: (1, g, 0, 0)),
            pl.BlockSpec((t, HEAD_DIM), kv_col(0)),
            pl.BlockSpec((t, HEAD_DIM), kv_col(1)),
            pl.BlockSpec((t, HEAD_DIM), kv_col(2)),
            pl.BlockSpec((t, HEAD_DIM), kv_col(3)),
            pl.BlockSpec((3, None, hpg, QB), lambda g, i: (0, g, 0, i)),
        ],
        out_specs=pl.BlockSpec((QB, hpg * HEAD_DIM), lambda g, i: (i, g)),
        out_shape=jax.ShapeDtypeStruct((t, hd), BF16),
        scratch_shapes=[
            pltpu.VMEM((SUBLANES + ncp, QB), F32),
            pltpu.VMEM((nb, QB), F32),
            pltpu.VMEM((nb, QB), F32),
            pltpu.VMEM((HEAD_DIM, w), F32),
            pltpu.VMEM((HEAD_DIM, w), F32),
            pltpu.VMEM((HEAD_DIM, w), F32),
            pltpu.VMEM((hpg, WIN_KEYS, QB), F32),
            pltpu.VMEM((2, hpg, KT, QB), F32),
            pltpu.SMEM((t // KT,), jnp.int32),
        ],
        name="sparse_attention",
        compiler_params=_params("parallel", "arbitrary"),
    )(slopes, q, cmp_kv, cmp_kv, kv_rest, kv_rest, kv_rest, kv_rest, gate_logits_t)


def _inproj_conv_kernel(a_ref, wb_ref, wc_ref, wh_ref, cw_ref, o_ref, halo_ref):
    i = pl.program_id(0)
    j = pl.program_id(1)
    a = a_ref[...]
    b = jnp.dot(a, wb_ref[...], preferred_element_type=F32)
    u = (jnp.dot(a, wc_ref[...], preferred_element_type=F32)
         * jnp.dot(a, wh_ref[...], preferred_element_type=F32))
    wk = cw_ref[...]
    w0, w1, w2 = wk[0:1, :], wk[1:2, :], wk[2:3, :]
    tm = u.shape[0]
    y = b * (w0 * pltpu.roll(u, 2, 0) + w1 * pltpu.roll(u, 1, 0) + w2 * u)
    o_ref[...] = y.astype(o_ref.dtype)
    @pl.when(i == 0)
    def _():
        halo_ref[j] = jnp.zeros(halo_ref.shape[1:], F32)

    ue = jnp.concatenate([halo_ref[j], u[0:SUBLANES, :]], axis=0)
    n2 = 2 * SUBLANES
    u1 = pltpu.roll(ue, 1, 0)[SUBLANES:n2, :]
    u2 = pltpu.roll(ue, 2, 0)[SUBLANES:n2, :]
    y0 = b[0:SUBLANES, :] * (w0 * u2 + w1 * u1 + w2 * u[0:SUBLANES, :])
    o_ref[0:SUBLANES, :] = y0.astype(o_ref.dtype)
    halo_ref[j] = u[tm - SUBLANES:tm, :]


def inproj_short_conv(a, w, conv_w, *, col0, cw, tm=1024, tn=512):
    m, k = a.shape
    tm, tn = _tile(m, tm), _tile(cw, tn)
    while col0 % tn:
        tn = _tile(cw, tn - 1)
    ncb = cw // tn
    jb = col0 // tn
    wcol = lambda part: (lambda i, j: (0, jb + part * ncb + j))
    return pl.pallas_call(
        _inproj_conv_kernel,
        grid=(m // tm, ncb),
        in_specs=[
            pl.BlockSpec((tm, k), lambda i, j: (i, 0)),
            pl.BlockSpec((k, tn), wcol(0)),
            pl.BlockSpec((k, tn), wcol(1)),
            pl.BlockSpec((k, tn), wcol(2)),
            pl.BlockSpec((CONV_K, tn), lambda i, j: (0, j)),
        ],
        out_specs=pl.BlockSpec((tm, tn), lambda i, j: (i, j)),
        out_shape=jax.ShapeDtypeStruct((m, cw), BF16),
        scratch_shapes=[pltpu.VMEM((ncb, SUBLANES, tn), F32)],
        name="inproj_short_conv",
        compiler_params=_params("arbitrary", "arbitrary"),
    )(a, w, w, w, conv_w)


def _mem_kernel(x_ref, gpre_ref, gpost_ref, wq_ref, km_ref, vm_ref, wo_ref, o_ref):
    x = x_ref[...]
    xn = _rms(x, gpre_ref[...]).astype(BF16)
    qm = jnp.dot(xn, wq_ref[...], preferred_element_type=F32) * (MEM_HEAD_DIM ** -0.5)
    outs = []
    for h in range(MEM_HEADS):
        cols = slice(h * MEM_HEAD_DIM, (h + 1) * MEM_HEAD_DIM)
        s = lax.dot_general(qm[:, cols].astype(BF16), km_ref[:, cols], _NT, preferred_element_type=F32)
        p = jnp.exp(s - jnp.max(s, axis=-1, keepdims=True))
        p = p / jnp.sum(p, axis=-1, keepdims=True)
        outs.append(jnp.dot(p.astype(BF16), vm_ref[:, cols], preferred_element_type=F32).astype(BF16))
    y = jnp.dot(jnp.concatenate(outs, axis=1), wo_ref[...], preferred_element_type=F32)
    o_ref[...] = x + _rms(y, gpost_ref[...])


def memory_sublayer(x, gpre, gpost, wq, km, vm, wo, tm=512):
    m, d = x.shape
    mw = wq.shape[1]
    nm = km.shape[0]
    tm = _tile(m, tm)
    full = lambda i: (0, 0)
    return pl.pallas_call(
        _mem_kernel,
        grid=(m // tm,),
        in_specs=[
            pl.BlockSpec((tm, d), lambda i: (i, 0)),
            pl.BlockSpec((1, d), full),
            pl.BlockSpec((1, d), full),
            pl.BlockSpec((d, mw), full),
            pl.BlockSpec((nm, mw), full),
            pl.BlockSpec((nm, mw), full),
            pl.BlockSpec((mw, d), full),
        ],
        out_specs=pl.BlockSpec((tm, d), lambda i: (i, 0)),
        out_shape=jax.ShapeDtypeStruct((m, d), F32),
        name="memory_sublayer",
        compiler_params=_params("parallel"),
    )(x, gpre.reshape(1, d), gpost.reshape(1, d), wq, km, vm, wo)


def _mlp_kernel(x_ref, gpre_ref, gpost_ref, wu_ref, wd_ref, o_ref, xn_scr, *, nf):
    f = pl.program_id(1)

    n_chunks = x_ref.shape[0] // NORM_ROWS

    @pl.when(f == 0)
    def _():
        def pre(r, carry):
            rows = pl.ds(pl.multiple_of(r * NORM_ROWS, NORM_ROWS), NORM_ROWS)
            xn_scr[rows, :] = _rms(x_ref[rows, :], gpre_ref[...]).astype(BF16)
            o_ref[rows, :] = jnp.zeros((NORM_ROWS, o_ref.shape[1]), F32)
            return carry
        lax.fori_loop(0, n_chunks, pre, 0)

    hid = jnp.dot(xn_scr[...], wu_ref[...], preferred_element_type=F32)
    hid = jnp.square(jnp.maximum(hid, 0.0))
    o_ref[...] += jnp.dot(hid.astype(BF16), wd_ref[...], preferred_element_type=F32)

    @pl.when(f == nf - 1)
    def _():
        def post(r, carry):
            rows = pl.ds(pl.multiple_of(r * NORM_ROWS, NORM_ROWS), NORM_ROWS)
            o_ref[rows, :] = x_ref[rows, :] + _rms(o_ref[rows, :], gpost_ref[...])
            return carry
        lax.fori_loop(0, n_chunks, post, 0)


def mlp_sublayer(x, gpre, gpost, w_up, w_down, tm=512, tf=1024):
    m, d = x.shape
    dff = w_up.shape[1]
    tm, tf = _tile(m, tm), _tile(dff, tf)
    nf = dff // tf
    w_up = w_up.astype(BF16)
    w_down = w_down.astype(BF16)
    return pl.pallas_call(
        functools.partial(_mlp_kernel, nf=nf),
        grid=(m // tm, nf),
        in_specs=[
            pl.BlockSpec((tm, d), lambda i, f: (i, 0), pipeline_mode=pl.Buffered(1)),
            pl.BlockSpec((1, d), lambda i, f: (0, 0)),
            pl.BlockSpec((1, d), lambda i, f: (0, 0)),
            pl.BlockSpec((d, tf), lambda i, f: (0, f)),
            pl.BlockSpec((tf, d), lambda i, f: (f, 0)),
        ],
        out_specs=pl.BlockSpec((tm, d), lambda i, f: (i, 0), pipeline_mode=pl.Buffered(1)),
        out_shape=jax.ShapeDtypeStruct((m, d), F32),
        scratch_shapes=[pltpu.VMEM((tm, d), BF16)],
        name="mlp_sublayer",
        compiler_params=pltpu.CompilerParams(dimension_semantics=("parallel", "arbitrary"),
                                             vmem_limit_bytes=MLP_VMEM_LIMIT),
    )(x, gpre.reshape(1, d), gpost.reshape(1, d), w_up, w_down)


def _mixer(x, norm_pre, norm_post, w_in, cmp_k_pe, cmp_k_w1, cmp_k_w2, cmp_v_pe, cmp_v_w1, cmp_v_w2,
           conv_w, w_out):
    t, d = x.shape
    attn_w = d // 2
    conv_cw = d - attn_w
    n_heads = attn_w // HEAD_DIM
    n_groups = n_heads // 4
    kvw = n_groups * HEAD_DIM
    o_kv = attn_w
    o_gate = o_kv + 6 * kvw
    o_conv = o_gate + 3 * n_heads
    assert w_in.shape[1] == o_conv + 3 * conv_cw

    assert o_gate % LANES == 0 and o_gate + LANES <= w_in.shape[1]
    w_conv = w_in[:, o_conv:].astype(BF16)

    xn, q = norm_matmul(x, norm_pre, w_in, BF16, scale=HEAD_DIM ** -0.5 * LOG2E, col0=0, n=attn_w)
    kvc = matmul(xn, w_in, F32, col0=o_kv, n=2 * kvw)
    kv_rest = matmul(xn, w_in, BF16, col0=o_kv + 2 * kvw, n=4 * kvw)
    o_conv_out = inproj_short_conv(xn, w_conv, conv_w, col0=0, cw=conv_cw)
    gate_logits = matmul(xn, w_in, F32, col0=o_gate, n=LANES)[:, :3 * n_heads]

    nc = (t - CMP_BLOCK) // CMP_STRIDE + 1
    pe = jnp.stack([cmp_k_pe, cmp_v_pe])
    w1 = jnp.stack([cmp_k_w1, cmp_v_w1]).astype(BF16)
    w2 = jnp.stack([cmp_k_w2, cmp_v_w2]).astype(BF16)
    cmp_kv = compress_tokens(kvc, pe, w1, w2, n_groups)

    gl_t = gate_logits.T.reshape(3, n_groups, n_heads // n_groups, t)
    idx = jnp.arange(1, n_heads + 1, dtype=F32)
    slopes = jnp.exp2(-8.0 * idx / n_heads) * LOG2E
    o_attn = sparse_attention(slopes, q, cmp_kv, kv_rest, gl_t, nc=nc)
    return matmul_postnorm_residual(o_attn, o_conv_out, w_out.astype(BF16), x, norm_post)


def _memory(x, mem, norm_pre, norm_kv, norm_post, wq, wk, wv, wo):
    mn = rmsnorm_cast(mem, norm_kv)
    km = matmul(mn, wk, BF16)
    vm = matmul(mn, wv, BF16)
    return memory_sublayer(x, norm_pre, norm_post, wq.astype(BF16), km, vm, wo.astype(BF16))


def kernel(x, mem, mix_norm_pre, mix_norm_post, w_in, cmp_k_pe, cmp_k_w1, cmp_k_w2, cmp_v_pe, cmp_v_w1,
           cmp_v_w2, conv_w, w_out, mem_norm_pre, mem_norm_kv, mem_norm_post, w_mem_q, w_mem_k, w_mem_v,
           w_mem_o, mlp_norm_pre, mlp_norm_post, w_up, w_down):
    b, t, d = x.shape
    assert b == 1
    h = x[0]
    m = mem[0]
    for l in range(w_in.shape[0]):
        h = _mixer(h, mix_norm_pre[l], mix_norm_post[l], w_in[l], cmp_k_pe[l], cmp_k_w1[l], cmp_k_w2[l],
                   cmp_v_pe[l], cmp_v_w1[l], cmp_v_w2[l], conv_w[l], w_out[l])
        h = _memory(h, m, mem_norm_pre[l], mem_norm_kv[l], mem_norm_post[l], w_mem_q[l], w_mem_k[l],
                    w_mem_v[l], w_mem_o[l])
        h = mlp_sublayer(h, mlp_norm_pre[l], mlp_norm_post[l], w_up[l], w_down[l])
    return h[None]
```

```python
import functools

import jax
import jax.numpy as jnp
from jax import lax
from jax.experimental import pallas as pl
from jax.experimental.pallas import tpu as pltpu

HEAD_DIM = 128
CMP_BLOCK = 32
CMP_STRIDE = 16
SLC_BLOCK = 64
N_SELECT = 16
N_LOCAL_FORCED = 2
WINDOW = 512
CONV_K = 3
MEM_HEADS = 4
MEM_HEAD_DIM = 128
RMS_EPS = 1e-6
NEG_INF = -1e30
FORCED_SCORE = 1e9
LOG2E = 1.4426950408889634

V7X_VMEM_BYTES = 64 * 1024 * 1024
VMEM_LIMIT = V7X_VMEM_BYTES - 8 * 1024 * 1024
MLP_VMEM_LIMIT = V7X_VMEM_BYTES - 4 * 1024 * 1024
NORM_ROWS = 64
LANES = 128
SUBLANES = 8

QB = 128
KB = 128
KT = 512
WIN_KEYS = WINDOW + QB

F32 = jnp.float32
BF16 = jnp.bfloat16

_NT = (((1,), (1,)), ((), ()))
_TN = (((0,), (0,)), ((), ()))


def _tile(n, pref):
    t = min(n, pref)
    while n % t:
        t -= 1
    return t


def _params(*sem):
    return pltpu.CompilerParams(dimension_semantics=sem, vmem_limit_bytes=VMEM_LIMIT)


def _rms(x, gain):
    return x * lax.rsqrt(jnp.mean(x * x, axis=-1, keepdims=True) + RMS_EPS) * gain


def _rmsnorm_kernel(x_ref, g_ref, o_ref):
    o_ref[...] = _rms(x_ref[...].astype(F32), g_ref[...]).astype(o_ref.dtype)


def rmsnorm_cast(x, gain, out_dtype=BF16):
    m, d = x.shape
    tm = _tile(m, 256)
    return pl.pallas_call(
        _rmsnorm_kernel,
        grid=(m // tm,),
        in_specs=[pl.BlockSpec((tm, d), lambda i: (i, 0)), pl.BlockSpec((1, d), lambda i: (0, 0))],
        out_specs=pl.BlockSpec((tm, d), lambda i: (i, 0)),
        out_shape=jax.ShapeDtypeStruct((m, d), out_dtype),
        name="rmsnorm_cast",
        compiler_params=_params("parallel"),
    )(x, gain.reshape(1, d))


def _mm_kernel(a_ref, w_ref, o_ref, *, scale):
    acc = jnp.dot(a_ref[...], w_ref[...], preferred_element_type=F32)
    if scale is not None:
        acc = acc * scale
    o_ref[...] = acc.astype(o_ref.dtype)


def matmul(a, w, out_dtype, scale=None, col0=0, n=None, tm=1024, tn=1024):
    m, k = a.shape
    n = w.shape[1] - col0 if n is None else n
    tm, tn = _tile(m, tm), _tile(n, tn)
    while col0 % tn:
        tn = _tile(n, tn - 1)
    jb = col0 // tn
    return pl.pallas_call(
        functools.partial(_mm_kernel, scale=scale),
        grid=(m // tm, n // tn),
        in_specs=[pl.BlockSpec((tm, k), lambda i, j: (i, 0)), pl.BlockSpec((k, tn), lambda i, j: (0, j + jb))],
        out_specs=pl.BlockSpec((tm, tn), lambda i, j: (i, j)),
        out_shape=jax.ShapeDtypeStruct((m, n), out_dtype),
        name="matmul",
        compiler_params=_params("parallel", "arbitrary"),
    )(a, w)


def _mm_postnorm_kernel(a1_ref, a2_ref, w_ref, x_ref, g_ref, o_ref, *, nk1, nk):
    k = pl.program_id(1)

    @pl.when(k == 0)
    def _():
        o_ref[...] = jnp.zeros_like(o_ref)

    @pl.when(k < nk1)
    def _():
        o_ref[...] += jnp.dot(a1_ref[...], w_ref[...], preferred_element_type=F32)

    @pl.when(k >= nk1)
    def _():
        o_ref[...] += jnp.dot(a2_ref[...], w_ref[...], preferred_element_type=F32)

    @pl.when(k == nk - 1)
    def _():
        def post(r, carry):
            rows = pl.ds(pl.multiple_of(r * NORM_ROWS, NORM_ROWS), NORM_ROWS)
            o_ref[rows, :] = x_ref[rows, :] + _rms(o_ref[rows, :], g_ref[...])
            return carry
        lax.fori_loop(0, o_ref.shape[0] // NORM_ROWS, post, 0)


def matmul_postnorm_residual(a1, a2, w, x, gain, tm=512, tk=1024):
    m, k1 = a1.shape
    k2 = a2.shape[1]
    _, n = w.shape
    tm = _tile(m, tm)
    tk = _tile(k1, tk)
    while k2 % tk:
        tk = _tile(k1, tk - 1)
    nk1, nk = k1 // tk, (k1 + k2) // tk
    return pl.pallas_call(
        functools.partial(_mm_postnorm_kernel, nk1=nk1, nk=nk),
        grid=(m // tm, nk),
        in_specs=[
            pl.BlockSpec((tm, tk), lambda i, k: (i, jnp.minimum(k, nk1 - 1))),
            pl.BlockSpec((tm, tk), lambda i, k: (i, jnp.maximum(k - nk1, 0))),
            pl.BlockSpec((tk, n), lambda i, k: (k, 0)),
            pl.BlockSpec((tm, n), lambda i, k: (i, 0)),
            pl.BlockSpec((1, n), lambda i, k: (0, 0)),
        ],
        out_specs=pl.BlockSpec((tm, n), lambda i, k: (i, 0)),
        out_shape=jax.ShapeDtypeStruct((m, n), F32),
        name="outproj_postnorm",
        compiler_params=_params("parallel", "arbitrary"),
    )(a1, a2, w, x, gain.reshape(1, n))


def _gelu_tanh(x):
    c = 0.7978845608028654
    return x * (0.5 * (1.0 + jnp.tanh(c * (x + 0.044715 * (x * x * x)))))


def _compress_kernel(x_ref, pe_ref, w1_ref, w2_ref, o_ref, *, ncp):
    a = jnp.zeros((ncp, HEAD_DIM), F32)
    b = jnp.zeros((ncp, HEAD_DIM), F32)
    for l in range(CMP_STRIDE):
        xl = x_ref[pl.ds(l, ncp, stride=CMP_STRIDE), :]
        lo, hi = l, CMP_STRIDE + l
        a = a + jnp.dot((xl + pe_ref[lo:lo + 1, :]).astype(BF16), w1_ref[lo * HEAD_DIM:(lo + 1) * HEAD_DIM, :],
                        preferred_element_type=F32)
        b = b + jnp.dot((xl + pe_ref[hi:hi + 1, :]).astype(BF16), w1_ref[hi * HEAD_DIM:(hi + 1) * HEAD_DIM, :],
                        preferred_element_type=F32)
    hid = _gelu_tanh(a + pltpu.roll(b, ncp - 1, 0))
    o_ref[...] = jnp.dot(hid.astype(BF16), w2_ref[...], preferred_element_type=F32).astype(o_ref.dtype)


def compress_tokens(kvc, pe, w1, w2, n_groups):
    t = kvc.shape[0]
    ncp = t // CMP_STRIDE
    assert CMP_BLOCK == 2 * CMP_STRIDE
    return pl.pallas_call(
        functools.partial(_compress_kernel, ncp=ncp),
        grid=(2, n_groups),
        in_specs=[
            pl.BlockSpec((t, HEAD_DIM), lambda s, j: (0, s * n_groups + j)),
            pl.BlockSpec((None, CMP_BLOCK, HEAD_DIM), lambda s, j: (s, 0, 0)),
            pl.BlockSpec((None, CMP_BLOCK * HEAD_DIM, HEAD_DIM), lambda s, j: (s, 0, 0)),
            pl.BlockSpec((None, HEAD_DIM, HEAD_DIM), lambda s, j: (s, 0, 0)),
        ],
        out_specs=pl.BlockSpec((None, None, ncp, HEAD_DIM), lambda s, j: (s, j, 0, 0)),
        out_shape=jax.ShapeDtypeStruct((2, n_groups, ncp, HEAD_DIM), BF16),
        name="compress_tokens",
        compiler_params=_params("parallel", "parallel"),
    )(kvc, pe, w1, w2)


def _attn_kernel(slopes_ref, q_ref, kcmp_ref, vcmp_ref, ks_ref, vs_ref, kw_ref, vw_ref, gl_ref,
                 o_ref, p_scr, imp_scr, sel_scr, ocmp_scr, acc_slc, acc_win, alibi_scr, u_scr, flag_ref,
                 *, hpg, nc, ncp, nb, n_sel):
    g = pl.program_id(0)
    i = pl.program_id(1)
    t0 = i * QB
    w = hpg * QB
    ninf = -jnp.inf

    q = q_ref[...]
    qs = jnp.concatenate([q[:, h * HEAD_DIM:(h + 1) * HEAD_DIM] for h in range(hpg)], axis=0)
    slopes = [slopes_ref[g * hpg + h] for h in range(hpg)]

    s = lax.dot_general(kcmp_ref[...], qs, _NT, preferred_element_type=F32)
    c_iota = lax.broadcasted_iota(jnp.int32, (ncp, QB), 0)
    q_iota = lax.broadcasted_iota(jnp.int32, (ncp, QB), 1)
    cmp_end = c_iota * CMP_STRIDE + (CMP_BLOCK - 1)
    mask_c = (cmp_end <= t0 + q_iota) & (c_iota < nc)
    rel_c = (cmp_end - t0).astype(F32)
    psum = jnp.zeros((ncp, QB), F32)
    for h in range(hpg):
        sh = s[:, h * QB:(h + 1) * QB] + slopes[h] * rel_c
        sh = jnp.where(mask_c, sh, ninf)
        m = jnp.maximum(jnp.max(sh, axis=0, keepdims=True), NEG_INF)
        p = jnp.exp2(sh - m)
        l = jnp.sum(p, axis=0, keepdims=True)
        pn = p * jnp.where(l > 0.0, 1.0 / l, 0.0)
        psum = psum + pn
        ocmp_scr[:, h * QB:(h + 1) * QB] = lax.dot_general(vcmp_ref[...], pn.astype(BF16), _TN,
                                                            preferred_element_type=F32)

    p_scr[0:SUBLANES, :] = jnp.zeros((SUBLANES, QB), F32)
    p_scr[SUBLANES:SUBLANES + ncp, :] = psum
    imp = p_scr[pl.ds(SUBLANES - 1, nb, stride=4), :]
    for j in range(4):
        imp = imp + p_scr[pl.ds(SUBLANES + j, nb, stride=4), :]
    n_iota = lax.broadcasted_iota(jnp.int32, (nb, QB), 0)
    tq = t0 + lax.broadcasted_iota(jnp.int32, (nb, QB), 1)
    cur = lax.shift_right_logical(tq, 6)
    valid = n_iota <= cur
    back = cur - n_iota
    forced = (n_iota == 0) | ((back >= 0) & (back < N_LOCAL_FORCED))
    imp = jnp.where(forced & valid, FORCED_SCORE, imp)
    imp = jnp.where(valid, imp, NEG_INF)
    imp_scr[...] = imp

    n_grp = nb // SUBLANES
    vals = [imp[j * SUBLANES:(j + 1) * SUBLANES, :] for j in range(n_grp)]
    row8 = lax.broadcasted_iota(jnp.int32, (SUBLANES, QB), 0)

    def rank_chunk(mc, cnts):
        cnts = list(cnts)
        for mi in range(mc * SUBLANES, (mc + 1) * SUBLANES):
            vm = jnp.broadcast_to(imp_scr[pl.ds(mi, 1), :], (SUBLANES, QB))
            for j in range(n_grp):
                lo = j * SUBLANES
                if lo + SUBLANES - 1 < mi:
                    beats = vm > vals[j]
                elif lo > mi:
                    beats = vm >= vals[j]
                else:
                    beats = (vm > vals[j]) | ((vm == vals[j]) & (row8 > mi - lo))
                cnts[j] = cnts[j] + jnp.where(beats, 1.0, 0.0)
        return tuple(cnts)

    last_valid = (t0 + QB - 1) // SLC_BLOCK
    cnts = tuple(jnp.zeros((SUBLANES, QB), F32) for _ in range(n_grp))
    for mc in range(n_grp):
        cnts = lax.cond(mc * SUBLANES <= last_valid, functools.partial(rank_chunk, mc), lambda c: c, cnts)
    cnt = jnp.concatenate(cnts, axis=0)
    sel = (cnt < float(n_sel)) & valid
    sel_scr[...] = jnp.where(sel, 0.0, ninf)
    blocks_per_kt = KT // SLC_BLOCK
    sel_f = jnp.where(sel, 1.0, 0.0)
    for kt in range(nb // blocks_per_kt):
        any_sel = jnp.max(sel_f[kt * blocks_per_kt:(kt + 1) * blocks_per_kt, :])
        flag_ref[kt] = (any_sel > 0.5).astype(jnp.int32)

    rc_i = (lax.broadcasted_iota(jnp.int32, (WIN_KEYS, QB), 1)
            - lax.broadcasted_iota(jnp.int32, (WIN_KEYS, QB), 0))
    @pl.when(i == 0)
    def _():
        rc_f = (-rc_i).astype(F32)
        for h in range(hpg):
            alibi_scr[h] = slopes[h] * rc_f

    init = (tuple(jnp.full((1, QB), NEG_INF, F32) for _ in range(hpg)),
            tuple(jnp.zeros((1, QB), F32) for _ in range(hpg)))

    def stage_a(kt, slot):
        k0 = pl.multiple_of(kt * KT, KT)
        st = lax.dot_general(ks_ref[pl.ds(k0, KT), :], qs, _NT, preferred_element_type=F32)
        rows = [jnp.broadcast_to(sel_scr[pl.ds(kt * blocks_per_kt + b, 1), :], (SLC_BLOCK, QB))
                for b in range(blocks_per_kt)]
        diff = (t0 - k0) + rc_i[0:KT, :]
        base = jnp.where(diff >= 0, jnp.concatenate(rows, axis=0), ninf)
        off = (k0 - t0).astype(F32)
        tile_max = []
        for h in range(hpg):
            u = st[:, h * QB:(h + 1) * QB] + (alibi_scr[h, 0:KT, :] + base)
            u_scr[slot, h] = u
            tile_max.append(jnp.max(u, axis=0, keepdims=True) + slopes[h] * off)
        return tuple(tile_max)

    def stage_b(kt, slot, tile_max, ms, ls):
        k0 = pl.multiple_of(kt * KT, KT)
        off = (k0 - t0).astype(F32)
        new_m, new_l, alphas, ps = [], [], [], []
        for h in range(hpg):
            mh = jnp.maximum(ms[h], tile_max[h])
            alpha = jnp.exp2(ms[h] - mh)
            p = jnp.exp2(u_scr[slot, h] - (mh - slopes[h] * off))
            new_m.append(mh)
            new_l.append(alpha * ls[h] + jnp.sum(p, axis=0, keepdims=True))
            alphas.append(alpha)
            ps.append(p.astype(BF16))
        pv = lax.dot_general(vs_ref[pl.ds(k0, KT), :], jnp.concatenate(ps, axis=1), _TN,
                             preferred_element_type=F32)
        acc_slc[...] = acc_slc[...] * jnp.concatenate(alphas, axis=1) + pv
        return tuple(new_m), tuple(new_l)

    def slc_step(kt, carry):
        def visit(c):
            ms, ls, tile_max, prev, slot = c
            ms, ls = stage_b(prev, slot, tile_max, ms, ls)
            return ms, ls, stage_a(kt, 1 - slot), kt, 1 - slot
        return lax.cond(flag_ref[kt] > 0, visit, lambda c: c, carry)

    kd = t0 // KT
    acc_slc[...] = jnp.zeros_like(acc_slc)
    zero = jnp.int32(0)
    ms, ls, tile_max, prev, slot = lax.fori_loop(1, kd + 1, slc_step, init + (stage_a(zero, zero), zero, zero))
    _, l_slc = stage_b(prev, slot, tile_max, ms, ls)

    w0 = pl.multiple_of(jnp.maximum(t0 - WINDOW, 0), KB)
    diff_w = (t0 - w0) + rc_i
    base_w = jnp.where((diff_w >= 0) & (diff_w < WINDOW), 0.0, ninf)
    st_w = lax.dot_general(kw_ref[pl.ds(w0, WIN_KEYS), :], qs, _NT, preferred_element_type=F32)
    l_win, ps = [], []
    for h in range(hpg):
        u = st_w[:, h * QB:(h + 1) * QB] + (alibi_scr[h] + base_w)
        p = jnp.exp2(u - jnp.max(u, axis=0, keepdims=True))
        l_win.append(jnp.sum(p, axis=0, keepdims=True))
        ps.append(p.astype(BF16))
    acc_win[...] = lax.dot_general(vw_ref[pl.ds(w0, WIN_KEYS), :], jnp.concatenate(ps, axis=1), _TN,
                                   preferred_element_type=F32)

    for h in range(hpg):
        cols = slice(h * QB, (h + 1) * QB)
        g_cmp = jax.nn.sigmoid(gl_ref[0, h:h + 1, :])
        g_slc = jax.nn.sigmoid(gl_ref[1, h:h + 1, :])
        g_win = jax.nn.sigmoid(gl_ref[2, h:h + 1, :])
        oT = (g_cmp * ocmp_scr[:, cols] + (g_slc / l_slc[h]) * acc_slc[:, cols]
              + (g_win / l_win[h]) * acc_win[:, cols])
        o_ref[:, h * HEAD_DIM:(h + 1) * HEAD_DIM] = oT.T.astype(o_ref.dtype)


def sparse_attention(slopes, q, cmp_kv, kv_rest, gate_logits_t, *, nc):
    t, hd = q.shape
    _, n_groups, ncp, _ = cmp_kv.shape
    hpg = hd // HEAD_DIM // n_groups
    nb = t // SLC_BLOCK
    assert t % KT == 0 and t >= WIN_KEYS and ncp == 4 * nb and nb % SUBLANES == 0
    assert (CMP_BLOCK, CMP_STRIDE, SLC_BLOCK) == (32, 16, 64)
    n_sel = min(N_SELECT, nb)
    w = hpg * QB
    kern = functools.partial(_attn_kernel, hpg=hpg, nc=nc, ncp=ncp, nb=nb, n_sel=n_sel)
    kv_col = lambda part: (lambda g, i: (0, part * n_groups + g))
    return pl.pallas_call(
        kern,
        grid=(n_groups, t // QB),
        in_specs=[
            pl.BlockSpec(memory_space=pltpu.SMEM),
            pl.BlockSpec((QB, hpg * HEAD_DIM), lambda g, i: (i, g)),
            pl.BlockSpec((None, None, ncp, HEAD_DIM), lambda g, i: (0, g, 0, 0)),
            pl.BlockSpec((None, None, ncp, HEAD_DIM), lambda g, i: (1, g, 0, 0)),
            pl.BlockSpec((t, HEAD_DIM), kv_col(0)),
            pl.BlockSpec((t, HEAD_DIM), kv_col(1)),
            pl.BlockSpec((t, HEAD_DIM), kv_col(2)),
            pl.BlockSpec((t, HEAD_DIM), kv_col(3)),
            pl.BlockSpec((3, None, hpg, QB), lambda g, i: (0, g, 0, i)),
        ],
        out_specs=pl.BlockSpec((QB, hpg * HEAD_DIM), lambda g, i: (i, g)),
        out_shape=jax.ShapeDtypeStruct((t, hd), BF16),
        scratch_shapes=[
            pltpu.VMEM((SUBLANES + ncp, QB), F32),
            pltpu.VMEM((nb, QB), F32),
            pltpu.VMEM((nb, QB), F32),
            pltpu.VMEM((HEAD_DIM, w), F32),
            pltpu.VMEM((HEAD_DIM, w), F32),
            pltpu.VMEM((HEAD_DIM, w), F32),
            pltpu.VMEM((hpg, WIN_KEYS, QB), F32),
            pltpu.VMEM((2, hpg, KT, QB), F32),
            pltpu.SMEM((t // KT,), jnp.int32),
        ],
        name="sparse_attention",
        compiler_params=_params("parallel", "arbitrary"),
    )(slopes, q, cmp_kv, cmp_kv, kv_rest, kv_rest, kv_rest, kv_rest, gate_logits_t)


def _inproj_conv_kernel(a_ref, wb_ref, wc_ref, wh_ref, cw_ref, o_ref, halo_ref):
    i = pl.program_id(0)
    j = pl.program_id(1)
    a = a_ref[...]
    b = jnp.dot(a, wb_ref[...], preferred_element_type=F32)
    u = (jnp.dot(a, wc_ref[...], preferred_element_type=F32)
         * jnp.dot(a, wh_ref[...], preferred_element_type=F32))
    wk = cw_ref[...]
    w0, w1, w2 = wk[0:1, :], wk[1:2, :], wk[2:3, :]
    tm = u.shape[0]
    y = b * (w0 * pltpu.roll(u, 2, 0) + w1 * pltpu.roll(u, 1, 0) + w2 * u)
    o_ref[...] = y.astype(o_ref.dtype)
    @pl.when(i == 0)
    def _():
        halo_ref[j] = jnp.zeros(halo_ref.shape[1:], F32)

    ue = jnp.concatenate([halo_ref[j], u[0:SUBLANES, :]], axis=0)
    n2 = 2 * SUBLANES
    u1 = pltpu.roll(ue, 1, 0)[SUBLANES:n2, :]
    u2 = pltpu.roll(ue, 2, 0)[SUBLANES:n2, :]
    y0 = b[0:SUBLANES, :] * (w0 * u2 + w1 * u1 + w2 * u[0:SUBLANES, :])
    o_ref[0:SUBLANES, :] = y0.astype(o_ref.dtype)
    halo_ref[j] = u[tm - SUBLANES:tm, :]


def inproj_short_conv(a, w, conv_w, *, col0, cw, tm=1024, tn=512):
    m, k = a.shape
    tm, tn = _tile(m, tm), _tile(cw, tn)
    while col0 % tn:
        tn = _tile(cw, tn - 1)
    ncb = cw // tn
    jb = col0 // tn
    wcol = lambda part: (lambda i, j: (0, jb + part * ncb + j))
    return pl.pallas_call(
        _inproj_conv_kernel,
        grid=(m // tm, ncb),
        in_specs=[
            pl.BlockSpec((tm, k), lambda i, j: (i, 0)),
            pl.BlockSpec((k, tn), wcol(0)),
            pl.BlockSpec((k, tn), wcol(1)),
            pl.BlockSpec((k, tn), wcol(2)),
            pl.BlockSpec((CONV_K, tn), lambda i, j: (0, j)),
        ],
        out_specs=pl.BlockSpec((tm, tn), lambda i, j: (i, j)),
        out_shape=jax.ShapeDtypeStruct((m, cw), BF16),
        scratch_shapes=[pltpu.VMEM((ncb, SUBLANES, tn), F32)],
        name="inproj_short_conv",
        compiler_params=_params("arbitrary", "arbitrary"),
    )(a, w, w, w, conv_w)


def _mem_kernel(x_ref, gpre_ref, gpost_ref, wq_ref, km_ref, vm_ref, wo_ref, o_ref):
    x = x_ref[...]
    xn = _rms(x, gpre_ref[...]).astype(BF16)
    qm = jnp.dot(xn, wq_ref[...], preferred_element_type=F32) * (MEM_HEAD_DIM ** -0.5)
    outs = []
    for h in range(MEM_HEADS):
        cols = slice(h * MEM_HEAD_DIM, (h + 1) * MEM_HEAD_DIM)
        s = lax.dot_general(qm[:, cols].astype(BF16), km_ref[:, cols], _NT, preferred_element_type=F32)
        p = jnp.exp(s - jnp.max(s, axis=-1, keepdims=True))
        p = p / jnp.sum(p, axis=-1, keepdims=True)
        outs.append(jnp.dot(p.astype(BF16), vm_ref[:, cols], preferred_element_type=F32).astype(BF16))
    y = jnp.dot(jnp.concatenate(outs, axis=1), wo_ref[...], preferred_element_type=F32)
    o_ref[...] = x + _rms(y, gpost_ref[...])


def memory_sublayer(x, gpre, gpost, wq, km, vm, wo, tm=512):
    m, d = x.shape
    mw = wq.shape[1]
    nm = km.shape[0]
    tm = _tile(m, tm)
    full = lambda i: (0, 0)
    return pl.pallas_call(
        _mem_kernel,
        grid=(m // tm,),
        in_specs=[
            pl.BlockSpec((tm, d), lambda i: (i, 0)),
            pl.BlockSpec((1, d), full),
            pl.BlockSpec((1, d), full),
            pl.BlockSpec((d, mw), full),
            pl.BlockSpec((nm, mw), full),
            pl.BlockSpec((nm, mw), full),
            pl.BlockSpec((mw, d), full),
        ],
        out_specs=pl.BlockSpec((tm, d), lambda i: (i, 0)),
        out_shape=jax.ShapeDtypeStruct((m, d), F32),
        name="memory_sublayer",
        compiler_params=_params("parallel"),
    )(x, gpre.reshape(1, d), gpost.reshape(1, d), wq, km, vm, wo)


def _mlp_kernel(x_ref, gpre_ref, gpost_ref, wu_ref, wd_ref, o_ref, xn_scr, *, nf):
    f = pl.program_id(1)

    n_chunks = x_ref.shape[0] // NORM_ROWS

    @pl.when(f == 0)
    def _():
        def pre(r, carry):
            rows = pl.ds(pl.multiple_of(r * NORM_ROWS, NORM_ROWS), NORM_ROWS)
            xn_scr[rows, :] = _rms(x_ref[rows, :], gpre_ref[...]).astype(BF16)
            o_ref[rows, :] = jnp.zeros((NORM_ROWS, o_ref.shape[1]), F32)
            return carry
        lax.fori_loop(0, n_chunks, pre, 0)

    hid = jnp.dot(xn_scr[...], wu_ref[...], preferred_element_type=F32)
    hid = jnp.square(jnp.maximum(hid, 0.0))
    o_ref[...] += jnp.dot(hid.astype(BF16), wd_ref[...], preferred_element_type=F32)

    @pl.when(f == nf - 1)
    def _():
        def post(r, carry):
            rows = pl.ds(pl.multiple_of(r * NORM_ROWS, NORM_ROWS), NORM_ROWS)
            o_ref[rows, :] = x_ref[rows, :] + _rms(o_ref[rows, :], gpost_ref[...])
            return carry
        lax.fori_loop(0, n_chunks, post, 0)


def mlp_sublayer(x, gpre, gpost, w_up, w_down, tm=1024, tf=512):
    m, d = x.shape
    dff = w_up.shape[1]
    tm, tf = _tile(m, tm), _tile(dff, tf)
    nf = dff // tf
    w_up = w_up.astype(BF16)
    w_down = w_down.astype(BF16)
    return pl.pallas_call(
        functools.partial(_mlp_kernel, nf=nf),
        grid=(m // tm, nf),
        in_specs=[
            pl.BlockSpec((tm, d), lambda i, f: (i, 0), pipeline_mode=pl.Buffered(1)),
            pl.BlockSpec((1, d), lambda i, f: (0, 0)),
            pl.BlockSpec((1, d), lambda i, f: (0, 0)),
            pl.BlockSpec((d, tf), lambda i, f: (0, f)),
            pl.BlockSpec((tf, d), lambda i, f: (f, 0)),
        ],
        out_specs=pl.BlockSpec((tm, d), lambda i, f: (i, 0), pipeline_mode=pl.Buffered(1)),
        out_shape=jax.ShapeDtypeStruct((m, d), F32),
        scratch_shapes=[pltpu.VMEM((tm, d), BF16)],
        name="mlp_sublayer",
        compiler_params=pltpu.CompilerParams(dimension_semantics=("parallel", "arbitrary"),
                                             vmem_limit_bytes=MLP_VMEM_LIMIT),
    )(x, gpre.reshape(1, d), gpost.reshape(1, d), w_up, w_down)


def _mixer(x, norm_pre, norm_post, w_in, cmp_k_pe, cmp_k_w1, cmp_k_w2, cmp_v_pe, cmp_v_w1, cmp_v_w2,
           conv_w, w_out):
    t, d = x.shape
    attn_w = d // 2
    conv_cw = d - attn_w
    n_heads = attn_w // HEAD_DIM
    n_groups = n_heads // 4
    kvw = n_groups * HEAD_DIM
    o_kv = attn_w
    o_gate = o_kv + 6 * kvw
    o_conv = o_gate + 3 * n_heads
    assert w_in.shape[1] == o_conv + 3 * conv_cw

    assert o_gate % LANES == 0 and o_gate + LANES <= w_in.shape[1]
    w_bf = w_in.astype(BF16)
    w_conv = w_in[:, o_conv:].astype(BF16)

    xn = rmsnorm_cast(x, norm_pre)
    q = matmul(xn, w_bf, BF16, scale=HEAD_DIM ** -0.5 * LOG2E, col0=0, n=attn_w)
    kvc = matmul(xn, w_bf, F32, col0=o_kv, n=2 * kvw)
    kv_rest = matmul(xn, w_bf, BF16, col0=o_kv + 2 * kvw, n=4 * kvw)
    o_conv_out = inproj_short_conv(xn, w_conv, conv_w, col0=0, cw=conv_cw)
    gate_logits = matmul(xn, w_bf, F32, col0=o_gate, n=LANES)[:, :3 * n_heads]

    nc = (t - CMP_BLOCK) // CMP_STRIDE + 1
    pe = jnp.stack([cmp_k_pe, cmp_v_pe])
    w1 = jnp.stack([cmp_k_w1, cmp_v_w1]).astype(BF16)
    w2 = jnp.stack([cmp_k_w2, cmp_v_w2]).astype(BF16)
    cmp_kv = compress_tokens(kvc, pe, w1, w2, n_groups)

    gl_t = gate_logits.T.reshape(3, n_groups, n_heads // n_groups, t)
    idx = jnp.arange(1, n_heads + 1, dtype=F32)
    slopes = jnp.exp2(-8.0 * idx / n_heads) * LOG2E
    o_attn = sparse_attention(slopes, q, cmp_kv, kv_rest, gl_t, nc=nc)
    return matmul_postnorm_residual(o_attn, o_conv_out, w_out.astype(BF16), x, norm_post)


def _memory(x, mem, norm_pre, norm_kv, norm_post, wq, wk, wv, wo):
    mn = rmsnorm_cast(mem, norm_kv)
    km = matmul(mn, wk.astype(BF16), BF16)
    vm = matmul(mn, wv.astype(BF16), BF16)
    return memory_sublayer(x, norm_pre, norm_post, wq.astype(BF16), km, vm, wo.astype(BF16))


def kernel(x, mem, mix_norm_pre, mix_norm_post, w_in, cmp_k_pe, cmp_k_w1, cmp_k_w2, cmp_v_pe, cmp_v_w1,
           cmp_v_w2, conv_w, w_out, mem_norm_pre, mem_norm_kv, mem_norm_post, w_mem_q, w_mem_k, w_mem_v,
           w_mem_o, mlp_norm_pre, mlp_norm_post, w_up, w_down):
    b, t, d = x.shape
    assert b == 1
    h = x[0]
    m = mem[0]
    for l in range(w_in.shape[0]):
        h = _mixer(h, mix_norm_pre[l], mix_norm_post[l], w_in[l], cmp_k_pe[l], cmp_k_w1[l], cmp_k_w2[l],
                   cmp_v_pe[l], cmp_v_w1[l], cmp_v_w2[l], conv_w[l], w_out[l])
        h = _memory(h, m, mem_norm_pre[l], mem_norm_kv[l], mem_norm_post[l], w_mem_q[l], w_mem_k[l],
                    w_mem_v[l], w_mem_o[l])
        h = mlp_sublayer(h, mlp_norm_pre[l], mlp_norm_post[l], w_up[l], w_down[l])
    return h[None]
```

```python
import functools

import jax
import jax.numpy as jnp
from jax import lax
from jax.experimental import pallas as pl
from jax.experimental.pallas import tpu as pltpu

HEAD_DIM = 128
CMP_BLOCK = 32
CMP_STRIDE = 16
SLC_BLOCK = 64
N_SELECT = 16
N_LOCAL_FORCED = 2
WINDOW = 512
CONV_K = 3
MEM_HEADS = 4
MEM_HEAD_DIM = 128
RMS_EPS = 1e-6
NEG_INF = -1e30
FORCED_SCORE = 1e9
LOG2E = 1.4426950408889634

V7X_VMEM_BYTES = 64 * 1024 * 1024
VMEM_LIMIT = V7X_VMEM_BYTES - 8 * 1024 * 1024
MLP_VMEM_LIMIT = V7X_VMEM_BYTES - 4 * 1024 * 1024
NORM_ROWS = 64
LANES = 128
SUBLANES = 8

QB = 128
KB = 128
KT = 512
CMP_ROWS = 128
WIN_KEYS = WINDOW + QB

F32 = jnp.float32
BF16 = jnp.bfloat16

_NT = (((1,), (1,)), ((), ()))
_TN = (((0,), (0,)), ((), ()))


def _tile(n, pref):
    t = min(n, pref)
    while n % t:
        t -= 1
    return t


def _params(*sem):
    return pltpu.CompilerParams(dimension_semantics=sem, vmem_limit_bytes=VMEM_LIMIT)


def _rms(x, gain):
    return x * lax.rsqrt(jnp.mean(x * x, axis=-1, keepdims=True) + RMS_EPS) * gain


def _rmsnorm_kernel(x_ref, g_ref, o_ref):
    o_ref[...] = _rms(x_ref[...].astype(F32), g_ref[...]).astype(o_ref.dtype)


def rmsnorm_cast(x, gain, out_dtype=BF16):
    m, d = x.shape
    tm = _tile(m, 256)
    return pl.pallas_call(
        _rmsnorm_kernel,
        grid=(m // tm,),
        in_specs=[pl.BlockSpec((tm, d), lambda i: (i, 0)), pl.BlockSpec((1, d), lambda i: (0, 0))],
        out_specs=pl.BlockSpec((tm, d), lambda i: (i, 0)),
        out_shape=jax.ShapeDtypeStruct((m, d), out_dtype),
        name="rmsnorm_cast",
        compiler_params=_params("parallel"),
    )(x, gain.reshape(1, d))


def _mm_kernel(a_ref, w_ref, o_ref, *, scale):
    acc = jnp.dot(a_ref[...], w_ref[...], preferred_element_type=F32)
    if scale is not None:
        acc = acc * scale
    o_ref[...] = acc.astype(o_ref.dtype)


def matmul(a, w, out_dtype, scale=None, col0=0, n=None, tm=1024, tn=1024):
    m, k = a.shape
    n = w.shape[1] - col0 if n is None else n
    tm, tn = _tile(m, tm), _tile(n, tn)
    while col0 % tn:
        tn = _tile(n, tn - 1)
    jb = col0 // tn
    return pl.pallas_call(
        functools.partial(_mm_kernel, scale=scale),
        grid=(m // tm, n // tn),
        in_specs=[pl.BlockSpec((tm, k), lambda i, j: (i, 0)), pl.BlockSpec((k, tn), lambda i, j: (0, j + jb))],
        out_specs=pl.BlockSpec((tm, tn), lambda i, j: (i, j)),
        out_shape=jax.ShapeDtypeStruct((m, n), out_dtype),
        name="matmul",
        compiler_params=_params("parallel", "arbitrary"),
    )(a, w)


def _mm_postnorm_kernel(a1_ref, a2_ref, w_ref, x_ref, g_ref, o_ref, *, nk1, nk):
    k = pl.program_id(1)

    @pl.when(k == 0)
    def _():
        o_ref[...] = jnp.zeros_like(o_ref)

    @pl.when(k < nk1)
    def _():
        o_ref[...] += jnp.dot(a1_ref[...], w_ref[...], preferred_element_type=F32)

    @pl.when(k >= nk1)
    def _():
        o_ref[...] += jnp.dot(a2_ref[...], w_ref[...], preferred_element_type=F32)

    @pl.when(k == nk - 1)
    def _():
        def post(r, carry):
            rows = pl.ds(pl.multiple_of(r * NORM_ROWS, NORM_ROWS), NORM_ROWS)
            o_ref[rows, :] = x_ref[rows, :] + _rms(o_ref[rows, :], g_ref[...])
            return carry
        lax.fori_loop(0, o_ref.shape[0] // NORM_ROWS, post, 0)


def matmul_postnorm_residual(a1, a2, w, x, gain, tm=512, tk=1024):
    m, k1 = a1.shape
    k2 = a2.shape[1]
    _, n = w.shape
    tm = _tile(m, tm)
    tk = _tile(k1, tk)
    while k2 % tk:
        tk = _tile(k1, tk - 1)
    nk1, nk = k1 // tk, (k1 + k2) // tk
    return pl.pallas_call(
        functools.partial(_mm_postnorm_kernel, nk1=nk1, nk=nk),
        grid=(m // tm, nk),
        in_specs=[
            pl.BlockSpec((tm, tk), lambda i, k: (i, jnp.minimum(k, nk1 - 1))),
            pl.BlockSpec((tm, tk), lambda i, k: (i, jnp.maximum(k - nk1, 0))),
            pl.BlockSpec((tk, n), lambda i, k: (k, 0)),
            pl.BlockSpec((tm, n), lambda i, k: (i, 0)),
            pl.BlockSpec((1, n), lambda i, k: (0, 0)),
        ],
        out_specs=pl.BlockSpec((tm, n), lambda i, k: (i, 0)),
        out_shape=jax.ShapeDtypeStruct((m, n), F32),
        name="outproj_postnorm",
        compiler_params=_params("parallel", "arbitrary"),
    )(a1, a2, w, x, gain.reshape(1, n))


def _gelu_tanh(x):
    c = 0.7978845608028654
    return x * (0.5 * (1.0 + jnp.tanh(c * (x + 0.044715 * (x * x * x)))))


def _compress_kernel(x_ref, pe_ref, w1_ref, w2_ref, o_ref, *, ncp):
    a = jnp.zeros((ncp, HEAD_DIM), F32)
    b = jnp.zeros((ncp, HEAD_DIM), F32)
    for l in range(CMP_STRIDE):
        xl = x_ref[pl.ds(l, ncp, stride=CMP_STRIDE), :]
        lo, hi = l, CMP_STRIDE + l
        a = a + jnp.dot((xl + pe_ref[lo:lo + 1, :]).astype(BF16), w1_ref[lo * HEAD_DIM:(lo + 1) * HEAD_DIM, :],
                        preferred_element_type=F32)
        b = b + jnp.dot((xl + pe_ref[hi:hi + 1, :]).astype(BF16), w1_ref[hi * HEAD_DIM:(hi + 1) * HEAD_DIM, :],
                        preferred_element_type=F32)
    hid = _gelu_tanh(a + pltpu.roll(b, ncp - 1, 0))
    o_ref[...] = jnp.dot(hid.astype(BF16), w2_ref[...], preferred_element_type=F32).astype(o_ref.dtype)


def compress_tokens(kvc, pe, w1, w2, n_groups):
    t = kvc.shape[0]
    ncp = t // CMP_STRIDE
    assert CMP_BLOCK == 2 * CMP_STRIDE
    return pl.pallas_call(
        functools.partial(_compress_kernel, ncp=ncp),
        grid=(2, n_groups),
        in_specs=[
            pl.BlockSpec((t, HEAD_DIM), lambda s, j: (0, s * n_groups + j)),
            pl.BlockSpec((None, CMP_BLOCK, HEAD_DIM), lambda s, j: (s, 0, 0)),
            pl.BlockSpec((None, CMP_BLOCK * HEAD_DIM, HEAD_DIM), lambda s, j: (s, 0, 0)),
            pl.BlockSpec((None, HEAD_DIM, HEAD_DIM), lambda s, j: (s, 0, 0)),
        ],
        out_specs=pl.BlockSpec((None, None, ncp, HEAD_DIM), lambda s, j: (s, j, 0, 0)),
        out_shape=jax.ShapeDtypeStruct((2, n_groups, ncp, HEAD_DIM), BF16),
        name="compress_tokens",
        compiler_params=_params("parallel", "parallel"),
    )(kvc, pe, w1, w2)


def _attn_kernel(slopes_ref, q_ref, kcmp_ref, vcmp_ref, ks_ref, vs_ref, kw_ref, vw_ref, gl_ref,
                 o_ref, p_scr, imp_scr, sel_scr, ocmp_scr, acc_slc, acc_win, alibi_scr, u_scr, flag_ref,
                 *, hpg, nc, ncp, nb, n_sel):
    g = pl.program_id(0)
    i = pl.program_id(1)
    t0 = i * QB
    w = hpg * QB
    ninf = -jnp.inf

    q = q_ref[...]
    qs = jnp.concatenate([q[:, h * HEAD_DIM:(h + 1) * HEAD_DIM] for h in range(hpg)], axis=0)
    slopes = [slopes_ref[g * hpg + h] for h in range(hpg)]

    rc_i = (lax.broadcasted_iota(jnp.int32, (WIN_KEYS, QB), 1)
            - lax.broadcasted_iota(jnp.int32, (WIN_KEYS, QB), 0))

    @pl.when(i == 0)
    def _():
        rc_f = (-rc_i).astype(F32)
        for h in range(hpg):
            alibi_scr[h] = slopes[h] * rc_f

    def cmp_branch(rows):
        s = lax.dot_general(kcmp_ref[0:rows, :], qs, _NT, preferred_element_type=F32)
        c_iota = lax.broadcasted_iota(jnp.int32, (rows, QB), 0)
        q_iota = lax.broadcasted_iota(jnp.int32, (rows, QB), 1)
        cmp_end = c_iota * CMP_STRIDE + (CMP_BLOCK - 1)
        mask_c = (cmp_end <= t0 + q_iota) & (c_iota < nc)
        rel_c = (cmp_end - t0).astype(F32)
        psum = jnp.zeros((rows, QB), F32)
        for h in range(hpg):
            sh = s[:, h * QB:(h + 1) * QB] + slopes[h] * rel_c
            sh = jnp.where(mask_c, sh, ninf)
            m = jnp.maximum(jnp.max(sh, axis=0, keepdims=True), NEG_INF)
            p = jnp.exp2(sh - m)
            l = jnp.sum(p, axis=0, keepdims=True)
            pn = p * jnp.where(l > 0.0, 1.0 / l, 0.0)
            psum = psum + pn
            ocmp_scr[:, h * QB:(h + 1) * QB] = lax.dot_general(vcmp_ref[0:rows, :], pn.astype(BF16), _TN,
                                                                preferred_element_type=F32)
        p_scr[SUBLANES:SUBLANES + rows, :] = psum
        if rows < ncp:
            p_scr[SUBLANES + rows:SUBLANES + ncp, :] = jnp.zeros((ncp - rows, QB), F32)

    n_ended = (t0 + QB - CMP_BLOCK) // CMP_STRIDE + 1
    variant = (n_ended - 1) // CMP_ROWS

    def cmp_dispatch(k):
        rows = (k + 1) * CMP_ROWS
        if rows >= ncp:
            cmp_branch(ncp)
        else:
            lax.cond(variant <= k, lambda: cmp_branch(rows), lambda: cmp_dispatch(k + 1))

    cmp_dispatch(0)

    p_scr[0:SUBLANES, :] = jnp.zeros((SUBLANES, QB), F32)
    imp = p_scr[pl.ds(SUBLANES - 1, nb, stride=4), :]
    for j in range(4):
        imp = imp + p_scr[pl.ds(SUBLANES + j, nb, stride=4), :]
    n_iota = lax.broadcasted_iota(jnp.int32, (nb, QB), 0)
    tq = t0 + lax.broadcasted_iota(jnp.int32, (nb, QB), 1)
    cur = lax.shift_right_logical(tq, 6)
    valid = n_iota <= cur
    back = cur - n_iota
    forced = (n_iota == 0) | ((back >= 0) & (back < N_LOCAL_FORCED))
    imp = jnp.where(forced & valid, FORCED_SCORE, imp)
    imp = jnp.where(valid, imp, NEG_INF)
    imp_scr[...] = imp

    n_grp = nb // SUBLANES
    vals = [imp[j * SUBLANES:(j + 1) * SUBLANES, :] for j in range(n_grp)]
    row8 = lax.broadcasted_iota(jnp.int32, (SUBLANES, QB), 0)

    def count(cnts, mi, j):
        vm = jnp.broadcast_to(imp_scr[pl.ds(mi, 1), :], (SUBLANES, QB))
        lo = j * SUBLANES
        if lo + SUBLANES - 1 < mi:
            beats = vm > vals[j]
        elif lo > mi:
            beats = vm >= vals[j]
        else:
            beats = (vm > vals[j]) | ((vm == vals[j]) & (row8 > mi - lo))
        cnts[j] = cnts[j] + jnp.where(beats, 1.0, 0.0)

    def rank_shell(sh, cnts):
        cnts = list(cnts)
        for mi in range(sh * SUBLANES, (sh + 1) * SUBLANES):
            for j in range(sh + 1):
                count(cnts, mi, j)
        for mi in range(sh * SUBLANES):
            count(cnts, mi, sh)
        return tuple(cnts)

    last_valid = (t0 + QB - 1) // SLC_BLOCK
    cnts = tuple(jnp.zeros((SUBLANES, QB), F32) for _ in range(n_grp))
    for sh in range(n_grp):
        cnts = lax.cond(sh * SUBLANES <= last_valid, functools.partial(rank_shell, sh), lambda c: c, cnts)
    cnt = jnp.concatenate(cnts, axis=0)
    sel = (cnt < float(n_sel)) & valid
    sel_scr[...] = jnp.where(sel, 0.0, ninf)
    blocks_per_kt = KT // SLC_BLOCK
    sel_f = jnp.where(sel, 1.0, 0.0)
    for kt in range(nb // blocks_per_kt):
        any_sel = jnp.max(sel_f[kt * blocks_per_kt:(kt + 1) * blocks_per_kt, :])
        flag_ref[kt] = (any_sel > 0.5).astype(jnp.int32)

    init = (tuple(jnp.full((1, QB), NEG_INF, F32) for _ in range(hpg)),
            tuple(jnp.zeros((1, QB), F32) for _ in range(hpg)))

    def stage_a(kt, slot):
        k0 = pl.multiple_of(kt * KT, KT)
        st = lax.dot_general(ks_ref[pl.ds(k0, KT), :], qs, _NT, preferred_element_type=F32)
        rows = [jnp.broadcast_to(sel_scr[pl.ds(kt * blocks_per_kt + b, 1), :], (SLC_BLOCK, QB))
                for b in range(blocks_per_kt)]
        diff = (t0 - k0) + rc_i[0:KT, :]
        base = jnp.where(diff >= 0, jnp.concatenate(rows, axis=0), ninf)
        off = (k0 - t0).astype(F32)
        tile_max = []
        for h in range(hpg):
            u = st[:, h * QB:(h + 1) * QB] + (alibi_scr[h, 0:KT, :] + base)
            u_scr[slot, h] = u
            tile_max.append(jnp.max(u, axis=0, keepdims=True) + slopes[h] * off)
        return tuple(tile_max)

    def stage_b(kt, slot, tile_max, ms, ls):
        k0 = pl.multiple_of(kt * KT, KT)
        off = (k0 - t0).astype(F32)
        new_m, new_l, alphas, ps = [], [], [], []
        for h in range(hpg):
            mh = jnp.maximum(ms[h], tile_max[h])
            alpha = jnp.exp2(ms[h] - mh)
            p = jnp.exp2(u_scr[slot, h] - (mh - slopes[h] * off))
            new_m.append(mh)
            new_l.append(alpha * ls[h] + jnp.sum(p, axis=0, keepdims=True))
            alphas.append(alpha)
            ps.append(p.astype(BF16))
        pv = lax.dot_general(vs_ref[pl.ds(k0, KT), :], jnp.concatenate(ps, axis=1), _TN,
                             preferred_element_type=F32)
        acc_slc[...] = acc_slc[...] * jnp.concatenate(alphas, axis=1) + pv
        return tuple(new_m), tuple(new_l)

    def slc_step(kt, carry):
        def visit(c):
            ms, ls, tile_max, prev, slot = c
            ms, ls = stage_b(prev, slot, tile_max, ms, ls)
            return ms, ls, stage_a(kt, 1 - slot), kt, 1 - slot
        return lax.cond(flag_ref[kt] > 0, visit, lambda c: c, carry)

    kd = t0 // KT
    acc_slc[...] = jnp.zeros_like(acc_slc)
    zero = jnp.int32(0)
    ms, ls, tile_max, prev, slot = lax.fori_loop(1, kd + 1, slc_step, init + (stage_a(zero, zero), zero, zero))
    _, l_slc = stage_b(prev, slot, tile_max, ms, ls)

    w0 = pl.multiple_of(jnp.maximum(t0 - WINDOW, 0), KB)
    diff_w = (t0 - w0) + rc_i
    base_w = jnp.where((diff_w >= 0) & (diff_w < WINDOW), 0.0, ninf)
    st_w = lax.dot_general(kw_ref[pl.ds(w0, WIN_KEYS), :], qs, _NT, preferred_element_type=F32)
    l_win, ps = [], []
    for h in range(hpg):
        u = st_w[:, h * QB:(h + 1) * QB] + (alibi_scr[h] + base_w)
        p = jnp.exp2(u - jnp.max(u, axis=0, keepdims=True))
        l_win.append(jnp.sum(p, axis=0, keepdims=True))
        ps.append(p.astype(BF16))
    acc_win[...] = lax.dot_general(vw_ref[pl.ds(w0, WIN_KEYS), :], jnp.concatenate(ps, axis=1), _TN,
                                   preferred_element_type=F32)

    for h in range(hpg):
        cols = slice(h * QB, (h + 1) * QB)
        g_cmp = jax.nn.sigmoid(gl_ref[0, h:h + 1, :])
        g_slc = jax.nn.sigmoid(gl_ref[1, h:h + 1, :])
        g_win = jax.nn.sigmoid(gl_ref[2, h:h + 1, :])
        oT = (g_cmp * ocmp_scr[:, cols] + (g_slc / l_slc[h]) * acc_slc[:, cols]
              + (g_win / l_win[h]) * acc_win[:, cols])
        o_ref[:, h * HEAD_DIM:(h + 1) * HEAD_DIM] = oT.T.astype(o_ref.dtype)


def sparse_attention(slopes, q, cmp_kv, kv_rest, gate_logits_t, *, nc):
    t, hd = q.shape
    _, n_groups, ncp, _ = cmp_kv.shape
    hpg = hd // HEAD_DIM // n_groups
    nb = t // SLC_BLOCK
    assert t % KT == 0 and t >= WIN_KEYS and ncp == 4 * nb and nb % SUBLANES == 0
    assert (CMP_BLOCK, CMP_STRIDE, SLC_BLOCK) == (32, 16, 64)
    n_sel = min(N_SELECT, nb)
    w = hpg * QB
    kern = functools.partial(_attn_kernel, hpg=hpg, nc=nc, ncp=ncp, nb=nb, n_sel=n_sel)
    kv_col = lambda part: (lambda g, i: (0, part * n_groups + g))
    return pl.pallas_call(
        kern,
        grid=(n_groups, t // QB),
        in_specs=[
            pl.BlockSpec(memory_space=pltpu.SMEM),
            pl.BlockSpec((QB, hpg * HEAD_DIM), lambda g, i: (i, g)),
            pl.BlockSpec((None, None, ncp, HEAD_DIM), lambda g, i: (0, g, 0, 0)),
            pl.BlockSpec((None, None, ncp, HEAD_DIM), lambda g, i: (1, g, 0, 0)),
            pl.BlockSpec((t, HEAD_DIM), kv_col(0)),
            pl.BlockSpec((t, HEAD_DIM), kv_col(1)),
            pl.BlockSpec((t, HEAD_DIM), kv_col(2)),
            pl.BlockSpec((t, HEAD_DIM), kv_col(3)),
            pl.BlockSpec((3, None, hpg, QB), lambda g, i: (0, g, 0, i)),
        ],
        out_specs=pl.BlockSpec((QB, hpg * HEAD_DIM), lambda g, i: (i, g)),
        out_shape=jax.ShapeDtypeStruct((t, hd), BF16),
        scratch_shapes=[
            pltpu.VMEM((SUBLANES + ncp, QB), F32),
            pltpu.VMEM((nb, QB), F32),
            pltpu.VMEM((nb, QB), F32),
            pltpu.VMEM((HEAD_DIM, w), F32),
            pltpu.VMEM((HEAD_DIM, w), F32),
            pltpu.VMEM((HEAD_DIM, w), F32),
            pltpu.VMEM((hpg, WIN_KEYS, QB), F32),
            pltpu.VMEM((2, hpg, KT, QB), F32),
            pltpu.SMEM((t // KT,), jnp.int32),
        ],
        name="sparse_attention",
        compiler_params=_params("parallel", "arbitrary"),
    )(slopes, q, cmp_kv, cmp_kv, kv_rest, kv_rest, kv_rest, kv_rest, gate_logits_t)


def _inproj_conv_kernel(a_ref, wb_ref, wc_ref, wh_ref, cw_ref, o_ref, halo_ref):
    i = pl.program_id(0)
    j = pl.program_id(1)
    a = a_ref[...]
    b = jnp.dot(a, wb_ref[...], preferred_element_type=F32)
    u = (jnp.dot(a, wc_ref[...], preferred_element_type=F32)
         * jnp.dot(a, wh_ref[...], preferred_element_type=F32))
    wk = cw_ref[...]
    w0, w1, w2 = wk[0:1, :], wk[1:2, :], wk[2:3, :]
    tm = u.shape[0]
    y = b * (w0 * pltpu.roll(u, 2, 0) + w1 * pltpu.roll(u, 1, 0) + w2 * u)
    o_ref[...] = y.astype(o_ref.dtype)
    @pl.when(i == 0)
    def _():
        halo_ref[j] = jnp.zeros(halo_ref.shape[1:], F32)

    ue = jnp.concatenate([halo_ref[j], u[0:SUBLANES, :]], axis=0)
    n2 = 2 * SUBLANES
    u1 = pltpu.roll(ue, 1, 0)[SUBLANES:n2, :]
    u2 = pltpu.roll(ue, 2, 0)[SUBLANES:n2, :]
    y0 = b[0:SUBLANES, :] * (w0 * u2 + w1 * u1 + w2 * u[0:SUBLANES, :])
    o_ref[0:SUBLANES, :] = y0.astype(o_ref.dtype)
    halo_ref[j] = u[tm - SUBLANES:tm, :]


def inproj_short_conv(a, w, conv_w, *, col0, cw, tm=1024, tn=512):
    m, k = a.shape
    tm, tn = _tile(m, tm), _tile(cw, tn)
    while col0 % tn:
        tn = _tile(cw, tn - 1)
    ncb = cw // tn
    jb = col0 // tn
    wcol = lambda part: (lambda i, j: (0, jb + part * ncb + j))
    return pl.pallas_call(
        _inproj_conv_kernel,
        grid=(m // tm, ncb),
        in_specs=[
            pl.BlockSpec((tm, k), lambda i, j: (i, 0)),
            pl.BlockSpec((k, tn), wcol(0)),
            pl.BlockSpec((k, tn), wcol(1)),
            pl.BlockSpec((k, tn), wcol(2)),
            pl.BlockSpec((CONV_K, tn), lambda i, j: (0, j)),
        ],
        out_specs=pl.BlockSpec((tm, tn), lambda i, j: (i, j)),
        out_shape=jax.ShapeDtypeStruct((m, cw), BF16),
        scratch_shapes=[pltpu.VMEM((ncb, SUBLANES, tn), F32)],
        name="inproj_short_conv",
        compiler_params=_params("arbitrary", "arbitrary"),
    )(a, w, w, w, conv_w)


def _mem_kernel(x_ref, gpre_ref, gpost_ref, wq_ref, km_ref, vm_ref, wo_ref, o_ref):
    x = x_ref[...]
    xn = _rms(x, gpre_ref[...]).astype(BF16)
    qm = jnp.dot(xn, wq_ref[...], preferred_element_type=F32) * (MEM_HEAD_DIM ** -0.5)
    outs = []
    for h in range(MEM_HEADS):
        cols = slice(h * MEM_HEAD_DIM, (h + 1) * MEM_HEAD_DIM)
        s = lax.dot_general(qm[:, cols].astype(BF16), km_ref[:, cols], _NT, preferred_element_type=F32)
        p = jnp.exp(s - jnp.max(s, axis=-1, keepdims=True))
        p = p / jnp.sum(p, axis=-1, keepdims=True)
        outs.append(jnp.dot(p.astype(BF16), vm_ref[:, cols], preferred_element_type=F32).astype(BF16))
    y = jnp.dot(jnp.concatenate(outs, axis=1), wo_ref[...], preferred_element_type=F32)
    o_ref[...] = x + _rms(y, gpost_ref[...])


def memory_sublayer(x, gpre, gpost, wq, km, vm, wo, tm=512):
    m, d = x.shape
    mw = wq.shape[1]
    nm = km.shape[0]
    tm = _tile(m, tm)
    full = lambda i: (0, 0)
    return pl.pallas_call(
        _mem_kernel,
        grid=(m // tm,),
        in_specs=[
            pl.BlockSpec((tm, d), lambda i: (i, 0)),
            pl.BlockSpec((1, d), full),
            pl.BlockSpec((1, d), full),
            pl.BlockSpec((d, mw), full),
            pl.BlockSpec((nm, mw), full),
            pl.BlockSpec((nm, mw), full),
            pl.BlockSpec((mw, d), full),
        ],
        out_specs=pl.BlockSpec((tm, d), lambda i: (i, 0)),
        out_shape=jax.ShapeDtypeStruct((m, d), F32),
        name="memory_sublayer",
        compiler_params=_params("parallel"),
    )(x, gpre.reshape(1, d), gpost.reshape(1, d), wq, km, vm, wo)


def _mlp_kernel(x_ref, gpre_ref, gpost_ref, wu_ref, wd_ref, o_ref, xn_scr, *, nf):
    f = pl.program_id(1)

    n_chunks = x_ref.shape[0] // NORM_ROWS

    @pl.when(f == 0)
    def _():
        def pre(r, carry):
            rows = pl.ds(pl.multiple_of(r * NORM_ROWS, NORM_ROWS), NORM_ROWS)
            xn_scr[rows, :] = _rms(x_ref[rows, :], gpre_ref[...]).astype(BF16)
            o_ref[rows, :] = jnp.zeros((NORM_ROWS, o_ref.shape[1]), F32)
            return carry
        lax.fori_loop(0, n_chunks, pre, 0)

    hid = jnp.dot(xn_scr[...], wu_ref[...], preferred_element_type=F32)
    hid = jnp.square(jnp.maximum(hid, 0.0))
    o_ref[...] += jnp.dot(hid.astype(BF16), wd_ref[...], preferred_element_type=F32)

    @pl.when(f == nf - 1)
    def _():
        def post(r, carry):
            rows = pl.ds(pl.multiple_of(r * NORM_ROWS, NORM_ROWS), NORM_ROWS)
            o_ref[rows, :] = x_ref[rows, :] + _rms(o_ref[rows, :], gpost_ref[...])
            return carry
        lax.fori_loop(0, n_chunks, post, 0)


def mlp_sublayer(x, gpre, gpost, w_up, w_down, tm=1024, tf=512):
    m, d = x.shape
    dff = w_up.shape[1]
    tm, tf = _tile(m, tm), _tile(dff, tf)
    nf = dff // tf
    w_up = w_up.astype(BF16)
    w_down = w_down.astype(BF16)
    return pl.pallas_call(
        functools.partial(_mlp_kernel, nf=nf),
        grid=(m // tm, nf),
        in_specs=[
            pl.BlockSpec((tm, d), lambda i, f: (i, 0), pipeline_mode=pl.Buffered(1)),
            pl.BlockSpec((1, d), lambda i, f: (0, 0)),
            pl.BlockSpec((1, d), lambda i, f: (0, 0)),
            pl.BlockSpec((d, tf), lambda i, f: (0, f)),
            pl.BlockSpec((tf, d), lambda i, f: (f, 0)),
        ],
        out_specs=pl.BlockSpec((tm, d), lambda i, f: (i, 0), pipeline_mode=pl.Buffered(1)),
        out_shape=jax.ShapeDtypeStruct((m, d), F32),
        scratch_shapes=[pltpu.VMEM((tm, d), BF16)],
        name="mlp_sublayer",
        compiler_params=pltpu.CompilerParams(dimension_semantics=("parallel", "arbitrary"),
                                             vmem_limit_bytes=MLP_VMEM_LIMIT),
    )(x, gpre.reshape(1, d), gpost.reshape(1, d), w_up, w_down)


def _mixer(x, norm_pre, norm_post, w_in, cmp_k_pe, cmp_k_w1, cmp_k_w2, cmp_v_pe, cmp_v_w1, cmp_v_w2,
           conv_w, w_out):
    t, d = x.shape
    attn_w = d // 2
    conv_cw = d - attn_w
    n_heads = attn_w // HEAD_DIM
    n_groups = n_heads // 4
    kvw = n_groups * HEAD_DIM
    o_kv = attn_w
    o_gate = o_kv + 6 * kvw
    o_conv = o_gate + 3 * n_heads
    assert w_in.shape[1] == o_conv + 3 * conv_cw

    assert o_gate % LANES == 0 and o_gate + LANES <= w_in.shape[1]
    w_bf = w_in.astype(BF16)
    w_conv = w_in[:, o_conv:].astype(BF16)

    xn = rmsnorm_cast(x, norm_pre)
    q = matmul(xn, w_bf, BF16, scale=HEAD_DIM ** -0.5 * LOG2E, col0=0, n=attn_w)
    kvc = matmul(xn, w_bf, F32, col0=o_kv, n=2 * kvw)
    kv_rest = matmul(xn, w_bf, BF16, col0=o_kv + 2 * kvw, n=4 * kvw)
    o_conv_out = inproj_short_conv(xn, w_conv, conv_w, col0=0, cw=conv_cw)
    gate_logits = matmul(xn, w_bf, F32, col0=o_gate, n=LANES)[:, :3 * n_heads]

    nc = (t - CMP_BLOCK) // CMP_STRIDE + 1
    pe = jnp.stack([cmp_k_pe, cmp_v_pe])
    w1 = jnp.stack([cmp_k_w1, cmp_v_w1]).astype(BF16)
    w2 = jnp.stack([cmp_k_w2, cmp_v_w2]).astype(BF16)
    cmp_kv = compress_tokens(kvc, pe, w1, w2, n_groups)

    gl_t = gate_logits.T.reshape(3, n_groups, n_heads // n_groups, t)
    idx = jnp.arange(1, n_heads + 1, dtype=F32)
    slopes = jnp.exp2(-8.0 * idx / n_heads) * LOG2E
    o_attn = sparse_attention(slopes, q, cmp_kv, kv_rest, gl_t, nc=nc)
    return matmul_postnorm_residual(o_attn, o_conv_out, w_out.astype(BF16), x, norm_post)


def _memory(x, mem, norm_pre, norm_kv, norm_post, wq, wk, wv, wo):
    mn = rmsnorm_cast(mem, norm_kv)
    km = matmul(mn, wk.astype(BF16), BF16)
    vm = matmul(mn, wv.astype(BF16), BF16)
    return memory_sublayer(x, norm_pre, norm_post, wq.astype(BF16), km, vm, wo.astype(BF16))


def kernel(x, mem, mix_norm_pre, mix_norm_post, w_in, cmp_k_pe, cmp_k_w1, cmp_k_w2, cmp_v_pe, cmp_v_w1,
           cmp_v_w2, conv_w, w_out, mem_norm_pre, mem_norm_kv, mem_norm_post, w_mem_q, w_mem_k, w_mem_v,
           w_mem_o, mlp_norm_pre, mlp_norm_post, w_up, w_down):
    b, t, d = x.shape
    assert b == 1
    h = x[0]
    m = mem[0]
    for l in range(w_in.shape[0]):
        h = _mixer(h, mix_norm_pre[l], mix_norm_post[l], w_in[l], cmp_k_pe[l], cmp_k_w1[l], cmp_k_w2[l],
                   cmp_v_pe[l], cmp_v_w1[l], cmp_v_w2[l], conv_w[l], w_out[l])
        h = _memory(h, m, mem_norm_pre[l], mem_norm_kv[l], mem_norm_post[l], w_mem_q[l], w_mem_k[l],
                    w_mem_v[l], w_mem_o[l])
        h = mlp_sublayer(h, mlp_norm_pre[l], mlp_norm_post[l], w_up[l], w_down[l])
    return h[None]
```

```python
import functools

import jax
import jax.numpy as jnp
from jax import lax
from jax.experimental import pallas as pl
from jax.experimental.pallas import tpu as pltpu

HEAD_DIM = 128
CMP_BLOCK = 32
CMP_STRIDE = 16
SLC_BLOCK = 64
N_SELECT = 16
N_LOCAL_FORCED = 2
WINDOW = 512
CONV_K = 3
MEM_HEADS = 4
MEM_HEAD_DIM = 128
RMS_EPS = 1e-6
NEG_INF = -1e30
FORCED_SCORE = 1e9
LOG2E = 1.4426950408889634

V7X_VMEM_BYTES = 64 * 1024 * 1024
VMEM_LIMIT = V7X_VMEM_BYTES - 8 * 1024 * 1024
MLP_VMEM_LIMIT = V7X_VMEM_BYTES - 4 * 1024 * 1024
NORM_ROWS = 64
LANES = 128
SUBLANES = 8

QB = 256
KB = 128
KT = 512
CMP_ROWS = 128
WIN_KEYS = WINDOW + QB

F32 = jnp.float32
BF16 = jnp.bfloat16

_NT = (((1,), (1,)), ((), ()))
_TN = (((0,), (0,)), ((), ()))


def _tile(n, pref):
    t = min(n, pref)
    while n % t:
        t -= 1
    return t


def _params(*sem):
    return pltpu.CompilerParams(dimension_semantics=sem, vmem_limit_bytes=VMEM_LIMIT)


def _rms(x, gain):
    return x * lax.rsqrt(jnp.mean(x * x, axis=-1, keepdims=True) + RMS_EPS) * gain


def _rmsnorm_kernel(x_ref, g_ref, o_ref):
    o_ref[...] = _rms(x_ref[...].astype(F32), g_ref[...]).astype(o_ref.dtype)


def rmsnorm_cast(x, gain, out_dtype=BF16):
    m, d = x.shape
    tm = _tile(m, 256)
    return pl.pallas_call(
        _rmsnorm_kernel,
        grid=(m // tm,),
        in_specs=[pl.BlockSpec((tm, d), lambda i: (i, 0)), pl.BlockSpec((1, d), lambda i: (0, 0))],
        out_specs=pl.BlockSpec((tm, d), lambda i: (i, 0)),
        out_shape=jax.ShapeDtypeStruct((m, d), out_dtype),
        name="rmsnorm_cast",
        compiler_params=_params("parallel"),
    )(x, gain.reshape(1, d))


def _mm_kernel(a_ref, w_ref, o_ref, *, scale):
    acc = jnp.dot(a_ref[...], w_ref[...], preferred_element_type=F32)
    if scale is not None:
        acc = acc * scale
    o_ref[...] = acc.astype(o_ref.dtype)


def matmul(a, w, out_dtype, scale=None, col0=0, n=None, tm=1024, tn=1024):
    m, k = a.shape
    n = w.shape[1] - col0 if n is None else n
    tm, tn = _tile(m, tm), _tile(n, tn)
    while col0 % tn:
        tn = _tile(n, tn - 1)
    jb = col0 // tn
    return pl.pallas_call(
        functools.partial(_mm_kernel, scale=scale),
        grid=(m // tm, n // tn),
        in_specs=[pl.BlockSpec((tm, k), lambda i, j: (i, 0)), pl.BlockSpec((k, tn), lambda i, j: (0, j + jb))],
        out_specs=pl.BlockSpec((tm, tn), lambda i, j: (i, j)),
        out_shape=jax.ShapeDtypeStruct((m, n), out_dtype),
        name="matmul",
        compiler_params=_params("parallel", "arbitrary"),
    )(a, w)


def _mm_postnorm_kernel(a1_ref, a2_ref, w_ref, x_ref, g_ref, o_ref, *, nk1, nk):
    k = pl.program_id(1)

    @pl.when(k == 0)
    def _():
        o_ref[...] = jnp.zeros_like(o_ref)

    @pl.when(k < nk1)
    def _():
        o_ref[...] += jnp.dot(a1_ref[...], w_ref[...], preferred_element_type=F32)

    @pl.when(k >= nk1)
    def _():
        o_ref[...] += jnp.dot(a2_ref[...], w_ref[...], preferred_element_type=F32)

    @pl.when(k == nk - 1)
    def _():
        def post(r, carry):
            rows = pl.ds(pl.multiple_of(r * NORM_ROWS, NORM_ROWS), NORM_ROWS)
            o_ref[rows, :] = x_ref[rows, :] + _rms(o_ref[rows, :], g_ref[...])
            return carry
        lax.fori_loop(0, o_ref.shape[0] // NORM_ROWS, post, 0)


def matmul_postnorm_residual(a1, a2, w, x, gain, tm=512, tk=1024):
    m, k1 = a1.shape
    k2 = a2.shape[1]
    _, n = w.shape
    tm = _tile(m, tm)
    tk = _tile(k1, tk)
    while k2 % tk:
        tk = _tile(k1, tk - 1)
    nk1, nk = k1 // tk, (k1 + k2) // tk
    return pl.pallas_call(
        functools.partial(_mm_postnorm_kernel, nk1=nk1, nk=nk),
        grid=(m // tm, nk),
        in_specs=[
            pl.BlockSpec((tm, tk), lambda i, k: (i, jnp.minimum(k, nk1 - 1))),
            pl.BlockSpec((tm, tk), lambda i, k: (i, jnp.maximum(k - nk1, 0))),
            pl.BlockSpec((tk, n), lambda i, k: (k, 0)),
            pl.BlockSpec((tm, n), lambda i, k: (i, 0)),
            pl.BlockSpec((1, n), lambda i, k: (0, 0)),
        ],
        out_specs=pl.BlockSpec((tm, n), lambda i, k: (i, 0)),
        out_shape=jax.ShapeDtypeStruct((m, n), F32),
        name="outproj_postnorm",
        compiler_params=_params("parallel", "arbitrary"),
    )(a1, a2, w, x, gain.reshape(1, n))


def _gelu_tanh(x):
    c = 0.7978845608028654
    return x * (0.5 * (1.0 + jnp.tanh(c * (x + 0.044715 * (x * x * x)))))


def _compress_kernel(x_ref, pe_ref, w1_ref, w2_ref, o_ref, *, ncp):
    a = jnp.zeros((ncp, HEAD_DIM), F32)
    b = jnp.zeros((ncp, HEAD_DIM), F32)
    for l in range(CMP_STRIDE):
        xl = x_ref[pl.ds(l, ncp, stride=CMP_STRIDE), :]
        lo, hi = l, CMP_STRIDE + l
        a = a + jnp.dot((xl + pe_ref[lo:lo + 1, :]).astype(BF16), w1_ref[lo * HEAD_DIM:(lo + 1) * HEAD_DIM, :],
                        preferred_element_type=F32)
        b = b + jnp.dot((xl + pe_ref[hi:hi + 1, :]).astype(BF16), w1_ref[hi * HEAD_DIM:(hi + 1) * HEAD_DIM, :],
                        preferred_element_type=F32)
    hid = _gelu_tanh(a + pltpu.roll(b, ncp - 1, 0))
    o_ref[...] = jnp.dot(hid.astype(BF16), w2_ref[...], preferred_element_type=F32).astype(o_ref.dtype)


def compress_tokens(kvc, pe, w1, w2, n_groups):
    t = kvc.shape[0]
    ncp = t // CMP_STRIDE
    assert CMP_BLOCK == 2 * CMP_STRIDE
    return pl.pallas_call(
        functools.partial(_compress_kernel, ncp=ncp),
        grid=(2, n_groups),
        in_specs=[
            pl.BlockSpec((t, HEAD_DIM), lambda s, j: (0, s * n_groups + j)),
            pl.BlockSpec((None, CMP_BLOCK, HEAD_DIM), lambda s, j: (s, 0, 0)),
            pl.BlockSpec((None, CMP_BLOCK * HEAD_DIM, HEAD_DIM), lambda s, j: (s, 0, 0)),
            pl.BlockSpec((None, HEAD_DIM, HEAD_DIM), lambda s, j: (s, 0, 0)),
        ],
        out_specs=pl.BlockSpec((None, None, ncp, HEAD_DIM), lambda s, j: (s, j, 0, 0)),
        out_shape=jax.ShapeDtypeStruct((2, n_groups, ncp, HEAD_DIM), BF16),
        name="compress_tokens",
        compiler_params=_params("parallel", "parallel"),
    )(kvc, pe, w1, w2)


def _attn_kernel(slopes_ref, q_ref, kcmp_ref, vcmp_ref, ks_ref, vs_ref, kw_ref, vw_ref, gl_ref,
                 o_ref, p_scr, imp_scr, sel_scr, ocmp_scr, acc_slc, acc_win, alibi_scr, u_scr, flag_ref,
                 *, hpg, nc, ncp, nb, n_sel):
    g = pl.program_id(0)
    i = pl.program_id(1)
    t0 = i * QB
    w = hpg * QB
    nq = QB // LANES
    ninf = -jnp.inf

    q = q_ref[...]
    qs = jnp.concatenate([q[:, h * HEAD_DIM:(h + 1) * HEAD_DIM] for h in range(hpg)], axis=0)
    slopes = [slopes_ref[g * hpg + h] for h in range(hpg)]

    rc_i = (lax.broadcasted_iota(jnp.int32, (WIN_KEYS, QB), 1)
            - lax.broadcasted_iota(jnp.int32, (WIN_KEYS, QB), 0))

    @pl.when(i == 0)
    def _():
        rc_f = (-rc_i).astype(F32)
        for h in range(hpg):
            alibi_scr[h] = slopes[h] * rc_f

    def cmp_branch(rows):
        s = lax.dot_general(kcmp_ref[0:rows, :], qs, _NT, preferred_element_type=F32)
        c_iota = lax.broadcasted_iota(jnp.int32, (rows, QB), 0)
        q_iota = lax.broadcasted_iota(jnp.int32, (rows, QB), 1)
        cmp_end = c_iota * CMP_STRIDE + (CMP_BLOCK - 1)
        mask_c = (cmp_end <= t0 + q_iota) & (c_iota < nc)
        rel_c = (cmp_end - t0).astype(F32)
        psum = jnp.zeros((rows, QB), F32)
        for h in range(hpg):
            sh = s[:, h * QB:(h + 1) * QB] + slopes[h] * rel_c
            sh = jnp.where(mask_c, sh, ninf)
            m = jnp.maximum(jnp.max(sh, axis=0, keepdims=True), NEG_INF)
            p = jnp.exp2(sh - m)
            l = jnp.sum(p, axis=0, keepdims=True)
            pn = p * jnp.where(l > 0.0, 1.0 / l, 0.0)
            psum = psum + pn
            ocmp_scr[:, h * QB:(h + 1) * QB] = lax.dot_general(vcmp_ref[0:rows, :], pn.astype(BF16), _TN,
                                                                preferred_element_type=F32)
        for hq in range(nq):
            p_scr[hq, SUBLANES:SUBLANES + rows, :] = psum[:, hq * LANES:(hq + 1) * LANES]
            if rows < ncp:
                p_scr[hq, SUBLANES + rows:SUBLANES + ncp, :] = jnp.zeros((ncp - rows, LANES), F32)

    n_ended = (t0 + QB - CMP_BLOCK) // CMP_STRIDE + 1
    variant = (n_ended - 1) // CMP_ROWS

    def cmp_dispatch(k):
        rows = (k + 1) * CMP_ROWS
        if rows >= ncp:
            cmp_branch(ncp)
        else:
            lax.cond(variant <= k, lambda: cmp_branch(rows), lambda: cmp_dispatch(k + 1))

    cmp_dispatch(0)

    n_grp = nb // SUBLANES
    n_iota = lax.broadcasted_iota(jnp.int32, (nb, LANES), 0)
    row8 = lax.broadcasted_iota(jnp.int32, (SUBLANES, LANES), 0)
    vals, valids = [], []
    for hq in range(nq):
        p_scr[hq, 0:SUBLANES, :] = jnp.zeros((SUBLANES, LANES), F32)
        imp = p_scr[hq, pl.ds(SUBLANES - 1, nb, stride=4), :]
        for j in range(4):
            imp = imp + p_scr[hq, pl.ds(SUBLANES + j, nb, stride=4), :]
        tq = t0 + hq * LANES + lax.broadcasted_iota(jnp.int32, (nb, LANES), 1)
        cur = lax.shift_right_logical(tq, 6)
        valid = n_iota <= cur
        back = cur - n_iota
        forced = (n_iota == 0) | ((back >= 0) & (back < N_LOCAL_FORCED))
        imp = jnp.where(forced & valid, FORCED_SCORE, imp)
        imp = jnp.where(valid, imp, NEG_INF)
        imp_scr[hq] = imp
        vals.append([imp[j * SUBLANES:(j + 1) * SUBLANES, :] for j in range(n_grp)])
        valids.append(valid)

    def count(cnts, hq, mi, j):
        vm = jnp.broadcast_to(imp_scr[hq, pl.ds(mi, 1), :], (SUBLANES, LANES))
        v = vals[hq][j]
        lo = j * SUBLANES
        if lo + SUBLANES - 1 < mi:
            beats = vm > v
        elif lo > mi:
            beats = vm >= v
        else:
            beats = (vm > v) | ((vm == v) & (row8 > mi - lo))
        cnts[hq * n_grp + j] = cnts[hq * n_grp + j] + jnp.where(beats, 1.0, 0.0)

    def rank_shell(sh, cnts):
        cnts = list(cnts)
        for hq in range(nq):
            for mi in range(sh * SUBLANES, (sh + 1) * SUBLANES):
                for j in range(sh + 1):
                    count(cnts, hq, mi, j)
            for mi in range(sh * SUBLANES):
                count(cnts, hq, mi, sh)
        return tuple(cnts)

    last_valid = (t0 + QB - 1) // SLC_BLOCK
    cnts = tuple(jnp.zeros((SUBLANES, LANES), F32) for _ in range(nq * n_grp))
    for sh in range(n_grp):
        cnts = lax.cond(sh * SUBLANES <= last_valid, functools.partial(rank_shell, sh), lambda c: c, cnts)
    blocks_per_kt = KT // SLC_BLOCK
    any_sel = jnp.zeros((nb, LANES), F32)
    for hq in range(nq):
        cnt = jnp.concatenate(cnts[hq * n_grp:(hq + 1) * n_grp], axis=0)
        sel = (cnt < float(n_sel)) & valids[hq]
        sel_scr[hq] = jnp.where(sel, 0.0, ninf)
        any_sel = jnp.maximum(any_sel, jnp.where(sel, 1.0, 0.0))
    for kt in range(nb // blocks_per_kt):
        hit = jnp.max(any_sel[kt * blocks_per_kt:(kt + 1) * blocks_per_kt, :])
        flag_ref[kt] = (hit > 0.5).astype(jnp.int32)

    init = (tuple(jnp.full((1, QB), NEG_INF, F32) for _ in range(hpg)),
            tuple(jnp.zeros((1, QB), F32) for _ in range(hpg)))

    def stage_a(kt, slot):
        k0 = pl.multiple_of(kt * KT, KT)
        st = lax.dot_general(ks_ref[pl.ds(k0, KT), :], qs, _NT, preferred_element_type=F32)
        rows = [jnp.concatenate([jnp.broadcast_to(sel_scr[hq, pl.ds(kt * blocks_per_kt + b, 1), :],
                                                  (SLC_BLOCK, LANES)) for hq in range(nq)], axis=1)
                for b in range(blocks_per_kt)]
        diff = (t0 - k0) + rc_i[0:KT, :]
        base = jnp.where(diff >= 0, jnp.concatenate(rows, axis=0), ninf)
        off = (k0 - t0).astype(F32)
        tile_max = []
        for h in range(hpg):
            u = st[:, h * QB:(h + 1) * QB] + (alibi_scr[h, 0:KT, :] + base)
            u_scr[slot, h] = u
            tile_max.append(jnp.max(u, axis=0, keepdims=True) + slopes[h] * off)
        return tuple(tile_max)

    def stage_b(kt, slot, tile_max, ms, ls):
        k0 = pl.multiple_of(kt * KT, KT)
        off = (k0 - t0).astype(F32)
        new_m, new_l, alphas, ps = [], [], [], []
        for h in range(hpg):
            mh = jnp.maximum(ms[h], tile_max[h])
            alpha = jnp.exp2(ms[h] - mh)
            p = jnp.exp2(u_scr[slot, h] - (mh - slopes[h] * off))
            new_m.append(mh)
            new_l.append(alpha * ls[h] + jnp.sum(p, axis=0, keepdims=True))
            alphas.append(alpha)
            ps.append(p.astype(BF16))
        pv = lax.dot_general(vs_ref[pl.ds(k0, KT), :], jnp.concatenate(ps, axis=1), _TN,
                             preferred_element_type=F32)
        acc_slc[...] = acc_slc[...] * jnp.concatenate(alphas, axis=1) + pv
        return tuple(new_m), tuple(new_l)

    def slc_step(kt, carry):
        def visit(c):
            ms, ls, tile_max, prev, slot = c
            ms, ls = stage_b(prev, slot, tile_max, ms, ls)
            return ms, ls, stage_a(kt, 1 - slot), kt, 1 - slot
        return lax.cond(flag_ref[kt] > 0, visit, lambda c: c, carry)

    kd = t0 // KT
    acc_slc[...] = jnp.zeros_like(acc_slc)
    zero = jnp.int32(0)
    ms, ls, tile_max, prev, slot = lax.fori_loop(1, kd + 1, slc_step, init + (stage_a(zero, zero), zero, zero))
    _, l_slc = stage_b(prev, slot, tile_max, ms, ls)

    w0 = pl.multiple_of(jnp.maximum(t0 - WINDOW, 0), KB)
    diff_w = (t0 - w0) + rc_i
    base_w = jnp.where((diff_w >= 0) & (diff_w < WINDOW), 0.0, ninf)
    st_w = lax.dot_general(kw_ref[pl.ds(w0, WIN_KEYS), :], qs, _NT, preferred_element_type=F32)
    l_win, ps = [], []
    for h in range(hpg):
        u = st_w[:, h * QB:(h + 1) * QB] + (alibi_scr[h] + base_w)
        p = jnp.exp2(u - jnp.max(u, axis=0, keepdims=True))
        l_win.append(jnp.sum(p, axis=0, keepdims=True))
        ps.append(p.astype(BF16))
    acc_win[...] = lax.dot_general(vw_ref[pl.ds(w0, WIN_KEYS), :], jnp.concatenate(ps, axis=1), _TN,
                                   preferred_element_type=F32)

    for h in range(hpg):
        cols = slice(h * QB, (h + 1) * QB)
        g_cmp = jax.nn.sigmoid(gl_ref[0, h:h + 1, :])
        g_slc = jax.nn.sigmoid(gl_ref[1, h:h + 1, :])
        g_win = jax.nn.sigmoid(gl_ref[2, h:h + 1, :])
        oT = (g_cmp * ocmp_scr[:, cols] + (g_slc / l_slc[h]) * acc_slc[:, cols]
              + (g_win / l_win[h]) * acc_win[:, cols])
        o_ref[:, h * HEAD_DIM:(h + 1) * HEAD_DIM] = oT.T.astype(o_ref.dtype)


def sparse_attention(slopes, q, cmp_kv, kv_rest, gate_logits_t, *, nc):
    t, hd = q.shape
    _, n_groups, ncp, _ = cmp_kv.shape
    hpg = hd // HEAD_DIM // n_groups
    nb = t // SLC_BLOCK
    assert t % KT == 0 and t >= WIN_KEYS and ncp == 4 * nb and nb % SUBLANES == 0
    assert (CMP_BLOCK, CMP_STRIDE, SLC_BLOCK) == (32, 16, 64)
    n_sel = min(N_SELECT, nb)
    w = hpg * QB
    kern = functools.partial(_attn_kernel, hpg=hpg, nc=nc, ncp=ncp, nb=nb, n_sel=n_sel)
    kv_col = lambda part: (lambda g, i: (0, part * n_groups + g))
    return pl.pallas_call(
        kern,
        grid=(n_groups, t // QB),
        in_specs=[
            pl.BlockSpec(memory_space=pltpu.SMEM),
            pl.BlockSpec((QB, hpg * HEAD_DIM), lambda g, i: (i, g)),
            pl.BlockSpec((None, None, ncp, HEAD_DIM), lambda g, i: (0, g, 0, 0)),
            pl.BlockSpec((None, None, ncp, HEAD_DIM), lambda g, i: (1, g, 0, 0)),
            pl.BlockSpec((t, HEAD_DIM), kv_col(0)),
            pl.BlockSpec((t, HEAD_DIM), kv_col(1)),
            pl.BlockSpec((t, HEAD_DIM), kv_col(2)),
            pl.BlockSpec((t, HEAD_DIM), kv_col(3)),
            pl.BlockSpec((3, None, hpg, QB), lambda g, i: (0, g, 0, i)),
        ],
        out_specs=pl.BlockSpec((QB, hpg * HEAD_DIM), lambda g, i: (i, g)),
        out_shape=jax.ShapeDtypeStruct((t, hd), BF16),
        scratch_shapes=[
            pltpu.VMEM((QB // LANES, SUBLANES + ncp, LANES), F32),
            pltpu.VMEM((QB // LANES, nb, LANES), F32),
            pltpu.VMEM((QB // LANES, nb, LANES), F32),
            pltpu.VMEM((HEAD_DIM, w), F32),
            pltpu.VMEM((HEAD_DIM, w), F32),
            pltpu.VMEM((HEAD_DIM, w), F32),
            pltpu.VMEM((hpg, WIN_KEYS, QB), F32),
            pltpu.VMEM((2, hpg, KT, QB), F32),
            pltpu.SMEM((t // KT,), jnp.int32),
        ],
        name="sparse_attention",
        compiler_params=_params("parallel", "arbitrary"),
    )(slopes, q, cmp_kv, cmp_kv, kv_rest, kv_rest, kv_rest, kv_rest, gate_logits_t)


def _inproj_conv_kernel(a_ref, wb_ref, wc_ref, wh_ref, cw_ref, o_ref, halo_ref):
    i = pl.program_id(0)
    j = pl.program_id(1)
    a = a_ref[...]
    b = jnp.dot(a, wb_ref[...], preferred_element_type=F32)
    u = (jnp.dot(a, wc_ref[...], preferred_element_type=F32)
         * jnp.dot(a, wh_ref[...], preferred_element_type=F32))
    wk = cw_ref[...]
    w0, w1, w2 = wk[0:1, :], wk[1:2, :], wk[2:3, :]
    tm = u.shape[0]
    y = b * (w0 * pltpu.roll(u, 2, 0) + w1 * pltpu.roll(u, 1, 0) + w2 * u)
    o_ref[...] = y.astype(o_ref.dtype)
    @pl.when(i == 0)
    def _():
        halo_ref[j] = jnp.zeros(halo_ref.shape[1:], F32)

    ue = jnp.concatenate([halo_ref[j], u[0:SUBLANES, :]], axis=0)
    n2 = 2 * SUBLANES
    u1 = pltpu.roll(ue, 1, 0)[SUBLANES:n2, :]
    u2 = pltpu.roll(ue, 2, 0)[SUBLANES:n2, :]
    y0 = b[0:SUBLANES, :] * (w0 * u2 + w1 * u1 + w2 * u[0:SUBLANES, :])
    o_ref[0:SUBLANES, :] = y0.astype(o_ref.dtype)
    halo_ref[j] = u[tm - SUBLANES:tm, :]


def inproj_short_conv(a, w, conv_w, *, col0, cw, tm=1024, tn=512):
    m, k = a.shape
    tm, tn = _tile(m, tm), _tile(cw, tn)
    while col0 % tn:
        tn = _tile(cw, tn - 1)
    ncb = cw // tn
    jb = col0 // tn
    wcol = lambda part: (lambda i, j: (0, jb + part * ncb + j))
    return pl.pallas_call(
        _inproj_conv_kernel,
        grid=(m // tm, ncb),
        in_specs=[
            pl.BlockSpec((tm, k), lambda i, j: (i, 0)),
            pl.BlockSpec((k, tn), wcol(0)),
            pl.BlockSpec((k, tn), wcol(1)),
            pl.BlockSpec((k, tn), wcol(2)),
            pl.BlockSpec((CONV_K, tn), lambda i, j: (0, j)),
        ],
        out_specs=pl.BlockSpec((tm, tn), lambda i, j: (i, j)),
        out_shape=jax.ShapeDtypeStruct((m, cw), BF16),
        scratch_shapes=[pltpu.VMEM((ncb, SUBLANES, tn), F32)],
        name="inproj_short_conv",
        compiler_params=_params("arbitrary", "arbitrary"),
    )(a, w, w, w, conv_w)


def _mem_kernel(x_ref, gpre_ref, gpost_ref, wq_ref, km_ref, vm_ref, wo_ref, o_ref):
    x = x_ref[...]
    xn = _rms(x, gpre_ref[...]).astype(BF16)
    qm = jnp.dot(xn, wq_ref[...], preferred_element_type=F32) * (MEM_HEAD_DIM ** -0.5)
    outs = []
    for h in range(MEM_HEADS):
        cols = slice(h * MEM_HEAD_DIM, (h + 1) * MEM_HEAD_DIM)
        s = lax.dot_general(qm[:, cols].astype(BF16), km_ref[:, cols], _NT, preferred_element_type=F32)
        p = jnp.exp(s - jnp.max(s, axis=-1, keepdims=True))
        p = p / jnp.sum(p, axis=-1, keepdims=True)
        outs.append(jnp.dot(p.astype(BF16), vm_ref[:, cols], preferred_element_type=F32).astype(BF16))
    y = jnp.dot(jnp.concatenate(outs, axis=1), wo_ref[...], preferred_element_type=F32)
    o_ref[...] = x + _rms(y, gpost_ref[...])


def memory_sublayer(x, gpre, gpost, wq, km, vm, wo, tm=512):
    m, d = x.shape
    mw = wq.shape[1]
    nm = km.shape[0]
    tm = _tile(m, tm)
    full = lambda i: (0, 0)
    return pl.pallas_call(
        _mem_kernel,
        grid=(m // tm,),
        in_specs=[
            pl.BlockSpec((tm, d), lambda i: (i, 0)),
            pl.BlockSpec((1, d), full),
            pl.BlockSpec((1, d), full),
            pl.BlockSpec((d, mw), full),
            pl.BlockSpec((nm, mw), full),
            pl.BlockSpec((nm, mw), full),
            pl.BlockSpec((mw, d), full),
        ],
        out_specs=pl.BlockSpec((tm, d), lambda i: (i, 0)),
        out_shape=jax.ShapeDtypeStruct((m, d), F32),
        name="memory_sublayer",
        compiler_params=_params("parallel"),
    )(x, gpre.reshape(1, d), gpost.reshape(1, d), wq, km, vm, wo)


def _mlp_kernel(x_ref, gpre_ref, gpost_ref, wu_ref, wd_ref, o_ref, xn_scr, *, nf):
    f = pl.program_id(1)

    n_chunks = x_ref.shape[0] // NORM_ROWS

    @pl.when(f == 0)
    def _():
        def pre(r, carry):
            rows = pl.ds(pl.multiple_of(r * NORM_ROWS, NORM_ROWS), NORM_ROWS)
            xn_scr[rows, :] = _rms(x_ref[rows, :], gpre_ref[...]).astype(BF16)
            o_ref[rows, :] = jnp.zeros((NORM_ROWS, o_ref.shape[1]), F32)
            return carry
        lax.fori_loop(0, n_chunks, pre, 0)

    hid = jnp.dot(xn_scr[...], wu_ref[...], preferred_element_type=F32)
    hid = jnp.square(jnp.maximum(hid, 0.0))
    o_ref[...] += jnp.dot(hid.astype(BF16), wd_ref[...], preferred_element_type=F32)

    @pl.when(f == nf - 1)
    def _():
        def post(r, carry):
            rows = pl.ds(pl.multiple_of(r * NORM_ROWS, NORM_ROWS), NORM_ROWS)
            o_ref[rows, :] = x_ref[rows, :] + _rms(o_ref[rows, :], gpost_ref[...])
            return carry
        lax.fori_loop(0, n_chunks, post, 0)


def mlp_sublayer(x, gpre, gpost, w_up, w_down, tm=1024, tf=512):
    m, d = x.shape
    dff = w_up.shape[1]
    tm, tf = _tile(m, tm), _tile(dff, tf)
    nf = dff // tf
    w_up = w_up.astype(BF16)
    w_down = w_down.astype(BF16)
    return pl.pallas_call(
        functools.partial(_mlp_kernel, nf=nf),
        grid=(m // tm, nf),
        in_specs=[
            pl.BlockSpec((tm, d), lambda i, f: (i, 0), pipeline_mode=pl.Buffered(1)),
            pl.BlockSpec((1, d), lambda i, f: (0, 0)),
            pl.BlockSpec((1, d), lambda i, f: (0, 0)),
            pl.BlockSpec((d, tf), lambda i, f: (0, f)),
            pl.BlockSpec((tf, d), lambda i, f: (f, 0)),
        ],
        out_specs=pl.BlockSpec((tm, d), lambda i, f: (i, 0), pipeline_mode=pl.Buffered(1)),
        out_shape=jax.ShapeDtypeStruct((m, d), F32),
        scratch_shapes=[pltpu.VMEM((tm, d), BF16)],
        name="mlp_sublayer",
        compiler_params=pltpu.CompilerParams(dimension_semantics=("parallel", "arbitrary"),
                                             vmem_limit_bytes=MLP_VMEM_LIMIT),
    )(x, gpre.reshape(1, d), gpost.reshape(1, d), w_up, w_down)


def _mixer(x, norm_pre, norm_post, w_in, cmp_k_pe, cmp_k_w1, cmp_k_w2, cmp_v_pe, cmp_v_w1, cmp_v_w2,
           conv_w, w_out):
    t, d = x.shape
    attn_w = d // 2
    conv_cw = d - attn_w
    n_heads = attn_w // HEAD_DIM
    n_groups = n_heads // 4
    kvw = n_groups * HEAD_DIM
    o_kv = attn_w
    o_gate = o_kv + 6 * kvw
    o_conv = o_gate + 3 * n_heads
    assert w_in.shape[1] == o_conv + 3 * conv_cw

    assert o_gate % LANES == 0 and o_gate + LANES <= w_in.shape[1]
    w_bf = w_in.astype(BF16)
    w_conv = w_in[:, o_conv:].astype(BF16)

    xn = rmsnorm_cast(x, norm_pre)
    q = matmul(xn, w_bf, BF16, scale=HEAD_DIM ** -0.5 * LOG2E, col0=0, n=attn_w)
    kvc = matmul(xn, w_bf, F32, col0=o_kv, n=2 * kvw)
    kv_rest = matmul(xn, w_bf, BF16, col0=o_kv + 2 * kvw, n=4 * kvw)
    o_conv_out = inproj_short_conv(xn, w_conv, conv_w, col0=0, cw=conv_cw)
    gate_logits = matmul(xn, w_bf, F32, col0=o_gate, n=LANES)[:, :3 * n_heads]

    nc = (t - CMP_BLOCK) // CMP_STRIDE + 1
    pe = jnp.stack([cmp_k_pe, cmp_v_pe])
    w1 = jnp.stack([cmp_k_w1, cmp_v_w1]).astype(BF16)
    w2 = jnp.stack([cmp_k_w2, cmp_v_w2]).astype(BF16)
    cmp_kv = compress_tokens(kvc, pe, w1, w2, n_groups)

    gl_t = gate_logits.T.reshape(3, n_groups, n_heads // n_groups, t)
    idx = jnp.arange(1, n_heads + 1, dtype=F32)
    slopes = jnp.exp2(-8.0 * idx / n_heads) * LOG2E
    o_attn = sparse_attention(slopes, q, cmp_kv, kv_rest, gl_t, nc=nc)
    return matmul_postnorm_residual(o_attn, o_conv_out, w_out.astype(BF16), x, norm_post)


def _memory(x, mem, norm_pre, norm_kv, norm_post, wq, wk, wv, wo):
    mn = rmsnorm_cast(mem, norm_kv)
    km = matmul(mn, wk.astype(BF16), BF16)
    vm = matmul(mn, wv.astype(BF16), BF16)
    return memory_sublayer(x, norm_pre, norm_post, wq.astype(BF16), km, vm, wo.astype(BF16))


def kernel(x, mem, mix_norm_pre, mix_norm_post, w_in, cmp_k_pe, cmp_k_w1, cmp_k_w2, cmp_v_pe, cmp_v_w1,
           cmp_v_w2, conv_w, w_out, mem_norm_pre, mem_norm_kv, mem_norm_post, w_mem_q, w_mem_k, w_mem_v,
           w_mem_o, mlp_norm_pre, mlp_norm_post, w_up, w_down):
    b, t, d = x.shape
    assert b == 1
    h = x[0]
    m = mem[0]
    for l in range(w_in.shape[0]):
        h = _mixer(h, mix_norm_pre[l], mix_norm_post[l], w_in[l], cmp_k_pe[l], cmp_k_w1[l], cmp_k_w2[l],
                   cmp_v_pe[l], cmp_v_w1[l], cmp_v_w2[l], conv_w[l], w_out[l])
        h = _memory(h, m, mem_norm_pre[l], mem_norm_kv[l], mem_norm_post[l], w_mem_q[l], w_mem_k[l],
                    w_mem_v[l], w_mem_o[l])
        h = mlp_sublayer(h, mlp_norm_pre[l], mlp_norm_post[l], w_up[l], w_down[l])
    return h[None]
```

```python
import functools

import jax
import jax.numpy as jnp
from jax import lax
from jax.experimental import pallas as pl
from jax.experimental.pallas import tpu as pltpu

HEAD_DIM = 128
CMP_BLOCK = 32
CMP_STRIDE = 16
SLC_BLOCK = 64
N_SELECT = 16
N_LOCAL_FORCED = 2
WINDOW = 512
CONV_K = 3
MEM_HEADS = 4
MEM_HEAD_DIM = 128
RMS_EPS = 1e-6
NEG_INF = -1e30
FORCED_SCORE = 1e9
LOG2E = 1.4426950408889634

V7X_VMEM_BYTES = 64 * 1024 * 1024
VMEM_LIMIT = V7X_VMEM_BYTES - 8 * 1024 * 1024
MLP_VMEM_LIMIT = V7X_VMEM_BYTES - 4 * 1024 * 1024
NORM_ROWS = 64
LANES = 128
SUBLANES = 8

QB = 256
KB = 128
KT = 512
CMP_ROWS = 128
WIN_KEYS = WINDOW + QB

F32 = jnp.float32
BF16 = jnp.bfloat16

_NT = (((1,), (1,)), ((), ()))
_TN = (((0,), (0,)), ((), ()))


def _tile(n, pref):
    t = min(n, pref)
    while n % t:
        t -= 1
    return t


def _params(*sem):
    return pltpu.CompilerParams(dimension_semantics=sem, vmem_limit_bytes=VMEM_LIMIT)


def _rms(x, gain):
    return x * lax.rsqrt(jnp.mean(x * x, axis=-1, keepdims=True) + RMS_EPS) * gain


def _rmsnorm_kernel(x_ref, g_ref, o_ref):
    o_ref[...] = _rms(x_ref[...].astype(F32), g_ref[...]).astype(o_ref.dtype)


def rmsnorm_cast(x, gain, out_dtype=BF16):
    m, d = x.shape
    tm = _tile(m, 256)
    return pl.pallas_call(
        _rmsnorm_kernel,
        grid=(m // tm,),
        in_specs=[pl.BlockSpec((tm, d), lambda i: (i, 0)), pl.BlockSpec((1, d), lambda i: (0, 0))],
        out_specs=pl.BlockSpec((tm, d), lambda i: (i, 0)),
        out_shape=jax.ShapeDtypeStruct((m, d), out_dtype),
        name="rmsnorm_cast",
        compiler_params=_params("parallel"),
    )(x, gain.reshape(1, d))


def _mm_kernel(a_ref, w_ref, o_ref, *, scale):
    acc = jnp.dot(a_ref[...], w_ref[...], preferred_element_type=F32)
    if scale is not None:
        acc = acc * scale
    o_ref[...] = acc.astype(o_ref.dtype)


def matmul(a, w, out_dtype, scale=None, col0=0, n=None, tm=1024, tn=1024):
    m, k = a.shape
    n = w.shape[1] - col0 if n is None else n
    tm, tn = _tile(m, tm), _tile(n, tn)
    while col0 % tn:
        tn = _tile(n, tn - 1)
    jb = col0 // tn
    return pl.pallas_call(
        functools.partial(_mm_kernel, scale=scale),
        grid=(m // tm, n // tn),
        in_specs=[pl.BlockSpec((tm, k), lambda i, j: (i, 0)), pl.BlockSpec((k, tn), lambda i, j: (0, j + jb))],
        out_specs=pl.BlockSpec((tm, tn), lambda i, j: (i, j)),
        out_shape=jax.ShapeDtypeStruct((m, n), out_dtype),
        name="matmul",
        compiler_params=_params("parallel", "arbitrary"),
    )(a, w)


def _mm_postnorm_kernel(a1_ref, a2_ref, w_ref, x_ref, g_ref, o_ref, *, nk1, nk):
    k = pl.program_id(1)

    @pl.when(k == 0)
    def _():
        o_ref[...] = jnp.dot(a1_ref[...], w_ref[...], preferred_element_type=F32)

    @pl.when((k > 0) & (k < nk1))
    def _():
        o_ref[...] += jnp.dot(a1_ref[...], w_ref[...], preferred_element_type=F32)

    @pl.when(k >= nk1)
    def _():
        o_ref[...] += jnp.dot(a2_ref[...], w_ref[...], preferred_element_type=F32)

    @pl.when(k == nk - 1)
    def _():
        def post(r, carry):
            rows = pl.ds(pl.multiple_of(r * NORM_ROWS, NORM_ROWS), NORM_ROWS)
            o_ref[rows, :] = x_ref[rows, :] + _rms(o_ref[rows, :], g_ref[...])
            return carry
        lax.fori_loop(0, o_ref.shape[0] // NORM_ROWS, post, 0)


def matmul_postnorm_residual(a1, a2, w, x, gain, tm=512, tk=1024):
    m, k1 = a1.shape
    k2 = a2.shape[1]
    _, n = w.shape
    tm = _tile(m, tm)
    tk = _tile(k1, tk)
    while k2 % tk:
        tk = _tile(k1, tk - 1)
    nk1, nk = k1 // tk, (k1 + k2) // tk
    return pl.pallas_call(
        functools.partial(_mm_postnorm_kernel, nk1=nk1, nk=nk),
        grid=(m // tm, nk),
        in_specs=[
            pl.BlockSpec((tm, tk), lambda i, k: (i, jnp.minimum(k, nk1 - 1))),
            pl.BlockSpec((tm, tk), lambda i, k: (i, jnp.maximum(k - nk1, 0))),
            pl.BlockSpec((tk, n), lambda i, k: (k, 0)),
            pl.BlockSpec((tm, n), lambda i, k: (i, 0)),
            pl.BlockSpec((1, n), lambda i, k: (0, 0)),
        ],
        out_specs=pl.BlockSpec((tm, n), lambda i, k: (i, 0)),
        out_shape=jax.ShapeDtypeStruct((m, n), F32),
        name="outproj_postnorm",
        compiler_params=_params("parallel", "arbitrary"),
    )(a1, a2, w, x, gain.reshape(1, n))


def _gelu_tanh(x):
    c = 0.7978845608028654
    return x * (0.5 * (1.0 + jnp.tanh(c * (x + 0.044715 * (x * x * x)))))


def _compress_kernel(x_ref, pe_ref, w1_ref, w2_ref, o_ref, *, ncp):
    a = jnp.zeros((ncp, HEAD_DIM), F32)
    b = jnp.zeros((ncp, HEAD_DIM), F32)
    for l in range(CMP_STRIDE):
        xl = x_ref[pl.ds(l, ncp, stride=CMP_STRIDE), :]
        lo, hi = l, CMP_STRIDE + l
        a = a + jnp.dot((xl + pe_ref[lo:lo + 1, :]).astype(BF16), w1_ref[lo * HEAD_DIM:(lo + 1) * HEAD_DIM, :],
                        preferred_element_type=F32)
        b = b + jnp.dot((xl + pe_ref[hi:hi + 1, :]).astype(BF16), w1_ref[hi * HEAD_DIM:(hi + 1) * HEAD_DIM, :],
                        preferred_element_type=F32)
    hid = _gelu_tanh(a + pltpu.roll(b, ncp - 1, 0))
    o_ref[...] = jnp.dot(hid.astype(BF16), w2_ref[...], preferred_element_type=F32).astype(o_ref.dtype)


def compress_tokens(kvc, pe, w1, w2, n_groups):
    t = kvc.shape[0]
    ncp = t // CMP_STRIDE
    assert CMP_BLOCK == 2 * CMP_STRIDE
    return pl.pallas_call(
        functools.partial(_compress_kernel, ncp=ncp),
        grid=(2, n_groups),
        in_specs=[
            pl.BlockSpec((t, HEAD_DIM), lambda s, j: (0, s * n_groups + j)),
            pl.BlockSpec((None, CMP_BLOCK, HEAD_DIM), lambda s, j: (s, 0, 0)),
            pl.BlockSpec((None, CMP_BLOCK * HEAD_DIM, HEAD_DIM), lambda s, j: (s, 0, 0)),
            pl.BlockSpec((None, HEAD_DIM, HEAD_DIM), lambda s, j: (s, 0, 0)),
        ],
        out_specs=pl.BlockSpec((None, None, ncp, HEAD_DIM), lambda s, j: (s, j, 0, 0)),
        out_shape=jax.ShapeDtypeStruct((2, n_groups, ncp, HEAD_DIM), BF16),
        name="compress_tokens",
        compiler_params=_params("parallel", "parallel"),
    )(kvc, pe, w1, w2)


def _attn_kernel(slopes_ref, q_ref, kcmp_ref, vcmp_ref, ks_ref, vs_ref, kw_ref, vw_ref, gl_ref,
                 o_ref, p_scr, imp_scr, sel_scr, ocmp_scr, acc_slc, acc_win, alibi_scr, u_scr, flag_ref,
                 *, hpg, nc, ncp, nb, n_sel):
    g = pl.program_id(0)
    i = pl.program_id(1)
    t0 = i * QB
    w = hpg * QB
    nq = QB // LANES
    ninf = -jnp.inf

    q = q_ref[...]
    qs = jnp.concatenate([q[:, h * HEAD_DIM:(h + 1) * HEAD_DIM] for h in range(hpg)], axis=0)
    slopes = [slopes_ref[g * hpg + h] for h in range(hpg)]

    rc_i = (lax.broadcasted_iota(jnp.int32, (WIN_KEYS, QB), 1)
            - lax.broadcasted_iota(jnp.int32, (WIN_KEYS, QB), 0))

    @pl.when(i == 0)
    def _():
        rc_f = (-rc_i).astype(F32)
        for h in range(hpg):
            alibi_scr[h] = slopes[h] * rc_f

    def cmp_branch(rows):
        s = lax.dot_general(kcmp_ref[0:rows, :], qs, _NT, preferred_element_type=F32)
        c_iota = lax.broadcasted_iota(jnp.int32, (rows, QB), 0)
        q_iota = lax.broadcasted_iota(jnp.int32, (rows, QB), 1)
        cmp_end = c_iota * CMP_STRIDE + (CMP_BLOCK - 1)
        mask_c = (cmp_end <= t0 + q_iota) & (c_iota < nc)
        rel_c = (cmp_end - t0).astype(F32)
        psum = jnp.zeros((rows, QB), F32)
        for h in range(hpg):
            sh = s[:, h * QB:(h + 1) * QB] + slopes[h] * rel_c
            sh = jnp.where(mask_c, sh, ninf)
            m = jnp.maximum(jnp.max(sh, axis=0, keepdims=True), NEG_INF)
            p = jnp.exp2(sh - m)
            l = jnp.sum(p, axis=0, keepdims=True)
            pn = p * jnp.where(l > 0.0, 1.0 / l, 0.0)
            psum = psum + pn
            ocmp_scr[:, h * QB:(h + 1) * QB] = lax.dot_general(vcmp_ref[0:rows, :], pn.astype(BF16), _TN,
                                                                preferred_element_type=F32)
        for hq in range(nq):
            p_scr[hq, SUBLANES:SUBLANES + rows, :] = psum[:, hq * LANES:(hq + 1) * LANES]
            if rows < ncp:
                p_scr[hq, SUBLANES + rows:SUBLANES + ncp, :] = jnp.zeros((ncp - rows, LANES), F32)

    n_ended = (t0 + QB - CMP_BLOCK) // CMP_STRIDE + 1
    variant = (n_ended - 1) // CMP_ROWS

    def cmp_dispatch(k):
        rows = (k + 1) * CMP_ROWS
        if rows >= ncp:
            cmp_branch(ncp)
        else:
            lax.cond(variant <= k, lambda: cmp_branch(rows), lambda: cmp_dispatch(k + 1))

    cmp_dispatch(0)

    n_grp = nb // SUBLANES
    n_iota = lax.broadcasted_iota(jnp.int32, (nb, LANES), 0)
    row8 = lax.broadcasted_iota(jnp.int32, (SUBLANES, LANES), 0)
    vals, valids = [], []
    for hq in range(nq):
        p_scr[hq, 0:SUBLANES, :] = jnp.zeros((SUBLANES, LANES), F32)
        imp = p_scr[hq, pl.ds(SUBLANES - 1, nb, stride=4), :]
        for j in range(4):
            imp = imp + p_scr[hq, pl.ds(SUBLANES + j, nb, stride=4), :]
        tq = t0 + hq * LANES + lax.broadcasted_iota(jnp.int32, (nb, LANES), 1)
        cur = lax.shift_right_logical(tq, 6)
        valid = n_iota <= cur
        back = cur - n_iota
        forced = (n_iota == 0) | ((back >= 0) & (back < N_LOCAL_FORCED))
        imp = jnp.where(forced & valid, FORCED_SCORE, imp)
        imp = jnp.where(valid, imp, NEG_INF)
        imp_scr[hq] = imp
        vals.append([imp[j * SUBLANES:(j + 1) * SUBLANES, :] for j in range(n_grp)])
        valids.append(valid)

    def count(cnts, hq, mi, j):
        vm = jnp.broadcast_to(imp_scr[hq, pl.ds(mi, 1), :], (SUBLANES, LANES))
        v = vals[hq][j]
        lo = j * SUBLANES
        if lo + SUBLANES - 1 < mi:
            beats = vm > v
        elif lo > mi:
            beats = vm >= v
        else:
            beats = (vm > v) | ((vm == v) & (row8 > mi - lo))
        cnts[hq * n_grp + j] = cnts[hq * n_grp + j] + jnp.where(beats, 1.0, 0.0)

    def rank_shell(sh, cnts):
        cnts = list(cnts)
        for hq in range(nq):
            for mi in range(sh * SUBLANES, (sh + 1) * SUBLANES):
                for j in range(sh + 1):
                    count(cnts, hq, mi, j)
            for mi in range(sh * SUBLANES):
                count(cnts, hq, mi, sh)
        return tuple(cnts)

    last_valid = (t0 + QB - 1) // SLC_BLOCK
    cnts = tuple(jnp.zeros((SUBLANES, LANES), F32) for _ in range(nq * n_grp))
    for sh in range(n_grp):
        cnts = lax.cond(sh * SUBLANES <= last_valid, functools.partial(rank_shell, sh), lambda c: c, cnts)
    blocks_per_kt = KT // SLC_BLOCK
    any_sel = jnp.zeros((nb, LANES), F32)
    for hq in range(nq):
        cnt = jnp.concatenate(cnts[hq * n_grp:(hq + 1) * n_grp], axis=0)
        sel = (cnt < float(n_sel)) & valids[hq]
        sel_scr[hq] = jnp.where(sel, 0.0, ninf)
        any_sel = jnp.maximum(any_sel, jnp.where(sel, 1.0, 0.0))
    for kt in range(nb // blocks_per_kt):
        hit = jnp.max(any_sel[kt * blocks_per_kt:(kt + 1) * blocks_per_kt, :])
        flag_ref[kt] = (hit > 0.5).astype(jnp.int32)

    init = (tuple(jnp.full((1, QB), NEG_INF, F32) for _ in range(hpg)),
            tuple(jnp.zeros((1, QB), F32) for _ in range(hpg)))

    def stage_a(kt, slot):
        k0 = pl.multiple_of(kt * KT, KT)
        st = lax.dot_general(ks_ref[pl.ds(k0, KT), :], qs, _NT, preferred_element_type=F32)
        rows = [jnp.concatenate([jnp.broadcast_to(sel_scr[hq, pl.ds(kt * blocks_per_kt + b, 1), :],
                                                  (SLC_BLOCK, LANES)) for hq in range(nq)], axis=1)
                for b in range(blocks_per_kt)]
        diff = (t0 - k0) + rc_i[0:KT, :]
        base = jnp.where(diff >= 0, jnp.concatenate(rows, axis=0), ninf)
        off = (k0 - t0).astype(F32)
        tile_max = []
        for h in range(hpg):
            u = st[:, h * QB:(h + 1) * QB] + (alibi_scr[h, 0:KT, :] + base)
            u_scr[slot, h] = u
            tile_max.append(jnp.max(u, axis=0, keepdims=True) + slopes[h] * off)
        return tuple(tile_max)

    def stage_b(kt, slot, tile_max, ms, ls):
        k0 = pl.multiple_of(kt * KT, KT)
        off = (k0 - t0).astype(F32)
        new_m, new_l, alphas, ps = [], [], [], []
        for h in range(hpg):
            mh = jnp.maximum(ms[h], tile_max[h])
            alpha = jnp.exp2(ms[h] - mh)
            p = jnp.exp2(u_scr[slot, h] - (mh - slopes[h] * off))
            new_m.append(mh)
            new_l.append(alpha * ls[h] + jnp.sum(p, axis=0, keepdims=True))
            alphas.append(alpha)
            ps.append(p.astype(BF16))
        pv = lax.dot_general(vs_ref[pl.ds(k0, KT), :], jnp.concatenate(ps, axis=1), _TN,
                             preferred_element_type=F32)
        acc_slc[...] = acc_slc[...] * jnp.concatenate(alphas, axis=1) + pv
        return tuple(new_m), tuple(new_l)

    def slc_step(kt, carry):
        def visit(c):
            ms, ls, tile_max, prev, slot = c
            ms, ls = stage_b(prev, slot, tile_max, ms, ls)
            return ms, ls, stage_a(kt, 1 - slot), kt, 1 - slot
        return lax.cond(flag_ref[kt] > 0, visit, lambda c: c, carry)

    kd = t0 // KT
    acc_slc[...] = jnp.zeros_like(acc_slc)
    zero = jnp.int32(0)
    ms, ls, tile_max, prev, slot = lax.fori_loop(1, kd + 1, slc_step, init + (stage_a(zero, zero), zero, zero))
    _, l_slc = stage_b(prev, slot, tile_max, ms, ls)

    w0 = pl.multiple_of(jnp.maximum(t0 - WINDOW, 0), KB)
    diff_w = (t0 - w0) + rc_i
    base_w = jnp.where((diff_w >= 0) & (diff_w < WINDOW), 0.0, ninf)
    st_w = lax.dot_general(kw_ref[pl.ds(w0, WIN_KEYS), :], qs, _NT, preferred_element_type=F32)
    l_win, ps = [], []
    for h in range(hpg):
        u = st_w[:, h * QB:(h + 1) * QB] + (alibi_scr[h] + base_w)
        p = jnp.exp2(u - jnp.max(u, axis=0, keepdims=True))
        l_win.append(jnp.sum(p, axis=0, keepdims=True))
        ps.append(p.astype(BF16))
    acc_win[...] = lax.dot_general(vw_ref[pl.ds(w0, WIN_KEYS), :], jnp.concatenate(ps, axis=1), _TN,
                                   preferred_element_type=F32)

    for h in range(hpg):
        cols = slice(h * QB, (h + 1) * QB)
        g_cmp = jax.nn.sigmoid(gl_ref[0, h:h + 1, :])
        g_slc = jax.nn.sigmoid(gl_ref[1, h:h + 1, :])
        g_win = jax.nn.sigmoid(gl_ref[2, h:h + 1, :])
        oT = (g_cmp * ocmp_scr[:, cols] + (g_slc / l_slc[h]) * acc_slc[:, cols]
              + (g_win / l_win[h]) * acc_win[:, cols])
        o_ref[:, h * HEAD_DIM:(h + 1) * HEAD_DIM] = oT.T.astype(o_ref.dtype)


def sparse_attention(slopes, q, cmp_kv, kv_rest, gate_logits_t, *, nc):
    t, hd = q.shape
    _, n_groups, ncp, _ = cmp_kv.shape
    hpg = hd // HEAD_DIM // n_groups
    nb = t // SLC_BLOCK
    assert t % KT == 0 and t >= WIN_KEYS and ncp == 4 * nb and nb % SUBLANES == 0
    assert KT <= WIN_KEYS and KT % QB == 0 and QB % LANES == 0
    assert (CMP_BLOCK, CMP_STRIDE, SLC_BLOCK) == (32, 16, 64)
    n_sel = min(N_SELECT, nb)
    w = hpg * QB
    kern = functools.partial(_attn_kernel, hpg=hpg, nc=nc, ncp=ncp, nb=nb, n_sel=n_sel)
    kv_col = lambda part: (lambda g, i: (0, part * n_groups + g))
    return pl.pallas_call(
        kern,
        grid=(n_groups, t // QB),
        in_specs=[
            pl.BlockSpec(memory_space=pltpu.SMEM),
            pl.BlockSpec((QB, hpg * HEAD_DIM), lambda g, i: (i, g)),
            pl.BlockSpec((None, None, ncp, HEAD_DIM), lambda g, i: (0, g, 0, 0)),
            pl.BlockSpec((None, None, ncp, HEAD_DIM), lambda g, i: (1, g, 0, 0)),
            pl.BlockSpec((t, HEAD_DIM), kv_col(0)),
            pl.BlockSpec((t, HEAD_DIM), kv_col(1)),
            pl.BlockSpec((t, HEAD_DIM), kv_col(2)),
            pl.BlockSpec((t, HEAD_DIM), kv_col(3)),
            pl.BlockSpec((3, None, hpg, QB), lambda g, i: (0, g, 0, i)),
        ],
        out_specs=pl.BlockSpec((QB, hpg * HEAD_DIM), lambda g, i: (i, g)),
        out_shape=jax.ShapeDtypeStruct((t, hd), BF16),
        scratch_shapes=[
            pltpu.VMEM((QB // LANES, SUBLANES + ncp, LANES), F32),
            pltpu.VMEM((QB // LANES, nb, LANES), F32),
            pltpu.VMEM((QB // LANES, nb, LANES), F32),
            pltpu.VMEM((HEAD_DIM, w), F32),
            pltpu.VMEM((HEAD_DIM, w), F32),
            pltpu.VMEM((HEAD_DIM, w), F32),
            pltpu.VMEM((hpg, WIN_KEYS, QB), F32),
            pltpu.VMEM((2, hpg, KT, QB), F32),
            pltpu.SMEM((t // KT,), jnp.int32),
        ],
        name="sparse_attention",
        compiler_params=_params("parallel", "arbitrary"),
    )(slopes, q, cmp_kv, cmp_kv, kv_rest, kv_rest, kv_rest, kv_rest, gate_logits_t)


def _inproj_conv_kernel(a_ref, wb_ref, wc_ref, wh_ref, cw_ref, o_ref, halo_ref):
    i = pl.program_id(0)
    j = pl.program_id(1)
    a = a_ref[...]
    b = jnp.dot(a, wb_ref[...], preferred_element_type=F32)
    u = (jnp.dot(a, wc_ref[...], preferred_element_type=F32)
         * jnp.dot(a, wh_ref[...], preferred_element_type=F32))
    wk = cw_ref[...]
    w0, w1, w2 = wk[0:1, :], wk[1:2, :], wk[2:3, :]
    tm = u.shape[0]
    y = b * (w0 * pltpu.roll(u, 2, 0) + w1 * pltpu.roll(u, 1, 0) + w2 * u)
    o_ref[...] = y.astype(o_ref.dtype)
    @pl.when(i == 0)
    def _():
        halo_ref[j] = jnp.zeros(halo_ref.shape[1:], F32)

    ue = jnp.concatenate([halo_ref[j], u[0:SUBLANES, :]], axis=0)
    n2 = 2 * SUBLANES
    u1 = pltpu.roll(ue, 1, 0)[SUBLANES:n2, :]
    u2 = pltpu.roll(ue, 2, 0)[SUBLANES:n2, :]
    y0 = b[0:SUBLANES, :] * (w0 * u2 + w1 * u1 + w2 * u[0:SUBLANES, :])
    o_ref[0:SUBLANES, :] = y0.astype(o_ref.dtype)
    halo_ref[j] = u[tm - SUBLANES:tm, :]


def inproj_short_conv(a, w, conv_w, *, col0, cw, tm=1024, tn=512):
    m, k = a.shape
    tm, tn = _tile(m, tm), _tile(cw, tn)
    while col0 % tn:
        tn = _tile(cw, tn - 1)
    ncb = cw // tn
    jb = col0 // tn
    wcol = lambda part: (lambda i, j: (0, jb + part * ncb + j))
    return pl.pallas_call(
        _inproj_conv_kernel,
        grid=(m // tm, ncb),
        in_specs=[
            pl.BlockSpec((tm, k), lambda i, j: (i, 0)),
            pl.BlockSpec((k, tn), wcol(0)),
            pl.BlockSpec((k, tn), wcol(1)),
            pl.BlockSpec((k, tn), wcol(2)),
            pl.BlockSpec((CONV_K, tn), lambda i, j: (0, j)),
        ],
        out_specs=pl.BlockSpec((tm, tn), lambda i, j: (i, j)),
        out_shape=jax.ShapeDtypeStruct((m, cw), BF16),
        scratch_shapes=[pltpu.VMEM((ncb, SUBLANES, tn), F32)],
        name="inproj_short_conv",
        compiler_params=_params("arbitrary", "arbitrary"),
    )(a, w, w, w, conv_w)


def _mem_kernel(x_ref, gpre_ref, gpost_ref, wq_ref, km_ref, vm_ref, wo_ref, o_ref):
    x = x_ref[...]
    xn = _rms(x, gpre_ref[...]).astype(BF16)
    qm = jnp.dot(xn, wq_ref[...], preferred_element_type=F32) * (MEM_HEAD_DIM ** -0.5)
    outs = []
    for h in range(MEM_HEADS):
        cols = slice(h * MEM_HEAD_DIM, (h + 1) * MEM_HEAD_DIM)
        s = lax.dot_general(qm[:, cols].astype(BF16), km_ref[:, cols], _NT, preferred_element_type=F32)
        p = jnp.exp(s - jnp.max(s, axis=-1, keepdims=True))
        p = p * (1.0 / jnp.sum(p, axis=-1, keepdims=True))
        outs.append(jnp.dot(p.astype(BF16), vm_ref[:, cols], preferred_element_type=F32).astype(BF16))
    y = jnp.dot(jnp.concatenate(outs, axis=1), wo_ref[...], preferred_element_type=F32)
    o_ref[...] = x + _rms(y, gpost_ref[...])


def memory_sublayer(x, gpre, gpost, wq, km, vm, wo, tm=512):
    m, d = x.shape
    mw = wq.shape[1]
    nm = km.shape[0]
    tm = _tile(m, tm)
    full = lambda i: (0, 0)
    return pl.pallas_call(
        _mem_kernel,
        grid=(m // tm,),
        in_specs=[
            pl.BlockSpec((tm, d), lambda i: (i, 0)),
            pl.BlockSpec((1, d), full),
            pl.BlockSpec((1, d), full),
            pl.BlockSpec((d, mw), full),
            pl.BlockSpec((nm, mw), full),
            pl.BlockSpec((nm, mw), full),
            pl.BlockSpec((mw, d), full),
        ],
        out_specs=pl.BlockSpec((tm, d), lambda i: (i, 0)),
        out_shape=jax.ShapeDtypeStruct((m, d), F32),
        name="memory_sublayer",
        compiler_params=_params("parallel"),
    )(x, gpre.reshape(1, d), gpost.reshape(1, d), wq, km, vm, wo)


def _mlp_kernel(x_ref, gpre_ref, gpost_ref, wu_ref, wd_ref, o_ref, xn_scr, *, nf):
    f = pl.program_id(1)

    n_chunks = x_ref.shape[0] // NORM_ROWS

    @pl.when(f == 0)
    def _():
        def pre(r, carry):
            rows = pl.ds(pl.multiple_of(r * NORM_ROWS, NORM_ROWS), NORM_ROWS)
            xn_scr[rows, :] = _rms(x_ref[rows, :], gpre_ref[...]).astype(BF16)
            o_ref[rows, :] = jnp.zeros((NORM_ROWS, o_ref.shape[1]), F32)
            return carry
        lax.fori_loop(0, n_chunks, pre, 0)

    hid = jnp.dot(xn_scr[...], wu_ref[...], preferred_element_type=F32)
    hid = jnp.square(jnp.maximum(hid, 0.0))
    o_ref[...] += jnp.dot(hid.astype(BF16), wd_ref[...], preferred_element_type=F32)

    @pl.when(f == nf - 1)
    def _():
        def post(r, carry):
            rows = pl.ds(pl.multiple_of(r * NORM_ROWS, NORM_ROWS), NORM_ROWS)
            o_ref[rows, :] = x_ref[rows, :] + _rms(o_ref[rows, :], gpost_ref[...])
            return carry
        lax.fori_loop(0, n_chunks, post, 0)


def mlp_sublayer(x, gpre, gpost, w_up, w_down, tm=1024, tf=512):
    m, d = x.shape
    dff = w_up.shape[1]
    tm, tf = _tile(m, tm), _tile(dff, tf)
    nf = dff // tf
    w_up = w_up.astype(BF16)
    w_down = w_down.astype(BF16)
    return pl.pallas_call(
        functools.partial(_mlp_kernel, nf=nf),
        grid=(m // tm, nf),
        in_specs=[
            pl.BlockSpec((tm, d), lambda i, f: (i, 0), pipeline_mode=pl.Buffered(1)),
            pl.BlockSpec((1, d), lambda i, f: (0, 0)),
            pl.BlockSpec((1, d), lambda i, f: (0, 0)),
            pl.BlockSpec((d, tf), lambda i, f: (0, f)),
            pl.BlockSpec((tf, d), lambda i, f: (f, 0)),
        ],
        out_specs=pl.BlockSpec((tm, d), lambda i, f: (i, 0), pipeline_mode=pl.Buffered(1)),
        out_shape=jax.ShapeDtypeStruct((m, d), F32),
        scratch_shapes=[pltpu.VMEM((tm, d), BF16)],
        name="mlp_sublayer",
        compiler_params=pltpu.CompilerParams(dimension_semantics=("parallel", "arbitrary"),
                                             vmem_limit_bytes=MLP_VMEM_LIMIT),
    )(x, gpre.reshape(1, d), gpost.reshape(1, d), w_up, w_down)


def _mixer(x, norm_pre, norm_post, w_in, cmp_k_pe, cmp_k_w1, cmp_k_w2, cmp_v_pe, cmp_v_w1, cmp_v_w2,
           conv_w, w_out):
    t, d = x.shape
    attn_w = d // 2
    conv_cw = d - attn_w
    n_heads = attn_w // HEAD_DIM
    n_groups = n_heads // 4
    kvw = n_groups * HEAD_DIM
    o_kv = attn_w
    o_gate = o_kv + 6 * kvw
    o_conv = o_gate + 3 * n_heads
    assert w_in.shape[1] == o_conv + 3 * conv_cw

    assert o_gate % LANES == 0 and o_gate + LANES <= w_in.shape[1]
    w_bf = w_in.astype(BF16)
    w_conv = w_in[:, o_conv:].astype(BF16)

    xn = rmsnorm_cast(x, norm_pre)
    q = matmul(xn, w_bf, BF16, scale=HEAD_DIM ** -0.5 * LOG2E, col0=0, n=attn_w)
    kvc = matmul(xn, w_bf, F32, col0=o_kv, n=2 * kvw)
    kv_rest = matmul(xn, w_bf, BF16, col0=o_kv + 2 * kvw, n=4 * kvw)
    o_conv_out = inproj_short_conv(xn, w_conv, conv_w, col0=0, cw=conv_cw)
    gate_logits = matmul(xn, w_bf, F32, col0=o_gate, n=LANES)[:, :3 * n_heads]

    nc = (t - CMP_BLOCK) // CMP_STRIDE + 1
    pe = jnp.stack([cmp_k_pe, cmp_v_pe])
    w1 = jnp.stack([cmp_k_w1, cmp_v_w1]).astype(BF16)
    w2 = jnp.stack([cmp_k_w2, cmp_v_w2]).astype(BF16)
    cmp_kv = compress_tokens(kvc, pe, w1, w2, n_groups)

    gl_t = gate_logits.T.reshape(3, n_groups, n_heads // n_groups, t)
    idx = jnp.arange(1, n_heads + 1, dtype=F32)
    slopes = jnp.exp2(-8.0 * idx / n_heads) * LOG2E
    o_attn = sparse_attention(slopes, q, cmp_kv, kv_rest, gl_t, nc=nc)
    return matmul_postnorm_residual(o_attn, o_conv_out, w_out.astype(BF16), x, norm_post)


def _memory(x, mem, norm_pre, norm_kv, norm_post, wq, wk, wv, wo):
    mn = rmsnorm_cast(mem, norm_kv)
    km = matmul(mn, wk.astype(BF16), BF16)
    vm = matmul(mn, wv.astype(BF16), BF16)
    return memory_sublayer(x, norm_pre, norm_post, wq.astype(BF16), km, vm, wo.astype(BF16))


def kernel(x, mem, mix_norm_pre, mix_norm_post, w_in, cmp_k_pe, cmp_k_w1, cmp_k_w2, cmp_v_pe, cmp_v_w1,
           cmp_v_w2, conv_w, w_out, mem_norm_pre, mem_norm_kv, mem_norm_post, w_mem_q, w_mem_k, w_mem_v,
           w_mem_o, mlp_norm_pre, mlp_norm_post, w_up, w_down):
    b, t, d = x.shape
    assert b == 1
    h = x[0]
    m = mem[0]
    for l in range(w_in.shape[0]):
        h = _mixer(h, mix_norm_pre[l], mix_norm_post[l], w_in[l], cmp_k_pe[l], cmp_k_w1[l], cmp_k_w2[l],
                   cmp_v_pe[l], cmp_v_w1[l], cmp_v_w2[l], conv_w[l], w_out[l])
        h = _memory(h, m, mem_norm_pre[l], mem_norm_kv[l], mem_norm_post[l], w_mem_q[l], w_mem_k[l],
                    w_mem_v[l], w_mem_o[l])
        h = mlp_sublayer(h, mlp_norm_pre[l], mlp_norm_post[l], w_up[l], w_down[l])
    return h[None]
```

```python
import functools

import jax
import jax.numpy as jnp
from jax import lax
from jax.experimental import pallas as pl
from jax.experimental.pallas import tpu as pltpu

HEAD_DIM = 128
CMP_BLOCK = 32
CMP_STRIDE = 16
SLC_BLOCK = 64
N_SELECT = 16
N_LOCAL_FORCED = 2
WINDOW = 512
CONV_K = 3
MEM_HEADS = 4
MEM_HEAD_DIM = 128
RMS_EPS = 1e-6
NEG_INF = -1e30
FORCED_SCORE = 1e9
LOG2E = 1.4426950408889634

V7X_VMEM_BYTES = 64 * 1024 * 1024
VMEM_LIMIT = V7X_VMEM_BYTES - 8 * 1024 * 1024
MLP_VMEM_LIMIT = V7X_VMEM_BYTES - 4 * 1024 * 1024
NORM_ROWS = 64
LANES = 128
SUBLANES = 8

QB = 256
KB = 128
KT = 512
CMP_ROWS = 128
WIN_KEYS = WINDOW + QB

F32 = jnp.float32
BF16 = jnp.bfloat16

_NT = (((1,), (1,)), ((), ()))
_TN = (((0,), (0,)), ((), ()))


def _tile(n, pref):
    t = min(n, pref)
    while n % t:
        t -= 1
    return t


def _params(*sem):
    return pltpu.CompilerParams(dimension_semantics=sem, vmem_limit_bytes=VMEM_LIMIT)


def _rms(x, gain):
    return x * lax.rsqrt(jnp.mean(x * x, axis=-1, keepdims=True) + RMS_EPS) * gain


def _rmsnorm_kernel(x_ref, g_ref, o_ref):
    o_ref[...] = _rms(x_ref[...].astype(F32), g_ref[...]).astype(o_ref.dtype)


def rmsnorm_cast(x, gain, out_dtype=BF16):
    m, d = x.shape
    tm = _tile(m, 256)
    return pl.pallas_call(
        _rmsnorm_kernel,
        grid=(m // tm,),
        in_specs=[pl.BlockSpec((tm, d), lambda i: (i, 0)), pl.BlockSpec((1, d), lambda i: (0, 0))],
        out_specs=pl.BlockSpec((tm, d), lambda i: (i, 0)),
        out_shape=jax.ShapeDtypeStruct((m, d), out_dtype),
        name="rmsnorm_cast",
        compiler_params=_params("parallel"),
    )(x, gain.reshape(1, d))


def _mm_kernel(a_ref, w_ref, o_ref, *, scale):
    acc = jnp.dot(a_ref[...], w_ref[...], preferred_element_type=F32)
    if scale is not None:
        acc = acc * scale
    o_ref[...] = acc.astype(o_ref.dtype)


def matmul(a, w, out_dtype, scale=None, col0=0, n=None, tm=1024, tn=1024):
    m, k = a.shape
    n = w.shape[1] - col0 if n is None else n
    tm, tn = _tile(m, tm), _tile(n, tn)
    while col0 % tn:
        tn = _tile(n, tn - 1)
    jb = col0 // tn
    return pl.pallas_call(
        functools.partial(_mm_kernel, scale=scale),
        grid=(m // tm, n // tn),
        in_specs=[pl.BlockSpec((tm, k), lambda i, j: (i, 0)), pl.BlockSpec((k, tn), lambda i, j: (0, j + jb))],
        out_specs=pl.BlockSpec((tm, tn), lambda i, j: (i, j)),
        out_shape=jax.ShapeDtypeStruct((m, n), out_dtype),
        name="matmul",
        compiler_params=_params("parallel", "arbitrary"),
    )(a, w)


def _mm_postnorm_kernel(a1_ref, a2_ref, w_ref, x_ref, g_ref, o_ref, *, nk1, nk):
    k = pl.program_id(1)

    @pl.when(k == 0)
    def _():
        o_ref[...] = jnp.dot(a1_ref[...], w_ref[...], preferred_element_type=F32)

    @pl.when((k > 0) & (k < nk1))
    def _():
        o_ref[...] += jnp.dot(a1_ref[...], w_ref[...], preferred_element_type=F32)

    @pl.when(k >= nk1)
    def _():
        o_ref[...] += jnp.dot(a2_ref[...], w_ref[...], preferred_element_type=F32)

    @pl.when(k == nk - 1)
    def _():
        def post(r, carry):
            rows = pl.ds(pl.multiple_of(r * NORM_ROWS, NORM_ROWS), NORM_ROWS)
            o_ref[rows, :] = x_ref[rows, :] + _rms(o_ref[rows, :], g_ref[...])
            return carry
        lax.fori_loop(0, o_ref.shape[0] // NORM_ROWS, post, 0)


def matmul_postnorm_residual(a1, a2, w, x, gain, tm=512, tk=1024):
    m, k1 = a1.shape
    k2 = a2.shape[1]
    _, n = w.shape
    tm = _tile(m, tm)
    tk = _tile(k1, tk)
    while k2 % tk:
        tk = _tile(k1, tk - 1)
    nk1, nk = k1 // tk, (k1 + k2) // tk
    return pl.pallas_call(
        functools.partial(_mm_postnorm_kernel, nk1=nk1, nk=nk),
        grid=(m // tm, nk),
        in_specs=[
            pl.BlockSpec((tm, tk), lambda i, k: (i, jnp.minimum(k, nk1 - 1))),
            pl.BlockSpec((tm, tk), lambda i, k: (i, jnp.maximum(k - nk1, 0))),
            pl.BlockSpec((tk, n), lambda i, k: (k, 0)),
            pl.BlockSpec((tm, n), lambda i, k: (i, 0)),
            pl.BlockSpec((1, n), lambda i, k: (0, 0)),
        ],
        out_specs=pl.BlockSpec((tm, n), lambda i, k: (i, 0)),
        out_shape=jax.ShapeDtypeStruct((m, n), F32),
        name="outproj_postnorm",
        compiler_params=_params("parallel", "arbitrary"),
    )(a1, a2, w, x, gain.reshape(1, n))


def _gelu_tanh(x):
    c = 0.7978845608028654
    return x * (0.5 * (1.0 + jnp.tanh(c * (x + 0.044715 * (x * x * x)))))


def _compress_kernel(x_ref, pe_ref, w1_ref, w2_ref, o_ref, *, ncp):
    a = jnp.zeros((ncp, HEAD_DIM), F32)
    b = jnp.zeros((ncp, HEAD_DIM), F32)
    for l in range(CMP_STRIDE):
        xl = x_ref[pl.ds(l, ncp, stride=CMP_STRIDE), :]
        lo, hi = l, CMP_STRIDE + l
        a = a + jnp.dot((xl + pe_ref[lo:lo + 1, :]).astype(BF16), w1_ref[lo * HEAD_DIM:(lo + 1) * HEAD_DIM, :],
                        preferred_element_type=F32)
        b = b + jnp.dot((xl + pe_ref[hi:hi + 1, :]).astype(BF16), w1_ref[hi * HEAD_DIM:(hi + 1) * HEAD_DIM, :],
                        preferred_element_type=F32)
    hid = _gelu_tanh(a + pltpu.roll(b, ncp - 1, 0))
    o_ref[...] = jnp.dot(hid.astype(BF16), w2_ref[...], preferred_element_type=F32).astype(o_ref.dtype)


def compress_tokens(kvc, pe, w1, w2, n_groups):
    t = kvc.shape[0]
    ncp = t // CMP_STRIDE
    assert CMP_BLOCK == 2 * CMP_STRIDE
    return pl.pallas_call(
        functools.partial(_compress_kernel, ncp=ncp),
        grid=(2, n_groups),
        in_specs=[
            pl.BlockSpec((t, HEAD_DIM), lambda s, j: (0, s * n_groups + j)),
            pl.BlockSpec((None, CMP_BLOCK, HEAD_DIM), lambda s, j: (s, 0, 0)),
            pl.BlockSpec((None, CMP_BLOCK * HEAD_DIM, HEAD_DIM), lambda s, j: (s, 0, 0)),
            pl.BlockSpec((None, HEAD_DIM, HEAD_DIM), lambda s, j: (s, 0, 0)),
        ],
        out_specs=pl.BlockSpec((None, None, ncp, HEAD_DIM), lambda s, j: (s, j, 0, 0)),
        out_shape=jax.ShapeDtypeStruct((2, n_groups, ncp, HEAD_DIM), BF16),
        name="compress_tokens",
        compiler_params=_params("parallel", "parallel"),
    )(kvc, pe, w1, w2)


def _attn_kernel(slopes_ref, q_ref, kcmp_ref, vcmp_ref, ks_ref, vs_ref, kw_ref, vw_ref, gl_ref,
                 o_ref, p_scr, imp_scr, sel_scr, ocmp_scr, acc_slc, acc_win, alibi_scr, u_scr, flag_ref,
                 *, hpg, nc, ncp, nb, n_sel):
    g = pl.program_id(0)
    i = pl.program_id(1)
    t0 = i * QB
    w = hpg * QB
    nq = QB // LANES
    ninf = -jnp.inf

    q = q_ref[...]
    qs = jnp.concatenate([q[:, h * HEAD_DIM:(h + 1) * HEAD_DIM] for h in range(hpg)], axis=0)
    slopes = [slopes_ref[g * hpg + h] for h in range(hpg)]

    rc_i = (lax.broadcasted_iota(jnp.int32, (WIN_KEYS, QB), 1)
            - lax.broadcasted_iota(jnp.int32, (WIN_KEYS, QB), 0))

    @pl.when(i == 0)
    def _():
        rc_f = (-rc_i).astype(F32)
        for h in range(hpg):
            alibi_scr[h] = slopes[h] * rc_f

    def cmp_branch(rows):
        s = lax.dot_general(kcmp_ref[0:rows, :], qs, _NT, preferred_element_type=F32)
        c_iota = lax.broadcasted_iota(jnp.int32, (rows, QB), 0)
        q_iota = lax.broadcasted_iota(jnp.int32, (rows, QB), 1)
        cmp_end = c_iota * CMP_STRIDE + (CMP_BLOCK - 1)
        mask_c = (cmp_end <= t0 + q_iota) & (c_iota < nc)
        rel_c = (cmp_end - t0).astype(F32)
        psum = jnp.zeros((rows, QB), F32)
        for h in range(hpg):
            sh = s[:, h * QB:(h + 1) * QB] + slopes[h] * rel_c
            sh = jnp.where(mask_c, sh, ninf)
            m = jnp.maximum(jnp.max(sh, axis=0, keepdims=True), NEG_INF)
            p = jnp.exp2(sh - m)
            l = jnp.sum(p, axis=0, keepdims=True)
            pn = p * jnp.where(l > 0.0, 1.0 / l, 0.0)
            psum = psum + pn
            ocmp_scr[:, h * QB:(h + 1) * QB] = lax.dot_general(vcmp_ref[0:rows, :], pn.astype(BF16), _TN,
                                                                preferred_element_type=F32)
        for hq in range(nq):
            p_scr[hq, SUBLANES:SUBLANES + rows, :] = psum[:, hq * LANES:(hq + 1) * LANES]
            if rows < ncp:
                p_scr[hq, SUBLANES + rows:SUBLANES + ncp, :] = jnp.zeros((ncp - rows, LANES), F32)

    n_ended = (t0 + QB - CMP_BLOCK) // CMP_STRIDE + 1
    variant = (n_ended - 1) // CMP_ROWS

    def cmp_dispatch(k):
        rows = (k + 1) * CMP_ROWS
        if rows >= ncp:
            cmp_branch(ncp)
        else:
            lax.cond(variant <= k, lambda: cmp_branch(rows), lambda: cmp_dispatch(k + 1))

    cmp_dispatch(0)

    n_grp = nb // SUBLANES
    n_iota = lax.broadcasted_iota(jnp.int32, (nb, LANES), 0)
    row8 = lax.broadcasted_iota(jnp.int32, (SUBLANES, LANES), 0)
    vals, valids = [], []
    for hq in range(nq):
        p_scr[hq, 0:SUBLANES, :] = jnp.zeros((SUBLANES, LANES), F32)
        imp = p_scr[hq, pl.ds(SUBLANES - 1, nb, stride=4), :]
        for j in range(4):
            imp = imp + p_scr[hq, pl.ds(SUBLANES + j, nb, stride=4), :]
        tq = t0 + hq * LANES + lax.broadcasted_iota(jnp.int32, (nb, LANES), 1)
        cur = lax.shift_right_logical(tq, 6)
        valid = n_iota <= cur
        back = cur - n_iota
        forced = (n_iota == 0) | ((back >= 0) & (back < N_LOCAL_FORCED))
        imp = jnp.where(forced & valid, FORCED_SCORE, imp)
        imp = jnp.where(valid, imp, NEG_INF)
        imp_scr[hq] = imp
        vals.append([imp[j * SUBLANES:(j + 1) * SUBLANES, :] for j in range(n_grp)])
        valids.append(valid)

    def count(cnts, hq, mi, j):
        vm = jnp.broadcast_to(imp_scr[hq, pl.ds(mi, 1), :], (SUBLANES, LANES))
        v = vals[hq][j]
        lo = j * SUBLANES
        if lo + SUBLANES - 1 < mi:
            beats = vm > v
        elif lo > mi:
            beats = vm >= v
        else:
            beats = (vm > v) | ((vm == v) & (row8 > mi - lo))
        cnts[hq * n_grp + j] = cnts[hq * n_grp + j] + jnp.where(beats, 1.0, 0.0)

    def rank_shell(sh, cnts):
        cnts = list(cnts)
        for hq in range(nq):
            for mi in range(sh * SUBLANES, (sh + 1) * SUBLANES):
                for j in range(sh + 1):
                    count(cnts, hq, mi, j)
            for mi in range(sh * SUBLANES):
                count(cnts, hq, mi, sh)
        return tuple(cnts)

    last_valid = (t0 + QB - 1) // SLC_BLOCK
    cnts = tuple(jnp.zeros((SUBLANES, LANES), F32) for _ in range(nq * n_grp))
    for sh in range(n_grp):
        cnts = lax.cond(sh * SUBLANES <= last_valid, functools.partial(rank_shell, sh), lambda c: c, cnts)
    blocks_per_kt = KT // SLC_BLOCK
    any_sel = jnp.zeros((nb, LANES), F32)
    for hq in range(nq):
        cnt = jnp.concatenate(cnts[hq * n_grp:(hq + 1) * n_grp], axis=0)
        sel = (cnt < float(n_sel)) & valids[hq]
        sel_scr[hq] = jnp.where(sel, 0.0, ninf)
        any_sel = jnp.maximum(any_sel, jnp.where(sel, 1.0, 0.0))
    for kt in range(nb // blocks_per_kt):
        hit = jnp.max(any_sel[kt * blocks_per_kt:(kt + 1) * blocks_per_kt, :])
        flag_ref[kt] = (hit > 0.5).astype(jnp.int32)

    init = (tuple(jnp.full((1, QB), NEG_INF, F32) for _ in range(hpg)),
            tuple(jnp.zeros((1, QB), F32) for _ in range(hpg)))

    def stage_a(kt, slot):
        k0 = pl.multiple_of(kt * KT, KT)
        st = lax.dot_general(ks_ref[pl.ds(k0, KT), :], qs, _NT, preferred_element_type=F32)
        rows = [jnp.concatenate([jnp.broadcast_to(sel_scr[hq, pl.ds(kt * blocks_per_kt + b, 1), :],
                                                  (SLC_BLOCK, LANES)) for hq in range(nq)], axis=1)
                for b in range(blocks_per_kt)]
        diff = (t0 - k0) + rc_i[0:KT, :]
        base = jnp.where(diff >= 0, jnp.concatenate(rows, axis=0), ninf)
        off = (k0 - t0).astype(F32)
        tile_max = []
        for h in range(hpg):
            u = st[:, h * QB:(h + 1) * QB] + (alibi_scr[h, 0:KT, :] + base)
            u_scr[slot, h] = u
            tile_max.append(jnp.max(u, axis=0, keepdims=True) + slopes[h] * off)
        return tuple(tile_max)

    def stage_b(kt, slot, tile_max, ms, ls):
        k0 = pl.multiple_of(kt * KT, KT)
        off = (k0 - t0).astype(F32)
        new_m, new_l, alphas, ps = [], [], [], []
        for h in range(hpg):
            mh = jnp.maximum(ms[h], tile_max[h])
            alpha = jnp.exp2(ms[h] - mh)
            p = jnp.exp2(u_scr[slot, h] - (mh - slopes[h] * off))
            new_m.append(mh)
            new_l.append(alpha * ls[h] + jnp.sum(p, axis=0, keepdims=True))
            alphas.append(alpha)
            ps.append(p.astype(BF16))
        pv = lax.dot_general(vs_ref[pl.ds(k0, KT), :], jnp.concatenate(ps, axis=1), _TN,
                             preferred_element_type=F32)
        acc_slc[...] = acc_slc[...] * jnp.concatenate(alphas, axis=1) + pv
        return tuple(new_m), tuple(new_l)

    def slc_step(kt, carry):
        def visit(c):
            ms, ls, tile_max, prev, slot = c
            ms, ls = stage_b(prev, slot, tile_max, ms, ls)
            return ms, ls, stage_a(kt, 1 - slot), kt, 1 - slot
        return lax.cond(flag_ref[kt] > 0, visit, lambda c: c, carry)

    kd = t0 // KT
    acc_slc[...] = jnp.zeros_like(acc_slc)
    zero = jnp.int32(0)
    ms, ls, tile_max, prev, slot = lax.fori_loop(1, kd + 1, slc_step, init + (stage_a(zero, zero), zero, zero))
    _, l_slc = stage_b(prev, slot, tile_max, ms, ls)

    w0 = pl.multiple_of(jnp.maximum(t0 - WINDOW, 0), KB)
    diff_w = (t0 - w0) + rc_i
    base_w = jnp.where((diff_w >= 0) & (diff_w < WINDOW), 0.0, ninf)
    st_w = lax.dot_general(kw_ref[pl.ds(w0, WIN_KEYS), :], qs, _NT, preferred_element_type=F32)
    l_win, ps = [], []
    for h in range(hpg):
        u = st_w[:, h * QB:(h + 1) * QB] + (alibi_scr[h] + base_w)
        p = jnp.exp2(u - jnp.max(u, axis=0, keepdims=True))
        l_win.append(jnp.sum(p, axis=0, keepdims=True))
        ps.append(p.astype(BF16))
    acc_win[...] = lax.dot_general(vw_ref[pl.ds(w0, WIN_KEYS), :], jnp.concatenate(ps, axis=1), _TN,
                                   preferred_element_type=F32)

    for h in range(hpg):
        cols = slice(h * QB, (h + 1) * QB)
        g_cmp = jax.nn.sigmoid(gl_ref[0, h:h + 1, :])
        g_slc = jax.nn.sigmoid(gl_ref[1, h:h + 1, :])
        g_win = jax.nn.sigmoid(gl_ref[2, h:h + 1, :])
        oT = (g_cmp * ocmp_scr[:, cols] + (g_slc / l_slc[h]) * acc_slc[:, cols]
              + (g_win / l_win[h]) * acc_win[:, cols])
        o_ref[:, h * HEAD_DIM:(h + 1) * HEAD_DIM] = oT.T.astype(o_ref.dtype)


def sparse_attention(slopes, q, cmp_kv, kv_rest, gate_logits_t, *, nc):
    t, hd = q.shape
    _, n_groups, ncp, _ = cmp_kv.shape
    hpg = hd // HEAD_DIM // n_groups
    nb = t // SLC_BLOCK
    assert t % KT == 0 and t >= WIN_KEYS and ncp == 4 * nb and nb % SUBLANES == 0
    assert KT <= WIN_KEYS and KT % QB == 0 and QB % LANES == 0
    assert (CMP_BLOCK, CMP_STRIDE, SLC_BLOCK) == (32, 16, 64)
    n_sel = min(N_SELECT, nb)
    w = hpg * QB
    kern = functools.partial(_attn_kernel, hpg=hpg, nc=nc, ncp=ncp, nb=nb, n_sel=n_sel)
    kv_col = lambda part: (lambda g, i: (0, part * n_groups + g))
    return pl.pallas_call(
        kern,
        grid=(n_groups, t // QB),
        in_specs=[
            pl.BlockSpec(memory_space=pltpu.SMEM),
            pl.BlockSpec((QB, hpg * HEAD_DIM), lambda g, i: (i, g)),
            pl.BlockSpec((None, None, ncp, HEAD_DIM), lambda g, i: (0, g, 0, 0)),
            pl.BlockSpec((None, None, ncp, HEAD_DIM), lambda g, i: (1, g, 0, 0)),
            pl.BlockSpec((t, HEAD_DIM), kv_col(0)),
            pl.BlockSpec((t, HEAD_DIM), kv_col(1)),
            pl.BlockSpec((t, HEAD_DIM), kv_col(2)),
            pl.BlockSpec((t, HEAD_DIM), kv_col(3)),
            pl.BlockSpec((3, None, hpg, QB), lambda g, i: (0, g, 0, i)),
        ],
        out_specs=pl.BlockSpec((QB, hpg * HEAD_DIM), lambda g, i: (i, g)),
        out_shape=jax.ShapeDtypeStruct((t, hd), BF16),
        scratch_shapes=[
            pltpu.VMEM((QB // LANES, SUBLANES + ncp, LANES), F32),
            pltpu.VMEM((QB // LANES, nb, LANES), F32),
            pltpu.VMEM((QB // LANES, nb, LANES), F32),
            pltpu.VMEM((HEAD_DIM, w), F32),
            pltpu.VMEM((HEAD_DIM, w), F32),
            pltpu.VMEM((HEAD_DIM, w), F32),
            pltpu.VMEM((hpg, WIN_KEYS, QB), F32),
            pltpu.VMEM((2, hpg, KT, QB), F32),
            pltpu.SMEM((t // KT,), jnp.int32),
        ],
        name="sparse_attention",
        compiler_params=_params("parallel", "arbitrary"),
    )(slopes, q, cmp_kv, cmp_kv, kv_rest, kv_rest, kv_rest, kv_rest, gate_logits_t)


def _inproj_conv_kernel(a_ref, wb_ref, wc_ref, wh_ref, cw_ref, o_ref, halo_ref):
    i = pl.program_id(0)
    j = pl.program_id(1)
    a = a_ref[...]
    b = jnp.dot(a, wb_ref[...], preferred_element_type=F32)
    u = (jnp.dot(a, wc_ref[...], preferred_element_type=F32)
         * jnp.dot(a, wh_ref[...], preferred_element_type=F32))
    wk = cw_ref[...]
    w0, w1, w2 = wk[0:1, :], wk[1:2, :], wk[2:3, :]
    tm = u.shape[0]
    y = b * (w0 * pltpu.roll(u, 2, 0) + w1 * pltpu.roll(u, 1, 0) + w2 * u)
    o_ref[...] = y.astype(o_ref.dtype)
    @pl.when(i == 0)
    def _():
        halo_ref[j] = jnp.zeros(halo_ref.shape[1:], F32)

    ue = jnp.concatenate([halo_ref[j], u[0:SUBLANES, :]], axis=0)
    n2 = 2 * SUBLANES
    u1 = pltpu.roll(ue, 1, 0)[SUBLANES:n2, :]
    u2 = pltpu.roll(ue, 2, 0)[SUBLANES:n2, :]
    y0 = b[0:SUBLANES, :] * (w0 * u2 + w1 * u1 + w2 * u[0:SUBLANES, :])
    o_ref[0:SUBLANES, :] = y0.astype(o_ref.dtype)
    halo_ref[j] = u[tm - SUBLANES:tm, :]


def inproj_short_conv(a, w, conv_w, *, col0, cw, tm=1024, tn=512):
    m, k = a.shape
    tm, tn = _tile(m, tm), _tile(cw, tn)
    while col0 % tn:
        tn = _tile(cw, tn - 1)
    ncb = cw // tn
    jb = col0 // tn
    wcol = lambda part: (lambda i, j: (0, jb + part * ncb + j))
    return pl.pallas_call(
        _inproj_conv_kernel,
        grid=(m // tm, ncb),
        in_specs=[
            pl.BlockSpec((tm, k), lambda i, j: (i, 0)),
            pl.BlockSpec((k, tn), wcol(0)),
            pl.BlockSpec((k, tn), wcol(1)),
            pl.BlockSpec((k, tn), wcol(2)),
            pl.BlockSpec((CONV_K, tn), lambda i, j: (0, j)),
        ],
        out_specs=pl.BlockSpec((tm, tn), lambda i, j: (i, j)),
        out_shape=jax.ShapeDtypeStruct((m, cw), BF16),
        scratch_shapes=[pltpu.VMEM((ncb, SUBLANES, tn), F32)],
        name="inproj_short_conv",
        compiler_params=_params("arbitrary", "arbitrary"),
    )(a, w, w, w, conv_w)


def _mem_kernel(x_ref, gpre_ref, gpost_ref, wq_ref, km_ref, vm_ref, wo_ref, o_ref):
    x = x_ref[...]
    xn = _rms(x, gpre_ref[...]).astype(BF16)
    qm = jnp.dot(xn, wq_ref[...], preferred_element_type=F32) * (MEM_HEAD_DIM ** -0.5)
    outs = []
    for h in range(MEM_HEADS):
        cols = slice(h * MEM_HEAD_DIM, (h + 1) * MEM_HEAD_DIM)
        s = lax.dot_general(qm[:, cols].astype(BF16), km_ref[:, cols], _NT, preferred_element_type=F32)
        p = jnp.exp(s - jnp.max(s, axis=-1, keepdims=True))
        p = p * (1.0 / jnp.sum(p, axis=-1, keepdims=True))
        outs.append(jnp.dot(p.astype(BF16), vm_ref[:, cols], preferred_element_type=F32).astype(BF16))
    y = jnp.dot(jnp.concatenate(outs, axis=1), wo_ref[...], preferred_element_type=F32)
    o_ref[...] = x + _rms(y, gpost_ref[...])


def memory_sublayer(x, gpre, gpost, wq, km, vm, wo, tm=512):
    m, d = x.shape
    mw = wq.shape[1]
    nm = km.shape[0]
    tm = _tile(m, tm)
    full = lambda i: (0, 0)
    return pl.pallas_call(
        _mem_kernel,
        grid=(m // tm,),
        in_specs=[
            pl.BlockSpec((tm, d), lambda i: (i, 0)),
            pl.BlockSpec((1, d), full),
            pl.BlockSpec((1, d), full),
            pl.BlockSpec((d, mw), full),
            pl.BlockSpec((nm, mw), full),
            pl.BlockSpec((nm, mw), full),
            pl.BlockSpec((mw, d), full),
        ],
        out_specs=pl.BlockSpec((tm, d), lambda i: (i, 0)),
        out_shape=jax.ShapeDtypeStruct((m, d), F32),
        name="memory_sublayer",
        compiler_params=_params("parallel"),
    )(x, gpre.reshape(1, d), gpost.reshape(1, d), wq, km, vm, wo)


def _mlp_kernel(x_hbm, gpre_ref, gpost_ref, wu_ref, wd_ref, o_ref, xn_scr, xbuf, sem, *, nf):
    i = pl.program_id(0)
    f = pl.program_id(1)
    tm = o_ref.shape[0]
    n_chunks = tm // NORM_ROWS

    def x_copy(r, slot):
        row = pl.multiple_of(i * tm + r * NORM_ROWS, NORM_ROWS)
        return pltpu.make_async_copy(x_hbm.at[pl.ds(row, NORM_ROWS), :], xbuf.at[slot], sem.at[slot])

    def for_each_x_chunk(body):
        x_copy(0, 0).start()

        def step(r, carry):
            slot = r % 2

            @pl.when(r + 1 < n_chunks)
            def _():
                x_copy(r + 1, 1 - slot).start()

            x_copy(r, slot).wait()
            body(pl.ds(pl.multiple_of(r * NORM_ROWS, NORM_ROWS), NORM_ROWS), xbuf[slot])
            return carry

        lax.fori_loop(0, n_chunks, step, 0)

    @pl.when(f == 0)
    def _():
        def pre(rows, xc):
            xn_scr[rows, :] = _rms(xc, gpre_ref[...]).astype(BF16)
            o_ref[rows, :] = jnp.zeros((NORM_ROWS, o_ref.shape[1]), F32)
        for_each_x_chunk(pre)

    hid = jnp.dot(xn_scr[...], wu_ref[...], preferred_element_type=F32)
    hid = jnp.square(jnp.maximum(hid, 0.0))
    o_ref[...] += jnp.dot(hid.astype(BF16), wd_ref[...], preferred_element_type=F32)

    @pl.when(f == nf - 1)
    def _():
        def post(rows, xc):
            o_ref[rows, :] = xc + _rms(o_ref[rows, :], gpost_ref[...])
        for_each_x_chunk(post)


def mlp_sublayer(x, gpre, gpost, w_up, w_down, tm=1024, tf=512):
    m, d = x.shape
    dff = w_up.shape[1]
    tm, tf = _tile(m, tm), _tile(dff, tf)
    nf = dff // tf
    w_up = w_up.astype(BF16)
    w_down = w_down.astype(BF16)
    return pl.pallas_call(
        functools.partial(_mlp_kernel, nf=nf),
        grid=(m // tm, nf),
        in_specs=[
            pl.BlockSpec(memory_space=pl.ANY),
            pl.BlockSpec((1, d), lambda i, f: (0, 0)),
            pl.BlockSpec((1, d), lambda i, f: (0, 0)),
            pl.BlockSpec((d, tf), lambda i, f: (0, f)),
            pl.BlockSpec((tf, d), lambda i, f: (f, 0)),
        ],
        out_specs=pl.BlockSpec((tm, d), lambda i, f: (i, 0)),
        out_shape=jax.ShapeDtypeStruct((m, d), F32),
        scratch_shapes=[pltpu.VMEM((tm, d), BF16),
                        pltpu.VMEM((2, NORM_ROWS, d), F32),
                        pltpu.SemaphoreType.DMA((2,))],
        name="mlp_sublayer",
        compiler_params=pltpu.CompilerParams(dimension_semantics=("parallel", "arbitrary"),
                                             vmem_limit_bytes=MLP_VMEM_LIMIT),
    )(x, gpre.reshape(1, d), gpost.reshape(1, d), w_up, w_down)


def _mixer(x, norm_pre, norm_post, w_in, cmp_k_pe, cmp_k_w1, cmp_k_w2, cmp_v_pe, cmp_v_w1, cmp_v_w2,
           conv_w, w_out):
    t, d = x.shape
    attn_w = d // 2
    conv_cw = d - attn_w
    n_heads = attn_w // HEAD_DIM
    n_groups = n_heads // 4
    kvw = n_groups * HEAD_DIM
    o_kv = attn_w
    o_gate = o_kv + 6 * kvw
    o_conv = o_gate + 3 * n_heads
    assert w_in.shape[1] == o_conv + 3 * conv_cw

    assert o_gate % LANES == 0 and o_gate + LANES <= w_in.shape[1]
    w_bf = w_in.astype(BF16)
    w_conv = w_in[:, o_conv:].astype(BF16)

    xn = rmsnorm_cast(x, norm_pre)
    q = matmul(xn, w_bf, BF16, scale=HEAD_DIM ** -0.5 * LOG2E, col0=0, n=attn_w)
    kvc = matmul(xn, w_bf, F32, col0=o_kv, n=2 * kvw)
    kv_rest = matmul(xn, w_bf, BF16, col0=o_kv + 2 * kvw, n=4 * kvw)
    o_conv_out = inproj_short_conv(xn, w_conv, conv_w, col0=0, cw=conv_cw)
    gate_logits = matmul(xn, w_bf, F32, col0=o_gate, n=LANES)[:, :3 * n_heads]

    nc = (t - CMP_BLOCK) // CMP_STRIDE + 1
    pe = jnp.stack([cmp_k_pe, cmp_v_pe])
    w1 = jnp.stack([cmp_k_w1, cmp_v_w1]).astype(BF16)
    w2 = jnp.stack([cmp_k_w2, cmp_v_w2]).astype(BF16)
    cmp_kv = compress_tokens(kvc, pe, w1, w2, n_groups)

    gl_t = gate_logits.T.reshape(3, n_groups, n_heads // n_groups, t)
    idx = jnp.arange(1, n_heads + 1, dtype=F32)
    slopes = jnp.exp2(-8.0 * idx / n_heads) * LOG2E
    o_attn = sparse_attention(slopes, q, cmp_kv, kv_rest, gl_t, nc=nc)
    return matmul_postnorm_residual(o_attn, o_conv_out, w_out.astype(BF16), x, norm_post)


def _memory(x, mem, norm_pre, norm_kv, norm_post, wq, wk, wv, wo):
    mn = rmsnorm_cast(mem, norm_kv)
    km = matmul(mn, wk.astype(BF16), BF16)
    vm = matmul(mn, wv.astype(BF16), BF16)
    return memory_sublayer(x, norm_pre, norm_post, wq.astype(BF16), km, vm, wo.astype(BF16))


def kernel(x, mem, mix_norm_pre, mix_norm_post, w_in, cmp_k_pe, cmp_k_w1, cmp_k_w2, cmp_v_pe, cmp_v_w1,
           cmp_v_w2, conv_w, w_out, mem_norm_pre, mem_norm_kv, mem_norm_post, w_mem_q, w_mem_k, w_mem_v,
           w_mem_o, mlp_norm_pre, mlp_norm_post, w_up, w_down):
    b, t, d = x.shape
    assert b == 1
    h = x[0]
    m = mem[0]
    for l in range(w_in.shape[0]):
        h = _mixer(h, mix_norm_pre[l], mix_norm_post[l], w_in[l], cmp_k_pe[l], cmp_k_w1[l], cmp_k_w2[l],
                   cmp_v_pe[l], cmp_v_w1[l], cmp_v_w2[l], conv_w[l], w_out[l])
        h = _memory(h, m, mem_norm_pre[l], mem_norm_kv[l], mem_norm_post[l], w_mem_q[l], w_mem_k[l],
                    w_mem_v[l], w_mem_o[l])
        h = mlp_sublayer(h, mlp_norm_pre[l], mlp_norm_post[l], w_up[l], w_down[l])
    return h[None]
```

```python
import functools

import jax
import jax.numpy as jnp
from jax import lax
from jax.experimental import pallas as pl
from jax.experimental.pallas import tpu as pltpu

HEAD_DIM = 128
CMP_BLOCK = 32
CMP_STRIDE = 16
SLC_BLOCK = 64
N_SELECT = 16
N_LOCAL_FORCED = 2
WINDOW = 512
CONV_K = 3
MEM_HEADS = 4
MEM_HEAD_DIM = 128
RMS_EPS = 1e-6
NEG_INF = -1e30
FORCED_SCORE = 1e9
LOG2E = 1.4426950408889634

V7X_VMEM_BYTES = 64 * 1024 * 1024
VMEM_LIMIT = V7X_VMEM_BYTES - 8 * 1024 * 1024
MLP_VMEM_LIMIT = V7X_VMEM_BYTES - 4 * 1024 * 1024
NORM_ROWS = 64
LANES = 128
SUBLANES = 8

QB = 256
KB = 128
KT = 512
CMP_ROWS = 128
WIN_KEYS = WINDOW + QB

F32 = jnp.float32
BF16 = jnp.bfloat16

_NT = (((1,), (1,)), ((), ()))
_TN = (((0,), (0,)), ((), ()))


def _tile(n, pref):
    t = min(n, pref)
    while n % t:
        t -= 1
    return t


def _params(*sem):
    return pltpu.CompilerParams(dimension_semantics=sem, vmem_limit_bytes=VMEM_LIMIT)


def _rms(x, gain):
    return x * lax.rsqrt(jnp.mean(x * x, axis=-1, keepdims=True) + RMS_EPS) * gain


def _rmsnorm_kernel(x_ref, g_ref, o_ref):
    o_ref[...] = _rms(x_ref[...].astype(F32), g_ref[...]).astype(o_ref.dtype)


def rmsnorm_cast(x, gain, out_dtype=BF16):
    m, d = x.shape
    tm = _tile(m, 256)
    return pl.pallas_call(
        _rmsnorm_kernel,
        grid=(m // tm,),
        in_specs=[pl.BlockSpec((tm, d), lambda i: (i, 0)), pl.BlockSpec((1, d), lambda i: (0, 0))],
        out_specs=pl.BlockSpec((tm, d), lambda i: (i, 0)),
        out_shape=jax.ShapeDtypeStruct((m, d), out_dtype),
        name="rmsnorm_cast",
        compiler_params=_params("parallel"),
    )(x, gain.reshape(1, d))


def _mm_kernel(a_ref, w_ref, o_ref, *, scale):
    acc = jnp.dot(a_ref[...], w_ref[...], preferred_element_type=F32)
    if scale is not None:
        acc = acc * scale
    o_ref[...] = acc.astype(o_ref.dtype)


def matmul(a, w, out_dtype, scale=None, col0=0, n=None, tm=1024, tn=1024):
    m, k = a.shape
    n = w.shape[1] - col0 if n is None else n
    tm, tn = _tile(m, tm), _tile(n, tn)
    while col0 % tn:
        tn = _tile(n, tn - 1)
    jb = col0 // tn
    return pl.pallas_call(
        functools.partial(_mm_kernel, scale=scale),
        grid=(m // tm, n // tn),
        in_specs=[pl.BlockSpec((tm, k), lambda i, j: (i, 0)), pl.BlockSpec((k, tn), lambda i, j: (0, j + jb))],
        out_specs=pl.BlockSpec((tm, tn), lambda i, j: (i, j)),
        out_shape=jax.ShapeDtypeStruct((m, n), out_dtype),
        name="matmul",
        compiler_params=_params("parallel", "arbitrary"),
    )(a, w)


def _mm2_kernel(a1_ref, a2_ref, w_ref, o_ref):
    k1 = a1_ref.shape[1]
    o_ref[...] = (jnp.dot(a1_ref[...], w_ref[0:k1, :], preferred_element_type=F32)
                  + jnp.dot(a2_ref[...], w_ref[k1:, :], preferred_element_type=F32))


def matmul2(a1, a2, w, tm=1024, tn=1024):
    m, k1 = a1.shape
    k2 = a2.shape[1]
    n = w.shape[1]
    tm, tn = _tile(m, tm), _tile(n, tn)
    return pl.pallas_call(
        _mm2_kernel,
        grid=(m // tm, n // tn),
        in_specs=[pl.BlockSpec((tm, k1), lambda i, j: (i, 0)), pl.BlockSpec((tm, k2), lambda i, j: (i, 0)),
                  pl.BlockSpec((k1 + k2, tn), lambda i, j: (0, j))],
        out_specs=pl.BlockSpec((tm, tn), lambda i, j: (i, j)),
        out_shape=jax.ShapeDtypeStruct((m, n), F32),
        name="outproj",
        compiler_params=_params("parallel", "arbitrary"),
    )(a1, a2, w)


def _gelu_tanh(x):
    c = 0.7978845608028654
    return x * (0.5 * (1.0 + jnp.tanh(c * (x + 0.044715 * (x * x * x)))))


def _compress_kernel(x_ref, pe_ref, w1_ref, w2_ref, o_ref, *, ncp):
    a = jnp.zeros((ncp, HEAD_DIM), F32)
    b = jnp.zeros((ncp, HEAD_DIM), F32)
    for l in range(CMP_STRIDE):
        xl = x_ref[pl.ds(l, ncp, stride=CMP_STRIDE), :]
        lo, hi = l, CMP_STRIDE + l
        a = a + jnp.dot((xl + pe_ref[lo:lo + 1, :]).astype(BF16), w1_ref[lo * HEAD_DIM:(lo + 1) * HEAD_DIM, :],
                        preferred_element_type=F32)
        b = b + jnp.dot((xl + pe_ref[hi:hi + 1, :]).astype(BF16), w1_ref[hi * HEAD_DIM:(hi + 1) * HEAD_DIM, :],
                        preferred_element_type=F32)
    hid = _gelu_tanh(a + pltpu.roll(b, ncp - 1, 0))
    o_ref[...] = jnp.dot(hid.astype(BF16), w2_ref[...], preferred_element_type=F32).astype(o_ref.dtype)


def compress_tokens(kvc, pe, w1, w2, n_groups):
    t = kvc.shape[0]
    ncp = t // CMP_STRIDE
    assert CMP_BLOCK == 2 * CMP_STRIDE
    return pl.pallas_call(
        functools.partial(_compress_kernel, ncp=ncp),
        grid=(2, n_groups),
        in_specs=[
            pl.BlockSpec((t, HEAD_DIM), lambda s, j: (0, s * n_groups + j)),
            pl.BlockSpec((None, CMP_BLOCK, HEAD_DIM), lambda s, j: (s, 0, 0)),
            pl.BlockSpec((None, CMP_BLOCK * HEAD_DIM, HEAD_DIM), lambda s, j: (s, 0, 0)),
            pl.BlockSpec((None, HEAD_DIM, HEAD_DIM), lambda s, j: (s, 0, 0)),
        ],
        out_specs=pl.BlockSpec((None, None, ncp, HEAD_DIM), lambda s, j: (s, j, 0, 0)),
        out_shape=jax.ShapeDtypeStruct((2, n_groups, ncp, HEAD_DIM), BF16),
        name="compress_tokens",
        compiler_params=_params("parallel", "parallel"),
    )(kvc, pe, w1, w2)


def _attn_kernel(slopes_ref, q_ref, kcmp_ref, vcmp_ref, ks_ref, vs_ref, kw_ref, vw_ref, gl_ref,
                 o_ref, p_scr, imp_scr, sel_scr, ocmp_scr, acc_slc, acc_win, alibi_scr, u_scr, flag_ref,
                 *, hpg, nc, ncp, nb, n_sel):
    g = pl.program_id(0)
    i = pl.program_id(1)
    t0 = i * QB
    w = hpg * QB
    nq = QB // LANES
    ninf = -jnp.inf

    q = q_ref[...]
    qs = jnp.concatenate([q[:, h * HEAD_DIM:(h + 1) * HEAD_DIM] for h in range(hpg)], axis=0)
    slopes = [slopes_ref[g * hpg + h] for h in range(hpg)]

    rc_i = (lax.broadcasted_iota(jnp.int32, (WIN_KEYS, QB), 1)
            - lax.broadcasted_iota(jnp.int32, (WIN_KEYS, QB), 0))

    @pl.when(i == 0)
    def _():
        rc_f = (-rc_i).astype(F32)
        for h in range(hpg):
            alibi_scr[h] = slopes[h] * rc_f

    def cmp_branch(rows):
        s = lax.dot_general(kcmp_ref[0:rows, :], qs, _NT, preferred_element_type=F32)
        c_iota = lax.broadcasted_iota(jnp.int32, (rows, QB), 0)
        q_iota = lax.broadcasted_iota(jnp.int32, (rows, QB), 1)
        cmp_end = c_iota * CMP_STRIDE + (CMP_BLOCK - 1)
        mask_c = (cmp_end <= t0 + q_iota) & (c_iota < nc)
        rel_c = (cmp_end - t0).astype(F32)
        psum = jnp.zeros((rows, QB), F32)
        for h in range(hpg):
            sh = s[:, h * QB:(h + 1) * QB] + slopes[h] * rel_c
            sh = jnp.where(mask_c, sh, ninf)
            m = jnp.maximum(jnp.max(sh, axis=0, keepdims=True), NEG_INF)
            p = jnp.exp2(sh - m)
            l = jnp.sum(p, axis=0, keepdims=True)
            pn = p * jnp.where(l > 0.0, 1.0 / l, 0.0)
            psum = psum + pn
            ocmp_scr[:, h * QB:(h + 1) * QB] = lax.dot_general(vcmp_ref[0:rows, :], pn.astype(BF16), _TN,
                                                                preferred_element_type=F32)
        for hq in range(nq):
            p_scr[hq, SUBLANES:SUBLANES + rows, :] = psum[:, hq * LANES:(hq + 1) * LANES]
            if rows < ncp:
                p_scr[hq, SUBLANES + rows:SUBLANES + ncp, :] = jnp.zeros((ncp - rows, LANES), F32)

    n_ended = (t0 + QB - CMP_BLOCK) // CMP_STRIDE + 1
    variant = (n_ended - 1) // CMP_ROWS

    def cmp_dispatch(k):
        rows = (k + 1) * CMP_ROWS
        if rows >= ncp:
            cmp_branch(ncp)
        else:
            lax.cond(variant <= k, lambda: cmp_branch(rows), lambda: cmp_dispatch(k + 1))

    cmp_dispatch(0)

    n_grp = nb // SUBLANES
    n_iota = lax.broadcasted_iota(jnp.int32, (nb, LANES), 0)
    row8 = lax.broadcasted_iota(jnp.int32, (SUBLANES, LANES), 0)
    vals, valids = [], []
    for hq in range(nq):
        p_scr[hq, 0:SUBLANES, :] = jnp.zeros((SUBLANES, LANES), F32)
        imp = p_scr[hq, pl.ds(SUBLANES - 1, nb, stride=4), :]
        for j in range(4):
            imp = imp + p_scr[hq, pl.ds(SUBLANES + j, nb, stride=4), :]
        tq = t0 + hq * LANES + lax.broadcasted_iota(jnp.int32, (nb, LANES), 1)
        cur = lax.shift_right_logical(tq, 6)
        valid = n_iota <= cur
        back = cur - n_iota
        forced = (n_iota == 0) | ((back >= 0) & (back < N_LOCAL_FORCED))
        imp = jnp.where(forced & valid, FORCED_SCORE, imp)
        imp = jnp.where(valid, imp, NEG_INF)
        imp_scr[hq] = imp
        vals.append([imp[j * SUBLANES:(j + 1) * SUBLANES, :] for j in range(n_grp)])
        valids.append(valid)

    def count(cnts, hq, mi, j):
        vm = jnp.broadcast_to(imp_scr[hq, pl.ds(mi, 1), :], (SUBLANES, LANES))
        v = vals[hq][j]
        lo = j * SUBLANES
        if lo + SUBLANES - 1 < mi:
            beats = vm > v
        elif lo > mi:
            beats = vm >= v
        else:
            beats = (vm > v) | ((vm == v) & (row8 > mi - lo))
        cnts[hq * n_grp + j] = cnts[hq * n_grp + j] + jnp.where(beats, 1.0, 0.0)

    def rank_shell(sh, cnts):
        cnts = list(cnts)
        for hq in range(nq):
            for mi in range(sh * SUBLANES, (sh + 1) * SUBLANES):
                for j in range(sh + 1):
                    count(cnts, hq, mi, j)
            for mi in range(sh * SUBLANES):
                count(cnts, hq, mi, sh)
        return tuple(cnts)

    last_valid = (t0 + QB - 1) // SLC_BLOCK
    cnts = tuple(jnp.zeros((SUBLANES, LANES), F32) for _ in range(nq * n_grp))
    for sh in range(n_grp):
        cnts = lax.cond(sh * SUBLANES <= last_valid, functools.partial(rank_shell, sh), lambda c: c, cnts)
    blocks_per_kt = KT // SLC_BLOCK
    any_sel = jnp.zeros((nb, LANES), F32)
    for hq in range(nq):
        cnt = jnp.concatenate(cnts[hq * n_grp:(hq + 1) * n_grp], axis=0)
        sel = (cnt < float(n_sel)) & valids[hq]
        sel_scr[hq] = jnp.where(sel, 0.0, ninf)
        any_sel = jnp.maximum(any_sel, jnp.where(sel, 1.0, 0.0))
    for kt in range(nb // blocks_per_kt):
        hit = jnp.max(any_sel[kt * blocks_per_kt:(kt + 1) * blocks_per_kt, :])
        flag_ref[kt] = (hit > 0.5).astype(jnp.int32)

    init = (tuple(jnp.full((1, QB), NEG_INF, F32) for _ in range(hpg)),
            tuple(jnp.zeros((1, QB), F32) for _ in range(hpg)))

    def stage_a(kt, slot):
        k0 = pl.multiple_of(kt * KT, KT)
        st = lax.dot_general(ks_ref[pl.ds(k0, KT), :], qs, _NT, preferred_element_type=F32)
        rows = [jnp.concatenate([jnp.broadcast_to(sel_scr[hq, pl.ds(kt * blocks_per_kt + b, 1), :],
                                                  (SLC_BLOCK, LANES)) for hq in range(nq)], axis=1)
                for b in range(blocks_per_kt)]
        diff = (t0 - k0) + rc_i[0:KT, :]
        base = jnp.where(diff >= 0, jnp.concatenate(rows, axis=0), ninf)
        off = (k0 - t0).astype(F32)
        tile_max = []
        for h in range(hpg):
            u = st[:, h * QB:(h + 1) * QB] + (alibi_scr[h, 0:KT, :] + base)
            u_scr[slot, h] = u
            tile_max.append(jnp.max(u, axis=0, keepdims=True) + slopes[h] * off)
        return tuple(tile_max)

    def stage_b(kt, slot, tile_max, ms, ls):
        k0 = pl.multiple_of(kt * KT, KT)
        off = (k0 - t0).astype(F32)
        new_m, new_l, alphas, ps = [], [], [], []
        for h in range(hpg):
            mh = jnp.maximum(ms[h], tile_max[h])
            alpha = jnp.exp2(ms[h] - mh)
            p = jnp.exp2(u_scr[slot, h] - (mh - slopes[h] * off))
            new_m.append(mh)
            new_l.append(alpha * ls[h] + jnp.sum(p, axis=0, keepdims=True))
            alphas.append(alpha)
            ps.append(p.astype(BF16))
        pv = lax.dot_general(vs_ref[pl.ds(k0, KT), :], jnp.concatenate(ps, axis=1), _TN,
                             preferred_element_type=F32)
        acc_slc[...] = acc_slc[...] * jnp.concatenate(alphas, axis=1) + pv
        return tuple(new_m), tuple(new_l)

    def slc_step(kt, carry):
        def visit(c):
            ms, ls, tile_max, prev, slot = c
            ms, ls = stage_b(prev, slot, tile_max, ms, ls)
            return ms, ls, stage_a(kt, 1 - slot), kt, 1 - slot
        return lax.cond(flag_ref[kt] > 0, visit, lambda c: c, carry)

    kd = t0 // KT
    acc_slc[...] = jnp.zeros_like(acc_slc)
    zero = jnp.int32(0)
    ms, ls, tile_max, prev, slot = lax.fori_loop(1, kd + 1, slc_step, init + (stage_a(zero, zero), zero, zero))
    _, l_slc = stage_b(prev, slot, tile_max, ms, ls)

    w0 = pl.multiple_of(jnp.maximum(t0 - WINDOW, 0), KB)
    diff_w = (t0 - w0) + rc_i
    base_w = jnp.where((diff_w >= 0) & (diff_w < WINDOW), 0.0, ninf)
    st_w = lax.dot_general(kw_ref[pl.ds(w0, WIN_KEYS), :], qs, _NT, preferred_element_type=F32)
    l_win, ps = [], []
    for h in range(hpg):
        u = st_w[:, h * QB:(h + 1) * QB] + (alibi_scr[h] + base_w)
        p = jnp.exp2(u - jnp.max(u, axis=0, keepdims=True))
        l_win.append(jnp.sum(p, axis=0, keepdims=True))
        ps.append(p.astype(BF16))
    acc_win[...] = lax.dot_general(vw_ref[pl.ds(w0, WIN_KEYS), :], jnp.concatenate(ps, axis=1), _TN,
                                   preferred_element_type=F32)

    for h in range(hpg):
        cols = slice(h * QB, (h + 1) * QB)
        g_cmp = jax.nn.sigmoid(gl_ref[0, h:h + 1, :])
        g_slc = jax.nn.sigmoid(gl_ref[1, h:h + 1, :])
        g_win = jax.nn.sigmoid(gl_ref[2, h:h + 1, :])
        oT = (g_cmp * ocmp_scr[:, cols] + (g_slc / l_slc[h]) * acc_slc[:, cols]
              + (g_win / l_win[h]) * acc_win[:, cols])
        o_ref[:, h * HEAD_DIM:(h + 1) * HEAD_DIM] = oT.T.astype(o_ref.dtype)


def sparse_attention(slopes, q, cmp_kv, kv_rest, gate_logits_t, *, nc):
    t, hd = q.shape
    _, n_groups, ncp, _ = cmp_kv.shape
    hpg = hd // HEAD_DIM // n_groups
    nb = t // SLC_BLOCK
    assert t % KT == 0 and t >= WIN_KEYS and ncp == 4 * nb and nb % SUBLANES == 0
    assert KT <= WIN_KEYS and KT % QB == 0 and QB % LANES == 0
    assert (CMP_BLOCK, CMP_STRIDE, SLC_BLOCK) == (32, 16, 64)
    n_sel = min(N_SELECT, nb)
    w = hpg * QB
    kern = functools.partial(_attn_kernel, hpg=hpg, nc=nc, ncp=ncp, nb=nb, n_sel=n_sel)
    kv_col = lambda part: (lambda g, i: (0, part * n_groups + g))
    return pl.pallas_call(
        kern,
        grid=(n_groups, t // QB),
        in_specs=[
            pl.BlockSpec(memory_space=pltpu.SMEM),
            pl.BlockSpec((QB, hpg * HEAD_DIM), lambda g, i: (i, g)),
            pl.BlockSpec((None, None, ncp, HEAD_DIM), lambda g, i: (0, g, 0, 0)),
            pl.BlockSpec((None, None, ncp, HEAD_DIM), lambda g, i: (1, g, 0, 0)),
            pl.BlockSpec((t, HEAD_DIM), kv_col(0)),
            pl.BlockSpec((t, HEAD_DIM), kv_col(1)),
            pl.BlockSpec((t, HEAD_DIM), kv_col(2)),
            pl.BlockSpec((t, HEAD_DIM), kv_col(3)),
            pl.BlockSpec((3, None, hpg, QB), lambda g, i: (0, g, 0, i)),
        ],
        out_specs=pl.BlockSpec((QB, hpg * HEAD_DIM), lambda g, i: (i, g)),
        out_shape=jax.ShapeDtypeStruct((t, hd), BF16),
        scratch_shapes=[
            pltpu.VMEM((QB // LANES, SUBLANES + ncp, LANES), F32),
            pltpu.VMEM((QB // LANES, nb, LANES), F32),
            pltpu.VMEM((QB // LANES, nb, LANES), F32),
            pltpu.VMEM((HEAD_DIM, w), F32),
            pltpu.VMEM((HEAD_DIM, w), F32),
            pltpu.VMEM((HEAD_DIM, w), F32),
            pltpu.VMEM((hpg, WIN_KEYS, QB), F32),
            pltpu.VMEM((2, hpg, KT, QB), F32),
            pltpu.SMEM((t // KT,), jnp.int32),
        ],
        name="sparse_attention",
        compiler_params=_params("parallel", "arbitrary"),
    )(slopes, q, cmp_kv, cmp_kv, kv_rest, kv_rest, kv_rest, kv_rest, gate_logits_t)


def _inproj_conv_kernel(a_ref, wb_ref, wc_ref, wh_ref, cw_ref, o_ref, halo_ref):
    i = pl.program_id(0)
    j = pl.program_id(1)
    a = a_ref[...]
    b = jnp.dot(a, wb_ref[...], preferred_element_type=F32)
    u = (jnp.dot(a, wc_ref[...], preferred_element_type=F32)
         * jnp.dot(a, wh_ref[...], preferred_element_type=F32))
    wk = cw_ref[...]
    w0, w1, w2 = wk[0:1, :], wk[1:2, :], wk[2:3, :]
    tm = u.shape[0]
    y = b * (w0 * pltpu.roll(u, 2, 0) + w1 * pltpu.roll(u, 1, 0) + w2 * u)
    o_ref[...] = y.astype(o_ref.dtype)
    @pl.when(i == 0)
    def _():
        halo_ref[j] = jnp.zeros(halo_ref.shape[1:], F32)

    ue = jnp.concatenate([halo_ref[j], u[0:SUBLANES, :]], axis=0)
    n2 = 2 * SUBLANES
    u1 = pltpu.roll(ue, 1, 0)[SUBLANES:n2, :]
    u2 = pltpu.roll(ue, 2, 0)[SUBLANES:n2, :]
    y0 = b[0:SUBLANES, :] * (w0 * u2 + w1 * u1 + w2 * u[0:SUBLANES, :])
    o_ref[0:SUBLANES, :] = y0.astype(o_ref.dtype)
    halo_ref[j] = u[tm - SUBLANES:tm, :]


def inproj_short_conv(a, w, conv_w, *, col0, cw, tm=1024, tn=512):
    m, k = a.shape
    tm, tn = _tile(m, tm), _tile(cw, tn)
    while col0 % tn:
        tn = _tile(cw, tn - 1)
    ncb = cw // tn
    jb = col0 // tn
    wcol = lambda part: (lambda i, j: (0, jb + part * ncb + j))
    return pl.pallas_call(
        _inproj_conv_kernel,
        grid=(m // tm, ncb),
        in_specs=[
            pl.BlockSpec((tm, k), lambda i, j: (i, 0)),
            pl.BlockSpec((k, tn), wcol(0)),
            pl.BlockSpec((k, tn), wcol(1)),
            pl.BlockSpec((k, tn), wcol(2)),
            pl.BlockSpec((CONV_K, tn), lambda i, j: (0, j)),
        ],
        out_specs=pl.BlockSpec((tm, tn), lambda i, j: (i, j)),
        out_shape=jax.ShapeDtypeStruct((m, cw), BF16),
        scratch_shapes=[pltpu.VMEM((ncb, SUBLANES, tn), F32)],
        name="inproj_short_conv",
        compiler_params=_params("arbitrary", "arbitrary"),
    )(a, w, w, w, conv_w)


def _mem_kernel(x0_ref, y0_ref, g0_ref, gpre_ref, gpost_ref, wq_ref, km_ref, vm_ref, wo_ref, o_ref):
    x = x0_ref[...] + _rms(y0_ref[...], g0_ref[...])
    xn = _rms(x, gpre_ref[...]).astype(BF16)
    qm = jnp.dot(xn, wq_ref[...], preferred_element_type=F32) * (MEM_HEAD_DIM ** -0.5)
    outs = []
    for h in range(MEM_HEADS):
        cols = slice(h * MEM_HEAD_DIM, (h + 1) * MEM_HEAD_DIM)
        s = lax.dot_general(qm[:, cols].astype(BF16), km_ref[:, cols], _NT, preferred_element_type=F32)
        p = jnp.exp(s - jnp.max(s, axis=-1, keepdims=True))
        p = p * (1.0 / jnp.sum(p, axis=-1, keepdims=True))
        outs.append(jnp.dot(p.astype(BF16), vm_ref[:, cols], preferred_element_type=F32).astype(BF16))
    y = jnp.dot(jnp.concatenate(outs, axis=1), wo_ref[...], preferred_element_type=F32)
    o_ref[...] = x + _rms(y, gpost_ref[...])


def memory_sublayer(x0, y0, g0, gpre, gpost, wq, kvm, wo, tm=256):
    m, d = x0.shape
    mw = wq.shape[1]
    nm = kvm.shape[0]
    tm = _tile(m, tm)
    full = lambda i: (0, 0)
    return pl.pallas_call(
        _mem_kernel,
        grid=(m // tm,),
        in_specs=[
            pl.BlockSpec((tm, d), lambda i: (i, 0)),
            pl.BlockSpec((tm, d), lambda i: (i, 0)),
            pl.BlockSpec((1, d), full),
            pl.BlockSpec((1, d), full),
            pl.BlockSpec((1, d), full),
            pl.BlockSpec((d, mw), full),
            pl.BlockSpec((nm, mw), full),
            pl.BlockSpec((nm, mw), lambda i: (0, 1)),
            pl.BlockSpec((mw, d), full),
        ],
        out_specs=pl.BlockSpec((tm, d), lambda i: (i, 0)),
        out_shape=jax.ShapeDtypeStruct((m, d), F32),
        name="memory_sublayer",
        compiler_params=_params("parallel"),
    )(x0, y0, g0.reshape(1, d), gpre.reshape(1, d), gpost.reshape(1, d), wq, kvm, kvm, wo)


def _mlp_kernel(x_ref, gpre_ref, gpost_ref, wu_ref, wd_ref, o_ref, xn_scr, *, nf):
    f = pl.program_id(1)

    n_chunks = x_ref.shape[0] // NORM_ROWS

    @pl.when(f == 0)
    def _():
        def pre(r, carry):
            rows = pl.ds(pl.multiple_of(r * NORM_ROWS, NORM_ROWS), NORM_ROWS)
            xn_scr[rows, :] = _rms(x_ref[rows, :], gpre_ref[...]).astype(BF16)
            o_ref[rows, :] = jnp.zeros((NORM_ROWS, o_ref.shape[1]), F32)
            return carry
        lax.fori_loop(0, n_chunks, pre, 0)

    hid = jnp.dot(xn_scr[...], wu_ref[...], preferred_element_type=F32)
    hid = jnp.square(jnp.maximum(hid, 0.0))
    o_ref[...] += jnp.dot(hid.astype(BF16), wd_ref[...], preferred_element_type=F32)

    @pl.when(f == nf - 1)
    def _():
        def post(r, carry):
            rows = pl.ds(pl.multiple_of(r * NORM_ROWS, NORM_ROWS), NORM_ROWS)
            o_ref[rows, :] = x_ref[rows, :] + _rms(o_ref[rows, :], gpost_ref[...])
            return carry
        lax.fori_loop(0, n_chunks, post, 0)


def mlp_sublayer(x, gpre, gpost, w_up, w_down, tm=1024, tf=512):
    m, d = x.shape
    dff = w_up.shape[1]
    tm, tf = _tile(m, tm), _tile(dff, tf)
    nf = dff // tf
    w_up = w_up.astype(BF16)
    w_down = w_down.astype(BF16)
    return pl.pallas_call(
        functools.partial(_mlp_kernel, nf=nf),
        grid=(m // tm, nf),
        in_specs=[
            pl.BlockSpec((tm, d), lambda i, f: (i, 0), pipeline_mode=pl.Buffered(1)),
            pl.BlockSpec((1, d), lambda i, f: (0, 0)),
            pl.BlockSpec((1, d), lambda i, f: (0, 0)),
            pl.BlockSpec((d, tf), lambda i, f: (0, f)),
            pl.BlockSpec((tf, d), lambda i, f: (f, 0)),
        ],
        out_specs=pl.BlockSpec((tm, d), lambda i, f: (i, 0), pipeline_mode=pl.Buffered(1)),
        out_shape=jax.ShapeDtypeStruct((m, d), F32),
        scratch_shapes=[pltpu.VMEM((tm, d), BF16)],
        name="mlp_sublayer",
        compiler_params=pltpu.CompilerParams(dimension_semantics=("parallel", "arbitrary"),
                                             vmem_limit_bytes=MLP_VMEM_LIMIT),
    )(x, gpre.reshape(1, d), gpost.reshape(1, d), w_up, w_down)


def _mixer(x, norm_pre, w_in, cmp_k_pe, cmp_k_w1, cmp_k_w2, cmp_v_pe, cmp_v_w1, cmp_v_w2, conv_w, w_out):
    t, d = x.shape
    attn_w = d // 2
    conv_cw = d - attn_w
    n_heads = attn_w // HEAD_DIM
    n_groups = n_heads // 4
    kvw = n_groups * HEAD_DIM
    o_kv = attn_w
    o_gate = o_kv + 6 * kvw
    o_conv = o_gate + 3 * n_heads
    assert w_in.shape[1] == o_conv + 3 * conv_cw

    assert o_gate % LANES == 0 and o_gate + LANES <= w_in.shape[1]
    w_bf = w_in.astype(BF16)
    w_conv = w_in[:, o_conv:].astype(BF16)

    xn = rmsnorm_cast(x, norm_pre)
    q = matmul(xn, w_bf, BF16, scale=HEAD_DIM ** -0.5 * LOG2E, col0=0, n=attn_w)
    kvc = matmul(xn, w_bf, F32, col0=o_kv, n=2 * kvw)
    kv_rest = matmul(xn, w_bf, BF16, col0=o_kv + 2 * kvw, n=4 * kvw)
    o_conv_out = inproj_short_conv(xn, w_conv, conv_w, col0=0, cw=conv_cw)
    gate_logits = matmul(xn, w_bf, F32, col0=o_gate, n=LANES)[:, :3 * n_heads]

    nc = (t - CMP_BLOCK) // CMP_STRIDE + 1
    pe = jnp.stack([cmp_k_pe, cmp_v_pe])
    w1 = jnp.stack([cmp_k_w1, cmp_v_w1]).astype(BF16)
    w2 = jnp.stack([cmp_k_w2, cmp_v_w2]).astype(BF16)
    cmp_kv = compress_tokens(kvc, pe, w1, w2, n_groups)

    gl_t = gate_logits.T.reshape(3, n_groups, n_heads // n_groups, t)
    idx = jnp.arange(1, n_heads + 1, dtype=F32)
    slopes = jnp.exp2(-8.0 * idx / n_heads) * LOG2E
    o_attn = sparse_attention(slopes, q, cmp_kv, kv_rest, gl_t, nc=nc)
    return matmul2(o_attn, o_conv_out, w_out.astype(BF16))


def _memory(x0, y0, g0, mem, norm_pre, norm_kv, norm_post, wq, wk, wv, wo):
    mn = rmsnorm_cast(mem, norm_kv)
    kvm = matmul(mn, jnp.concatenate([wk, wv], axis=1).astype(BF16), BF16)
    return memory_sublayer(x0, y0, g0, norm_pre, norm_post, wq.astype(BF16), kvm, wo.astype(BF16))


def kernel(x, mem, mix_norm_pre, mix_norm_post, w_in, cmp_k_pe, cmp_k_w1, cmp_k_w2, cmp_v_pe, cmp_v_w1,
           cmp_v_w2, conv_w, w_out, mem_norm_pre, mem_norm_kv, mem_norm_post, w_mem_q, w_mem_k, w_mem_v,
           w_mem_o, mlp_norm_pre, mlp_norm_post, w_up, w_down):
    b, t, d = x.shape
    assert b == 1
    h = x[0]
    m = mem[0]
    for l in range(w_in.shape[0]):
        y = _mixer(h, mix_norm_pre[l], w_in[l], cmp_k_pe[l], cmp_k_w1[l], cmp_k_w2[l],
                   cmp_v_pe[l], cmp_v_w1[l], cmp_v_w2[l], conv_w[l], w_out[l])
        h = _memory(h, y, mix_norm_post[l], m, mem_norm_pre[l], mem_norm_kv[l], mem_norm_post[l],
                    w_mem_q[l], w_mem_k[l], w_mem_v[l], w_mem_o[l])
        h = mlp_sublayer(h, mlp_norm_pre[l], mlp_norm_post[l], w_up[l], w_down[l])
    return h[None]
```

```python
import functools

import jax
import jax.numpy as jnp
from jax import lax
from jax.experimental import pallas as pl
from jax.experimental.pallas import tpu as pltpu

HEAD_DIM = 128
CMP_BLOCK = 32
CMP_STRIDE = 16
SLC_BLOCK = 64
N_SELECT = 16
N_LOCAL_FORCED = 2
WINDOW = 512
CONV_K = 3
MEM_HEADS = 4
MEM_HEAD_DIM = 128
RMS_EPS = 1e-6
NEG_INF = -1e30
FORCED_SCORE = 1e9
LOG2E = 1.4426950408889634

V7X_VMEM_BYTES = 64 * 1024 * 1024
VMEM_LIMIT = V7X_VMEM_BYTES - 8 * 1024 * 1024
MLP_VMEM_LIMIT = V7X_VMEM_BYTES - 4 * 1024 * 1024
NORM_ROWS = 64
LANES = 128
SUBLANES = 8

QB = 256
KB = 128
KT = 512
CMP_ROWS = 128
WIN_KEYS = WINDOW + QB

F32 = jnp.float32
BF16 = jnp.bfloat16

_NT = (((1,), (1,)), ((), ()))
_TN = (((0,), (0,)), ((), ()))


def _tile(n, pref):
    t = min(n, pref)
    while n % t:
        t -= 1
    return t


def _params(*sem):
    return pltpu.CompilerParams(dimension_semantics=sem, vmem_limit_bytes=VMEM_LIMIT)


def _rms(x, gain):
    return x * lax.rsqrt(jnp.mean(x * x, axis=-1, keepdims=True) + RMS_EPS) * gain


def _rmsnorm_kernel(x_ref, g_ref, o_ref):
    o_ref[...] = _rms(x_ref[...].astype(F32), g_ref[...]).astype(o_ref.dtype)


def rmsnorm_cast(x, gain, out_dtype=BF16):
    m, d = x.shape
    tm = _tile(m, 256)
    return pl.pallas_call(
        _rmsnorm_kernel,
        grid=(m // tm,),
        in_specs=[pl.BlockSpec((tm, d), lambda i: (i, 0)), pl.BlockSpec((1, d), lambda i: (0, 0))],
        out_specs=pl.BlockSpec((tm, d), lambda i: (i, 0)),
        out_shape=jax.ShapeDtypeStruct((m, d), out_dtype),
        name="rmsnorm_cast",
        compiler_params=_params("parallel"),
    )(x, gain.reshape(1, d))


def _mm_kernel(a_ref, w_ref, o_ref, *, scale):
    acc = jnp.dot(a_ref[...], w_ref[...], preferred_element_type=F32)
    if scale is not None:
        acc = acc * scale
    o_ref[...] = acc.astype(o_ref.dtype)


def matmul(a, w, out_dtype, scale=None, col0=0, n=None, tm=1024, tn=1024):
    m, k = a.shape
    n = w.shape[1] - col0 if n is None else n
    tm, tn = _tile(m, tm), _tile(n, tn)
    while col0 % tn:
        tn = _tile(n, tn - 1)
    jb = col0 // tn
    return pl.pallas_call(
        functools.partial(_mm_kernel, scale=scale),
        grid=(m // tm, n // tn),
        in_specs=[pl.BlockSpec((tm, k), lambda i, j: (i, 0)), pl.BlockSpec((k, tn), lambda i, j: (0, j + jb))],
        out_specs=pl.BlockSpec((tm, tn), lambda i, j: (i, j)),
        out_shape=jax.ShapeDtypeStruct((m, n), out_dtype),
        name="matmul",
        compiler_params=_params("parallel", "arbitrary"),
    )(a, w)


def _mm2_kernel(a1_ref, a2_ref, w_ref, o_ref):
    k1 = a1_ref.shape[1]
    o_ref[...] = (jnp.dot(a1_ref[...], w_ref[0:k1, :], preferred_element_type=F32)
                  + jnp.dot(a2_ref[...], w_ref[k1:, :], preferred_element_type=F32))


def matmul2(a1, a2, w, tm=1024, tn=1024):
    m, k1 = a1.shape
    k2 = a2.shape[1]
    n = w.shape[1]
    tm, tn = _tile(m, tm), _tile(n, tn)
    return pl.pallas_call(
        _mm2_kernel,
        grid=(m // tm, n // tn),
        in_specs=[pl.BlockSpec((tm, k1), lambda i, j: (i, 0)), pl.BlockSpec((tm, k2), lambda i, j: (i, 0)),
                  pl.BlockSpec((k1 + k2, tn), lambda i, j: (0, j))],
        out_specs=pl.BlockSpec((tm, tn), lambda i, j: (i, j)),
        out_shape=jax.ShapeDtypeStruct((m, n), F32),
        name="outproj",
        compiler_params=_params("parallel", "arbitrary"),
    )(a1, a2, w)


def _gelu_tanh(x):
    c = 0.7978845608028654
    return x * (0.5 * (1.0 + jnp.tanh(c * (x + 0.044715 * (x * x * x)))))


def _compress_kernel(x_ref, pe_ref, w1_ref, w2_ref, o_ref, *, ncp):
    a = jnp.zeros((ncp, HEAD_DIM), F32)
    b = jnp.zeros((ncp, HEAD_DIM), F32)
    for l in range(CMP_STRIDE):
        xl = x_ref[pl.ds(l, ncp, stride=CMP_STRIDE), :]
        lo, hi = l, CMP_STRIDE + l
        a = a + jnp.dot((xl + pe_ref[lo:lo + 1, :]).astype(BF16), w1_ref[lo * HEAD_DIM:(lo + 1) * HEAD_DIM, :],
                        preferred_element_type=F32)
        b = b + jnp.dot((xl + pe_ref[hi:hi + 1, :]).astype(BF16), w1_ref[hi * HEAD_DIM:(hi + 1) * HEAD_DIM, :],
                        preferred_element_type=F32)
    hid = _gelu_tanh(a + pltpu.roll(b, ncp - 1, 0))
    o_ref[...] = jnp.dot(hid.astype(BF16), w2_ref[...], preferred_element_type=F32).astype(o_ref.dtype)


def compress_tokens(kvc, pe, w1, w2, n_groups):
    t = kvc.shape[0]
    ncp = t // CMP_STRIDE
    assert CMP_BLOCK == 2 * CMP_STRIDE
    return pl.pallas_call(
        functools.partial(_compress_kernel, ncp=ncp),
        grid=(2, n_groups),
        in_specs=[
            pl.BlockSpec((t, HEAD_DIM), lambda s, j: (0, s * n_groups + j)),
            pl.BlockSpec((None, CMP_BLOCK, HEAD_DIM), lambda s, j: (s, 0, 0)),
            pl.BlockSpec((None, CMP_BLOCK * HEAD_DIM, HEAD_DIM), lambda s, j: (s, 0, 0)),
            pl.BlockSpec((None, HEAD_DIM, HEAD_DIM), lambda s, j: (s, 0, 0)),
        ],
        out_specs=pl.BlockSpec((None, None, ncp, HEAD_DIM), lambda s, j: (s, j, 0, 0)),
        out_shape=jax.ShapeDtypeStruct((2, n_groups, ncp, HEAD_DIM), BF16),
        name="compress_tokens",
        compiler_params=_params("parallel", "parallel"),
    )(kvc, pe, w1, w2)


def _attn_kernel(slopes_ref, q_ref, kcmp_ref, vcmp_ref, ks_ref, vs_ref, kw_ref, vw_ref, gl_ref,
                 o_ref, p_scr, imp_scr, sel_scr, ocmp_scr, acc_slc, acc_win, alibi_scr, winb_scr, tri_scr, u_scr,
                 flag_ref,
                 *, hpg, nc, ncp, nb, n_sel):
    g = pl.program_id(0)
    i = pl.program_id(1)
    t0 = i * QB
    w = hpg * QB
    nq = QB // LANES
    ninf = -jnp.inf

    q = q_ref[...]
    qs = jnp.concatenate([q[:, h * HEAD_DIM:(h + 1) * HEAD_DIM] for h in range(hpg)], axis=0)
    slopes = [slopes_ref[g * hpg + h] for h in range(hpg)]

    rc_i = (lax.broadcasted_iota(jnp.int32, (WIN_KEYS, QB), 1)
            - lax.broadcasted_iota(jnp.int32, (WIN_KEYS, QB), 0))

    @pl.when(i == 0)
    def _():
        rc_f = (-rc_i).astype(F32)
        for h in range(hpg):
            alibi_scr[h] = slopes[h] * rc_f
        for o in range(WINDOW // QB + 1):
            dw = o * QB + rc_i
            bw = jnp.where((dw >= 0) & (dw < WINDOW), 0.0, ninf)
            for h in range(hpg):
                winb_scr[o, h] = slopes[h] * rc_f + bw
        for o in range(KT // QB):
            tri_scr[o] = jnp.where(o * QB + rc_i[0:KT, :] >= 0, 0.0, ninf)
        tri_scr[KT // QB] = jnp.zeros((KT, QB), F32)

    def cmp_branch(rows):
        s = lax.dot_general(kcmp_ref[0:rows, :], qs, _NT, preferred_element_type=F32)
        c_iota = lax.broadcasted_iota(jnp.int32, (rows, QB), 0)
        q_iota = lax.broadcasted_iota(jnp.int32, (rows, QB), 1)
        cmp_end = c_iota * CMP_STRIDE + (CMP_BLOCK - 1)
        mask_c = (cmp_end <= t0 + q_iota) & (c_iota < nc)
        rel_c = (cmp_end - t0).astype(F32)
        psum = jnp.zeros((rows, QB), F32)
        for h in range(hpg):
            sh = s[:, h * QB:(h + 1) * QB] + slopes[h] * rel_c
            sh = jnp.where(mask_c, sh, ninf)
            m = jnp.maximum(jnp.max(sh, axis=0, keepdims=True), NEG_INF)
            p = jnp.exp2(sh - m)
            l = jnp.sum(p, axis=0, keepdims=True)
            pn = p * jnp.where(l > 0.0, 1.0 / l, 0.0)
            psum = psum + pn
            ocmp_scr[:, h * QB:(h + 1) * QB] = lax.dot_general(vcmp_ref[0:rows, :], pn.astype(BF16), _TN,
                                                                preferred_element_type=F32)
        for hq in range(nq):
            p_scr[hq, SUBLANES:SUBLANES + rows, :] = psum[:, hq * LANES:(hq + 1) * LANES]
            if rows < ncp:
                p_scr[hq, SUBLANES + rows:SUBLANES + ncp, :] = jnp.zeros((ncp - rows, LANES), F32)

    n_ended = (t0 + QB - CMP_BLOCK) // CMP_STRIDE + 1
    variant = (n_ended - 1) // CMP_ROWS

    def cmp_dispatch(k):
        rows = (k + 1) * CMP_ROWS
        if rows >= ncp:
            cmp_branch(ncp)
        else:
            lax.cond(variant <= k, lambda: cmp_branch(rows), lambda: cmp_dispatch(k + 1))

    cmp_dispatch(0)

    n_grp = nb // SUBLANES
    n_iota = lax.broadcasted_iota(jnp.int32, (nb, LANES), 0)
    row8 = lax.broadcasted_iota(jnp.int32, (SUBLANES, LANES), 0)
    vals, valids = [], []
    for hq in range(nq):
        p_scr[hq, 0:SUBLANES, :] = jnp.zeros((SUBLANES, LANES), F32)
        imp = p_scr[hq, pl.ds(SUBLANES - 1, nb, stride=4), :]
        for j in range(4):
            imp = imp + p_scr[hq, pl.ds(SUBLANES + j, nb, stride=4), :]
        tq = t0 + hq * LANES + lax.broadcasted_iota(jnp.int32, (nb, LANES), 1)
        cur = lax.shift_right_logical(tq, 6)
        valid = n_iota <= cur
        back = cur - n_iota
        forced = (n_iota == 0) | ((back >= 0) & (back < N_LOCAL_FORCED))
        imp = jnp.where(forced & valid, FORCED_SCORE, imp)
        imp = jnp.where(valid, imp, NEG_INF)
        imp_scr[hq] = imp
        vals.append([imp[j * SUBLANES:(j + 1) * SUBLANES, :] for j in range(n_grp)])
        valids.append(valid)

    def count(cnts, hq, mi, j):
        vm = jnp.broadcast_to(imp_scr[hq, pl.ds(mi, 1), :], (SUBLANES, LANES))
        v = vals[hq][j]
        lo = j * SUBLANES
        if lo + SUBLANES - 1 < mi:
            beats = vm > v
        elif lo > mi:
            beats = vm >= v
        else:
            beats = (vm > v) | ((vm == v) & (row8 > mi - lo))
        cnts[hq * n_grp + j] = cnts[hq * n_grp + j] + jnp.where(beats, 1.0, 0.0)

    def rank_shell(sh, cnts):
        cnts = list(cnts)
        for hq in range(nq):
            for mi in range(sh * SUBLANES, (sh + 1) * SUBLANES):
                for j in range(sh + 1):
                    count(cnts, hq, mi, j)
            for mi in range(sh * SUBLANES):
                count(cnts, hq, mi, sh)
        return tuple(cnts)

    last_valid = (t0 + QB - 1) // SLC_BLOCK
    cnts = tuple(jnp.zeros((SUBLANES, LANES), F32) for _ in range(nq * n_grp))
    for sh in range(n_grp):
        cnts = lax.cond(sh * SUBLANES <= last_valid, functools.partial(rank_shell, sh), lambda c: c, cnts)
    blocks_per_kt = KT // SLC_BLOCK
    any_sel = jnp.zeros((nb, LANES), F32)
    for hq in range(nq):
        cnt = jnp.concatenate(cnts[hq * n_grp:(hq + 1) * n_grp], axis=0)
        sel = (cnt < float(n_sel)) & valids[hq]
        sel_scr[hq] = jnp.where(sel, 0.0, ninf)
        any_sel = jnp.maximum(any_sel, jnp.where(sel, 1.0, 0.0))
    for kt in range(nb // blocks_per_kt):
        hit = jnp.max(any_sel[kt * blocks_per_kt:(kt + 1) * blocks_per_kt, :])
        flag_ref[kt] = (hit > 0.5).astype(jnp.int32)

    init = (tuple(jnp.full((1, QB), NEG_INF, F32) for _ in range(hpg)),
            tuple(jnp.zeros((1, QB), F32) for _ in range(hpg)))

    def stage_a(kt, slot):
        k0 = pl.multiple_of(kt * KT, KT)
        st = lax.dot_general(ks_ref[pl.ds(k0, KT), :], qs, _NT, preferred_element_type=F32)
        rows = [jnp.concatenate([jnp.broadcast_to(sel_scr[hq, pl.ds(kt * blocks_per_kt + b, 1), :],
                                                  (SLC_BLOCK, LANES)) for hq in range(nq)], axis=1)
                for b in range(blocks_per_kt)]
        causal = jnp.where(kt == kd, (t0 - k0) // QB, KT // QB)
        base = jnp.concatenate(rows, axis=0) + tri_scr[causal]
        off = (k0 - t0).astype(F32)
        tile_max = []
        for h in range(hpg):
            u = st[:, h * QB:(h + 1) * QB] + (alibi_scr[h, 0:KT, :] + base)
            u_scr[slot, h] = u
            tile_max.append(jnp.max(u, axis=0, keepdims=True) + slopes[h] * off)
        return tuple(tile_max)

    def stage_b(kt, slot, tile_max, ms, ls):
        k0 = pl.multiple_of(kt * KT, KT)
        off = (k0 - t0).astype(F32)
        new_m, new_l, alphas, ps = [], [], [], []
        for h in range(hpg):
            mh = jnp.maximum(ms[h], tile_max[h])
            alpha = jnp.exp2(ms[h] - mh)
            p = jnp.exp2(u_scr[slot, h] - (mh - slopes[h] * off))
            new_m.append(mh)
            new_l.append(alpha * ls[h] + jnp.sum(p, axis=0, keepdims=True))
            alphas.append(alpha)
            ps.append(p.astype(BF16))
        pv = lax.dot_general(vs_ref[pl.ds(k0, KT), :], jnp.concatenate(ps, axis=1), _TN,
                             preferred_element_type=F32)
        acc_slc[...] = acc_slc[...] * jnp.concatenate(alphas, axis=1) + pv
        return tuple(new_m), tuple(new_l)

    def slc_step(kt, carry):
        def visit(c):
            ms, ls, tile_max, prev, slot = c
            ms, ls = stage_b(prev, slot, tile_max, ms, ls)
            return ms, ls, stage_a(kt, 1 - slot), kt, 1 - slot
        return lax.cond(flag_ref[kt] > 0, visit, lambda c: c, carry)

    kd = t0 // KT
    acc_slc[...] = jnp.zeros_like(acc_slc)
    zero = jnp.int32(0)
    ms, ls, tile_max, prev, slot = lax.fori_loop(1, kd + 1, slc_step, init + (stage_a(zero, zero), zero, zero))
    _, l_slc = stage_b(prev, slot, tile_max, ms, ls)

    w0 = pl.multiple_of(jnp.maximum(t0 - WINDOW, 0), KB)
    w_off = (t0 - w0) // QB
    st_w = lax.dot_general(kw_ref[pl.ds(w0, WIN_KEYS), :], qs, _NT, preferred_element_type=F32)
    l_win, ps = [], []
    for h in range(hpg):
        u = st_w[:, h * QB:(h + 1) * QB] + winb_scr[w_off, h]
        p = jnp.exp2(u - jnp.max(u, axis=0, keepdims=True))
        l_win.append(jnp.sum(p, axis=0, keepdims=True))
        ps.append(p.astype(BF16))
    acc_win[...] = lax.dot_general(vw_ref[pl.ds(w0, WIN_KEYS), :], jnp.concatenate(ps, axis=1), _TN,
                                   preferred_element_type=F32)

    for h in range(hpg):
        cols = slice(h * QB, (h + 1) * QB)
        g_cmp = jax.nn.sigmoid(gl_ref[0, h:h + 1, :])
        g_slc = jax.nn.sigmoid(gl_ref[1, h:h + 1, :])
        g_win = jax.nn.sigmoid(gl_ref[2, h:h + 1, :])
        oT = (g_cmp * ocmp_scr[:, cols] + (g_slc / l_slc[h]) * acc_slc[:, cols]
              + (g_win / l_win[h]) * acc_win[:, cols])
        o_ref[:, h * HEAD_DIM:(h + 1) * HEAD_DIM] = oT.T.astype(o_ref.dtype)


def sparse_attention(slopes, q, cmp_kv, kv_rest, gate_logits_t, *, nc):
    t, hd = q.shape
    _, n_groups, ncp, _ = cmp_kv.shape
    hpg = hd // HEAD_DIM // n_groups
    nb = t // SLC_BLOCK
    assert t % KT == 0 and t >= WIN_KEYS and ncp == 4 * nb and nb % SUBLANES == 0
    assert KT <= WIN_KEYS and KT % QB == 0 and QB % LANES == 0
    assert WINDOW % QB == 0
    assert (CMP_BLOCK, CMP_STRIDE, SLC_BLOCK) == (32, 16, 64)
    n_sel = min(N_SELECT, nb)
    w = hpg * QB
    kern = functools.partial(_attn_kernel, hpg=hpg, nc=nc, ncp=ncp, nb=nb, n_sel=n_sel)
    kv_col = lambda part: (lambda g, i: (0, part * n_groups + g))
    return pl.pallas_call(
        kern,
        grid=(n_groups, t // QB),
        in_specs=[
            pl.BlockSpec(memory_space=pltpu.SMEM),
            pl.BlockSpec((QB, hpg * HEAD_DIM), lambda g, i: (i, g)),
            pl.BlockSpec((None, None, ncp, HEAD_DIM), lambda g, i: (0, g, 0, 0)),
            pl.BlockSpec((None, None, ncp, HEAD_DIM), lambda g, i: (1, g, 0, 0)),
            pl.BlockSpec((t, HEAD_DIM), kv_col(0)),
            pl.BlockSpec((t, HEAD_DIM), kv_col(1)),
            pl.BlockSpec((t, HEAD_DIM), kv_col(2)),
            pl.BlockSpec((t, HEAD_DIM), kv_col(3)),
            pl.BlockSpec((3, None, hpg, QB), lambda g, i: (0, g, 0, i)),
        ],
        out_specs=pl.BlockSpec((QB, hpg * HEAD_DIM), lambda g, i: (i, g)),
        out_shape=jax.ShapeDtypeStruct((t, hd), BF16),
        scratch_shapes=[
            pltpu.VMEM((QB // LANES, SUBLANES + ncp, LANES), F32),
            pltpu.VMEM((QB // LANES, nb, LANES), F32),
            pltpu.VMEM((QB // LANES, nb, LANES), F32),
            pltpu.VMEM((HEAD_DIM, w), F32),
            pltpu.VMEM((HEAD_DIM, w), F32),
            pltpu.VMEM((HEAD_DIM, w), F32),
            pltpu.VMEM((hpg, WIN_KEYS, QB), F32),
            pltpu.VMEM((WINDOW // QB + 1, hpg, WIN_KEYS, QB), F32),
            pltpu.VMEM((KT // QB + 1, KT, QB), F32),
            pltpu.VMEM((2, hpg, KT, QB), F32),
            pltpu.SMEM((t // KT,), jnp.int32),
        ],
        name="sparse_attention",
        compiler_params=_params("parallel", "arbitrary"),
    )(slopes, q, cmp_kv, cmp_kv, kv_rest, kv_rest, kv_rest, kv_rest, gate_logits_t)


def _inproj_conv_kernel(a_ref, wb_ref, wc_ref, wh_ref, cw_ref, o_ref, halo_ref):
    i = pl.program_id(0)
    j = pl.program_id(1)
    a = a_ref[...]
    b = jnp.dot(a, wb_ref[...], preferred_element_type=F32)
    u = (jnp.dot(a, wc_ref[...], preferred_element_type=F32)
         * jnp.dot(a, wh_ref[...], preferred_element_type=F32))
    wk = cw_ref[...]
    w0, w1, w2 = wk[0:1, :], wk[1:2, :], wk[2:3, :]
    tm = u.shape[0]
    y = b * (w0 * pltpu.roll(u, 2, 0) + w1 * pltpu.roll(u, 1, 0) + w2 * u)
    o_ref[...] = y.astype(o_ref.dtype)
    @pl.when(i == 0)
    def _():
        halo_ref[j] = jnp.zeros(halo_ref.shape[1:], F32)

    ue = jnp.concatenate([halo_ref[j], u[0:SUBLANES, :]], axis=0)
    n2 = 2 * SUBLANES
    u1 = pltpu.roll(ue, 1, 0)[SUBLANES:n2, :]
    u2 = pltpu.roll(ue, 2, 0)[SUBLANES:n2, :]
    y0 = b[0:SUBLANES, :] * (w0 * u2 + w1 * u1 + w2 * u[0:SUBLANES, :])
    o_ref[0:SUBLANES, :] = y0.astype(o_ref.dtype)
    halo_ref[j] = u[tm - SUBLANES:tm, :]


def inproj_short_conv(a, w, conv_w, *, col0, cw, tm=1024, tn=512):
    m, k = a.shape
    tm, tn = _tile(m, tm), _tile(cw, tn)
    while col0 % tn:
        tn = _tile(cw, tn - 1)
    ncb = cw // tn
    jb = col0 // tn
    wcol = lambda part: (lambda i, j: (0, jb + part * ncb + j))
    return pl.pallas_call(
        _inproj_conv_kernel,
        grid=(m // tm, ncb),
        in_specs=[
            pl.BlockSpec((tm, k), lambda i, j: (i, 0)),
            pl.BlockSpec((k, tn), wcol(0)),
            pl.BlockSpec((k, tn), wcol(1)),
            pl.BlockSpec((k, tn), wcol(2)),
            pl.BlockSpec((CONV_K, tn), lambda i, j: (0, j)),
        ],
        out_specs=pl.BlockSpec((tm, tn), lambda i, j: (i, j)),
        out_shape=jax.ShapeDtypeStruct((m, cw), BF16),
        scratch_shapes=[pltpu.VMEM((ncb, SUBLANES, tn), F32)],
        name="inproj_short_conv",
        compiler_params=_params("arbitrary", "arbitrary"),
    )(a, w, w, w, conv_w)


def _mem_kernel(x0_ref, y0_ref, g0_ref, gpre_ref, gpost_ref, wq_ref, km_ref, vm_ref, wo_ref, o_ref):
    x = x0_ref[...] + _rms(y0_ref[...], g0_ref[...])
    xn = _rms(x, gpre_ref[...]).astype(BF16)
    qm = jnp.dot(xn, wq_ref[...], preferred_element_type=F32) * (MEM_HEAD_DIM ** -0.5)
    outs = []
    for h in range(MEM_HEADS):
        cols = slice(h * MEM_HEAD_DIM, (h + 1) * MEM_HEAD_DIM)
        s = lax.dot_general(qm[:, cols].astype(BF16), km_ref[:, cols], _NT, preferred_element_type=F32)
        p = jnp.exp(s - jnp.max(s, axis=-1, keepdims=True))
        p = p * (1.0 / jnp.sum(p, axis=-1, keepdims=True))
        outs.append(jnp.dot(p.astype(BF16), vm_ref[:, cols], preferred_element_type=F32).astype(BF16))
    y = jnp.dot(jnp.concatenate(outs, axis=1), wo_ref[...], preferred_element_type=F32)
    o_ref[...] = x + _rms(y, gpost_ref[...])


def memory_sublayer(x0, y0, g0, gpre, gpost, wq, kvm, wo, tm=256):
    m, d = x0.shape
    mw = wq.shape[1]
    nm = kvm.shape[0]
    tm = _tile(m, tm)
    full = lambda i: (0, 0)
    return pl.pallas_call(
        _mem_kernel,
        grid=(m // tm,),
        in_specs=[
            pl.BlockSpec((tm, d), lambda i: (i, 0)),
            pl.BlockSpec((tm, d), lambda i: (i, 0)),
            pl.BlockSpec((1, d), full),
            pl.BlockSpec((1, d), full),
            pl.BlockSpec((1, d), full),
            pl.BlockSpec((d, mw), full),
            pl.BlockSpec((nm, mw), full),
            pl.BlockSpec((nm, mw), lambda i: (0, 1)),
            pl.BlockSpec((mw, d), full),
        ],
        out_specs=pl.BlockSpec((tm, d), lambda i: (i, 0)),
        out_shape=jax.ShapeDtypeStruct((m, d), F32),
        name="memory_sublayer",
        compiler_params=_params("parallel"),
    )(x0, y0, g0.reshape(1, d), gpre.reshape(1, d), gpost.reshape(1, d), wq, kvm, kvm, wo)


def _mlp_kernel(x_ref, gpre_ref, gpost_ref, wu_ref, wd_ref, o_ref, xn_scr, *, nf):
    f = pl.program_id(1)

    n_chunks = x_ref.shape[0] // NORM_ROWS

    @pl.when(f == 0)
    def _():
        def pre(r, carry):
            rows = pl.ds(pl.multiple_of(r * NORM_ROWS, NORM_ROWS), NORM_ROWS)
            xn_scr[rows, :] = _rms(x_ref[rows, :], gpre_ref[...]).astype(BF16)
            o_ref[rows, :] = jnp.zeros((NORM_ROWS, o_ref.shape[1]), F32)
            return carry
        lax.fori_loop(0, n_chunks, pre, 0)

    hid = jnp.dot(xn_scr[...], wu_ref[...], preferred_element_type=F32)
    hid = jnp.square(jnp.maximum(hid, 0.0))
    o_ref[...] += jnp.dot(hid.astype(BF16), wd_ref[...], preferred_element_type=F32)

    @pl.when(f == nf - 1)
    def _():
        def post(r, carry):
            rows = pl.ds(pl.multiple_of(r * NORM_ROWS, NORM_ROWS), NORM_ROWS)
            o_ref[rows, :] = x_ref[rows, :] + _rms(o_ref[rows, :], gpost_ref[...])
            return carry
        lax.fori_loop(0, n_chunks, post, 0)


def mlp_sublayer(x, gpre, gpost, w_up, w_down, tm=1024, tf=512):
    m, d = x.shape
    dff = w_up.shape[1]
    tm, tf = _tile(m, tm), _tile(dff, tf)
    nf = dff // tf
    w_up = w_up.astype(BF16)
    w_down = w_down.astype(BF16)
    return pl.pallas_call(
        functools.partial(_mlp_kernel, nf=nf),
        grid=(m // tm, nf),
        in_specs=[
            pl.BlockSpec((tm, d), lambda i, f: (i, 0), pipeline_mode=pl.Buffered(1)),
            pl.BlockSpec((1, d), lambda i, f: (0, 0)),
            pl.BlockSpec((1, d), lambda i, f: (0, 0)),
            pl.BlockSpec((d, tf), lambda i, f: (0, f)),
            pl.BlockSpec((tf, d), lambda i, f: (f, 0)),
        ],
        out_specs=pl.BlockSpec((tm, d), lambda i, f: (i, 0), pipeline_mode=pl.Buffered(1)),
        out_shape=jax.ShapeDtypeStruct((m, d), F32),
        scratch_shapes=[pltpu.VMEM((tm, d), BF16)],
        name="mlp_sublayer",
        compiler_params=pltpu.CompilerParams(dimension_semantics=("parallel", "arbitrary"),
                                             vmem_limit_bytes=MLP_VMEM_LIMIT),
    )(x, gpre.reshape(1, d), gpost.reshape(1, d), w_up, w_down)


def _mixer(x, norm_pre, w_in, cmp_k_pe, cmp_k_w1, cmp_k_w2, cmp_v_pe, cmp_v_w1, cmp_v_w2, conv_w, w_out):
    t, d = x.shape
    attn_w = d // 2
    conv_cw = d - attn_w
    n_heads = attn_w // HEAD_DIM
    n_groups = n_heads // 4
    kvw = n_groups * HEAD_DIM
    o_kv = attn_w
    o_gate = o_kv + 6 * kvw
    o_conv = o_gate + 3 * n_heads
    assert w_in.shape[1] == o_conv + 3 * conv_cw

    assert o_gate % LANES == 0 and o_gate + LANES <= w_in.shape[1]
    w_bf = w_in.astype(BF16)
    w_conv = w_in[:, o_conv:].astype(BF16)

    xn = rmsnorm_cast(x, norm_pre)
    q = matmul(xn, w_bf, BF16, scale=HEAD_DIM ** -0.5 * LOG2E, col0=0, n=attn_w)
    kvc = matmul(xn, w_bf, F32, col0=o_kv, n=2 * kvw)
    kv_rest = matmul(xn, w_bf, BF16, col0=o_kv + 2 * kvw, n=4 * kvw)
    o_conv_out = inproj_short_conv(xn, w_conv, conv_w, col0=0, cw=conv_cw)
    gate_logits = matmul(xn, w_bf, F32, col0=o_gate, n=LANES)[:, :3 * n_heads]

    nc = (t - CMP_BLOCK) // CMP_STRIDE + 1
    pe = jnp.stack([cmp_k_pe, cmp_v_pe])
    w1 = jnp.stack([cmp_k_w1, cmp_v_w1]).astype(BF16)
    w2 = jnp.stack([cmp_k_w2, cmp_v_w2]).astype(BF16)
    cmp_kv = compress_tokens(kvc, pe, w1, w2, n_groups)

    gl_t = gate_logits.T.reshape(3, n_groups, n_heads // n_groups, t)
    idx = jnp.arange(1, n_heads + 1, dtype=F32)
    slopes = jnp.exp2(-8.0 * idx / n_heads) * LOG2E
    o_attn = sparse_attention(slopes, q, cmp_kv, kv_rest, gl_t, nc=nc)
    return matmul2(o_attn, o_conv_out, w_out.astype(BF16))


def _memory(x0, y0, g0, mem, norm_pre, norm_kv, norm_post, wq, wk, wv, wo):
    mn = rmsnorm_cast(mem, norm_kv)
    kvm = matmul(mn, jnp.concatenate([wk, wv], axis=1).astype(BF16), BF16)
    return memory_sublayer(x0, y0, g0, norm_pre, norm_post, wq.astype(BF16), kvm, wo.astype(BF16))


def kernel(x, mem, mix_norm_pre, mix_norm_post, w_in, cmp_k_pe, cmp_k_w1, cmp_k_w2, cmp_v_pe, cmp_v_w1,
           cmp_v_w2, conv_w, w_out, mem_norm_pre, mem_norm_kv, mem_norm_post, w_mem_q, w_mem_k, w_mem_v,
           w_mem_o, mlp_norm_pre, mlp_norm_post, w_up, w_down):
    b, t, d = x.shape
    assert b == 1
    h = x[0]
    m = mem[0]
    for l in range(w_in.shape[0]):
        y = _mixer(h, mix_norm_pre[l], w_in[l], cmp_k_pe[l], cmp_k_w1[l], cmp_k_w2[l],
                   cmp_v_pe[l], cmp_v_w1[l], cmp_v_w2[l], conv_w[l], w_out[l])
        h = _memory(h, y, mix_norm_post[l], m, mem_norm_pre[l], mem_norm_kv[l], mem_norm_post[l],
                    w_mem_q[l], w_mem_k[l], w_mem_v[l], w_mem_o[l])
        h = mlp_sublayer(h, mlp_norm_pre[l], mlp_norm_post[l], w_up[l], w_down[l])
    return h[None]
```
